```python
import math
import jax, jax.numpy as jnp
from jax import lax
import numpy as np

D_MODEL = 1024
BATCH = 8
SEQ = 2048
DEPTH = 4
DEC_BATCH = 128
DEC_SEQ = 1
PAST_LEN = 16384
PAGE_SIZE = 128

MIX = D_MODEL
GROUP_W = MIX // 4
GM_HEADS = 4
GM_HEAD_DIM = GROUP_W // GM_HEADS
GM_CHUNK = 128
SSM_CH = 16
SSM_GROUPS = GROUP_W // SSM_CH
SSM_P = 64
CONV_W = 3
WKV_N = 64
WKV_HEADS = GROUP_W // WKV_N
LORA_W = 32
LORA_A = 32
LORA_G = 64
D_TM = 3 * GROUP_W + LORA_W + LORA_A + LORA_G
IN_COLS = 6 * GROUP_W + D_TM
IN_SPLITS = [GROUP_W * i for i in range(1, 7)]
TM_SPLITS = [GROUP_W, 2 * GROUP_W, 3 * GROUP_W, 3 * GROUP_W + LORA_W, 3 * GROUP_W + LORA_W + LORA_A]
D_FF = -(-(8 * D_MODEL) // (3 * 256)) * 256
NORM_EPS = 1e-6
GM_LN_EPS = 1e-5
WKV_LN_EPS = 64e-5

kernel_name = 'hybrid_gmlp_s5_conv_rwkv7_step'


def rmsnorm(x, g):
    xf = x.astype(jnp.float32)
    y = xf * lax.rsqrt(jnp.mean(xf * xf, axis=-1, keepdims=True) + NORM_EPS) * g.astype(jnp.float32)
    return y.astype(x.dtype)


def chunk_gmlp(zu, zv, ln_g, ln_b, ws, bs):
    bsz, t, _ = zu.shape
    u = jax.nn.gelu(zu)
    vf = jax.nn.gelu(zv.astype(jnp.float32))
    mu = jnp.mean(vf, axis=-1, keepdims=True)
    var = jnp.mean(jnp.square(vf - mu), axis=-1, keepdims=True)
    vn = ((vf - mu) * lax.rsqrt(var + GM_LN_EPS) * ln_g.astype(jnp.float32)
          + ln_b.astype(jnp.float32)).astype(zu.dtype)
    tc = min(t, GM_CHUNK)
    nc = t // tc
    wm = ws[:, :tc, :tc] * jnp.tril(jnp.ones((tc, tc), ws.dtype))
    vc = vn.reshape(bsz, nc, tc, GM_HEADS, GM_HEAD_DIM)
    s = jnp.einsum('hts,bcshd->bcthd', wm, vc) + jnp.transpose(bs[:, :tc])[None, None, :, :, None]
    return u * s.reshape(bsz, t, GROUP_W).astype(zu.dtype), vn


def s5_ssm(zu, s_re0, s_im0, a_re, a_im, log_dt, b_re, b_im, c_re, c_im, d, glu_w, glu_b):
    f32 = jnp.float32
    bsz, t, _ = zu.shape
    u = zu.astype(f32).reshape(bsz, t, SSM_GROUPS, SSM_CH)
    lam_re = jnp.minimum(a_re.astype(f32), -1e-4)
    lam_im = a_im.astype(f32)
    dt = jnp.exp(log_dt.astype(f32))
    mag = jnp.exp(lam_re * dt)
    lb_re = mag * jnp.cos(lam_im * dt)
    lb_im = mag * jnp.sin(lam_im * dt)
    den = lam_re * lam_re + lam_im * lam_im
    f_re = ((lb_re - 1.0) * lam_re + lb_im * lam_im) / den
    f_im = (lb_im * lam_re - (lb_re - 1.0) * lam_im) / den
    br, bi = b_re.astype(f32), b_im.astype(f32)
    bb_re = f_re[..., None] * br - f_im[..., None] * bi
    bb_im = f_re[..., None] * bi + f_im[..., None] * br
    bu_re = jnp.einsum('btgh,gph->tbgp', u, bb_re)
    bu_im = jnp.einsum('btgh,gph->tbgp', u, bb_im)
    a_r = jnp.broadcast_to(lb_re, (t, 1, SSM_GROUPS, SSM_P))
    a_i = jnp.broadcast_to(lb_im, (t, 1, SSM_GROUPS, SSM_P))

    def combine(e1, e2):
        a1r, a1i, b1r, b1i = e1
        a2r, a2i, b2r, b2i = e2
        return (a2r * a1r - a2i * a1i, a2r * a1i + a2i * a1r,
                a2r * b1r - a2i * b1i + b2r, a2r * b1i + a2i * b1r + b2i)

    pa_r, pa_i, sr, si = lax.associative_scan(combine, (a_r, a_i, bu_re, bu_im), axis=0)
    s0r = s_re0.astype(f32)[None]
    s0i = s_im0.astype(f32)[None]
    st_re = pa_r * s0r - pa_i * s0i + sr
    st_im = pa_r * s0i + pa_i * s0r + si
    y = (jnp.einsum('tbgp,ghp->btgh', st_re, c_re.astype(f32))
         - jnp.einsum('tbgp,ghp->btgh', st_im, c_im.astype(f32))
         + d.astype(f32) * u)
    y = jax.nn.gelu(y.reshape(bsz, t, GROUP_W))
    out = y * jax.nn.sigmoid(y @ glu_w.astype(f32) + glu_b.astype(f32))
    return out.astype(zu.dtype), st_re[-1], st_im[-1]


def short_conv(zx, zb, zc, buf, conv_w, conv_b):
    t = zx.shape[1]
    z = zc * zx
    zp = jnp.concatenate([buf.astype(z.dtype), z], axis=1)
    y = conv_b + sum(conv_w[j] * zp[:, j:j + t] for j in range(CONV_W))
    return zb * y, zp[:, t:]


def rwkv7(zd, prev, s0, mu, w0, w2, a0, a2, g2, k_k, k_a, r_k, ln_g, ln_b):
    f32 = jnp.float32
    bsz, t, _ = zd.shape
    zprev = jnp.concatenate([prev[:, None, :].astype(zd.dtype), zd[:, :-1]], axis=1)
    zs = zd + mu * (zprev - zd)
    r, k, v, xw, xa, xg = jnp.split(zs, TM_SPLITS, axis=-1)
    w = -jax.nn.softplus(-(w0 + jnp.tanh(xw) @ w2).astype(f32)) - 0.5
    decay = jnp.exp(-jnp.exp(w))
    a = jax.nn.sigmoid((a0 + xa @ a2).astype(f32))
    g = (jax.nn.sigmoid(xg) @ g2).astype(f32)
    hs = (bsz, t, WKV_HEADS, WKV_N)
    rf = r.astype(f32).reshape(hs)
    kf = k.astype(f32).reshape(hs)
    vf = v.astype(f32).reshape(hs)
    ah = a.reshape(hs)
    dh = decay.reshape(hs)
    kk = kf * k_k.astype(f32).reshape(WKV_HEADS, WKV_N)
    kk = kk / jnp.maximum(jnp.linalg.norm(kk, axis=-1, keepdims=True), 1e-12)
    kf = kf * (1.0 + (ah - 1.0) * k_a.astype(f32).reshape(WKV_HEADS, WKV_N))
    a_vec = -kk
    b_vec = kk * ah

    def step(S, inp):
        r_t, d_t, k_t, v_t, a_t, b_t = inp
        sa = jnp.einsum('bhvk,bhk->bhv', S, a_t)
        S = (S * d_t[:, :, None, :] + sa[..., None] * b_t[:, :, None, :]
             + v_t[..., None] * k_t[:, :, None, :])
        return S, jnp.einsum('bhvk,bhk->bhv', S, r_t)

    xs = tuple(jnp.moveaxis(q, 1, 0) for q in (rf, dh, kf, vf, a_vec, b_vec))
    S, o = lax.scan(step, s0.astype(f32), xs)
    o = jnp.moveaxis(o, 0, 1)
    m = jnp.mean(o, axis=-1, keepdims=True)
    var = jnp.mean(jnp.square(o - m), axis=-1, keepdims=True)
    on = ((o - m) * lax.rsqrt(var + WKV_LN_EPS) * ln_g.astype(f32).reshape(WKV_HEADS, WKV_N)
          + ln_b.astype(f32).reshape(WKV_HEADS, WKV_N))
    bonus = jnp.sum(rf * kf * r_k.astype(f32), axis=-1, keepdims=True) * vf
    out = (on + bonus).reshape(bsz, t, GROUP_W) * g
    return out.astype(zd.dtype), zd[:, -1], S


def setup_inputs(seed: int = 0) -> dict:
    key = jax.random.key(seed)
    ks = iter(jax.random.split(key, 64))
    f32 = jnp.float32

    def nrm(shape, scale):
        return jax.random.normal(next(ks), shape, f32) * scale

    def gain(shape):
        return 1.0 + nrm(shape, 0.02)

    def unif(shape, lo, hi):
        return jax.random.uniform(next(ks), shape, f32, lo, hi)

    L, G, P, H, N = DEPTH, SSM_GROUPS, SSM_P, WKV_HEADS, WKV_N
    return {
        'x_prompt': nrm((BATCH, SEQ, D_MODEL), 1.0),
        'x_sample': nrm((DEC_BATCH, DEC_SEQ, D_MODEL), 1.0),
        'state_wkv': nrm((L, DEC_BATCH, H, N, N), 0.3),
        'state_shift': nrm((L, DEC_BATCH, D_TM), 1.0),
        'state_ssm_re': nrm((L, DEC_BATCH, G, P), 0.3),
        'state_ssm_im': nrm((L, DEC_BATCH, G, P), 0.3),
        'state_conv': nrm((L, DEC_BATCH, CONV_W - 1, GROUP_W), 1.0),
        'norm1_g': gain((L, D_MODEL)),
        'w_in': nrm((L, D_MODEL, IN_COLS), D_MODEL ** -0.5),
        'gm_ln_g': gain((L, GROUP_W)),
        'gm_ln_b': nrm((L, GROUP_W), 0.02),
        'gm_ws': nrm((L, GM_HEADS, GM_CHUNK, GM_CHUNK), GM_CHUNK ** -0.5),
        'gm_bs': gain((L, GM_HEADS, GM_CHUNK)),
        'ssm_a_re': -0.5 + nrm((L, G, P), 0.01),
        'ssm_a_im': math.pi * jnp.arange(P, dtype=f32) + nrm((L, G, P), 0.01),
        'ssm_log_dt': unif((L, G, P), math.log(1e-3), math.log(1e-1)),
        'ssm_b_re': nrm((L, G, P, SSM_CH), (2 * SSM_CH) ** -0.5),
        'ssm_b_im': nrm((L, G, P, SSM_CH), (2 * SSM_CH) ** -0.5),
        'ssm_c_re': nrm((L, G, SSM_CH, P), P ** -0.5),
        'ssm_c_im': nrm((L, G, SSM_CH, P), P ** -0.5),
        'ssm_d': nrm((L, G, SSM_CH), 1.0),
        'ssm_glu_w': nrm((L, GROUP_W, GROUP_W), GROUP_W ** -0.5),
        'ssm_glu_b': nrm((L, GROUP_W), 0.02),
        'conv_w': nrm((L, CONV_W, GROUP_W), CONV_W ** -0.5),
        'conv_b': nrm((L, GROUP_W), 0.02),
        'tm_mu': unif((L, D_TM), 0.0, 1.0),
        'tm_w0': unif((L, GROUP_W), -6.0, -1.0),
        'tm_w2': nrm((L, LORA_W, GROUP_W), 0.1 * LORA_W ** -0.5),
        'tm_a0': nrm((L, GROUP_W), 0.1),
        'tm_a2': nrm((L, LORA_A, GROUP_W), 0.1 * LORA_A ** -0.5),
        'tm_g2': nrm((L, LORA_G, GROUP_W), LORA_G ** -0.5),
        'tm_k_k': 0.85 + nrm((L, GROUP_W), 0.02),
        'tm_k_a': gain((L, GROUP_W)),
        'tm_r_k': nrm((L, H, N), 0.1),
        'tm_ln_g': gain((L, GROUP_W)),
        'tm_ln_b': nrm((L, GROUP_W), 0.02),
        'w_out': nrm((L, MIX, D_MODEL), 0.5 * MIX ** -0.5),
        'norm2_g': gain((L, D_MODEL)),
        'ffn_w_gu': nrm((L, D_MODEL, 2 * D_FF), D_MODEL ** -0.5),
        'ffn_w_down': nrm((L, D_FF, D_MODEL), 0.5 * D_FF ** -0.5),
        'norm_f_g': gain((D_MODEL,)),
    }


def reference(x_prompt, x_sample, state_wkv, state_shift, state_ssm_re, state_ssm_im, state_conv,
              norm1_g, w_in, gm_ln_g, gm_ln_b, gm_ws, gm_bs,
              ssm_a_re, ssm_a_im, ssm_log_dt, ssm_b_re, ssm_b_im, ssm_c_re, ssm_c_im, ssm_d,
              ssm_glu_w, ssm_glu_b, conv_w, conv_b,
              tm_mu, tm_w0, tm_w2, tm_a0, tm_a2, tm_g2, tm_k_k, tm_k_a, tm_r_k, tm_ln_g, tm_ln_b,
              w_out, norm2_g, ffn_w_gu, ffn_w_down, norm_f_g):

    def run_layer(x, l, wkv0, shift0, sre0, sim0, conv0):
        h = rmsnorm(x, norm1_g[l])
        z = h @ w_in[l]
        zau, zav, zbu, zcx, zcb, zcc, zd = jnp.split(z, IN_SPLITS, axis=-1)
        ya, v_rows = chunk_gmlp(zau, zav, gm_ln_g[l], gm_ln_b[l], gm_ws[l], gm_bs[l])
        yb, s_re, s_im = s5_ssm(zbu, sre0, sim0, ssm_a_re[l], ssm_a_im[l], ssm_log_dt[l],
                                ssm_b_re[l], ssm_b_im[l], ssm_c_re[l], ssm_c_im[l], ssm_d[l],
                                ssm_glu_w[l], ssm_glu_b[l])
        yc, conv_new = short_conv(zcx, zcb, zcc, conv0, conv_w[l], conv_b[l])
        yd, shift_new, wkv_new = rwkv7(zd, shift0, wkv0, tm_mu[l], tm_w0[l], tm_w2[l], tm_a0[l],
                                       tm_a2[l], tm_g2[l], tm_k_k[l], tm_k_a[l], tm_r_k[l],
                                       tm_ln_g[l], tm_ln_b[l])
        x = x + jnp.concatenate([ya, yb, yc, yd], axis=-1) @ w_out[l]
        gate, up = jnp.split(rmsnorm(x, norm2_g[l]) @ ffn_w_gu[l], 2, axis=-1)
        x = x + (jax.nn.silu(gate) * up) @ ffn_w_down[l]
        return x, wkv_new, shift_new, s_re, s_im, conv_new, v_rows

    bp = x_prompt.shape[0]
    dtp = x_prompt.dtype
    xp, xs = x_prompt, x_sample
    wkv_p, wkv_s, sh_p, sh_s, re_p, re_s, im_p, im_s, cv_p, cv_s, chv_s = ([] for _ in range(11))
    for l in range(DEPTH):
        xp, a1, a2, a3, a4, a5, _ = run_layer(
            xp, l,
            jnp.zeros((bp, WKV_HEADS, WKV_N, WKV_N), dtp),
            jnp.zeros((bp, D_TM), dtp),
            jnp.zeros((bp, SSM_GROUPS, SSM_P), dtp),
            jnp.zeros((bp, SSM_GROUPS, SSM_P), dtp),
            jnp.zeros((bp, CONV_W - 1, GROUP_W), dtp))
        wkv_p.append(a1); sh_p.append(a2); re_p.append(a3); im_p.append(a4); cv_p.append(a5)
        xs, b1, b2, b3, b4, b5, b6 = run_layer(
            xs, l, state_wkv[l], state_shift[l], state_ssm_re[l], state_ssm_im[l], state_conv[l])
        wkv_s.append(b1); sh_s.append(b2); re_s.append(b3); im_s.append(b4); cv_s.append(b5)
        chv_s.append(b6)
    y_prompt = rmsnorm(xp, norm_f_g)
    y_sample = rmsnorm(xs, norm_f_g)
    return (y_prompt, y_sample,
            jnp.stack(wkv_p), jnp.stack(wkv_s),
            jnp.stack(sh_p), jnp.stack(sh_s),
            jnp.stack(re_p), jnp.stack(re_s),
            jnp.stack(im_p), jnp.stack(im_s),
            jnp.stack(cv_p), jnp.stack(cv_s),
            jnp.stack(chv_s))
```

```python
import functools
import math

import jax
import jax.numpy as jnp
from jax import lax
from jax.experimental import pallas as pl
from jax.experimental.pallas import tpu as pltpu

F32 = jnp.float32
BF16 = jnp.bfloat16

D_MODEL = 1024
GROUP_W = 256
GM_CHUNK = 128
GM_HEADS = 4
SSM_CH = 16
SSM_GROUPS = 16
SSM_P = 64
SSM_S = SSM_GROUPS * SSM_P
WKV_N = 64
WKV_HEADS = 4
LORA_PAD = 128
D_TM = 3 * GROUP_W + LORA_PAD
IN_COLS = 6 * GROUP_W + D_TM
D_FF = 2816
NORM_EPS = 1e-6
GM_LN_EPS = 1e-5
WKV_LN_EPS = 64e-5

WKV_CHUNK = 64
S5_TCHUNK = 128
VMEM_LIMIT = 56 * 1024 * 1024


def _cparams(sem):
    return pltpu.CompilerParams(dimension_semantics=sem, vmem_limit_bytes=VMEM_LIMIT)


def _full(shape):
    n = len(shape)
    return pl.BlockSpec(shape, lambda *_: (0,) * n)


def _dot(a, b):
    return jnp.dot(a.astype(BF16), b.astype(BF16), preferred_element_type=F32)


def _dot_nt(a, b):
    return lax.dot_general(a.astype(BF16), b.astype(BF16), (((1,), (1,)), ((), ())),
                           preferred_element_type=F32)


def _dot_tn(a, b):
    return lax.dot_general(a.astype(BF16), b.astype(BF16), (((0,), (0,)), ((), ())),
                           preferred_element_type=F32)


def _dot_split(x, ones_bf16):
    hi = x.astype(BF16)
    lo = (x - hi.astype(F32)).astype(BF16)
    return (jnp.dot(hi, ones_bf16, preferred_element_type=F32)
            + jnp.dot(lo, ones_bf16, preferred_element_type=F32))


def _rms(x, g):
    return x * lax.rsqrt(jnp.mean(x * x, axis=-1, keepdims=True) + NORM_EPS) * g


def _softplus(y):
    return jnp.maximum(y, 0.0) + jnp.log1p(jnp.exp(-jnp.abs(y)))


def _inproj_kernel(x_ref, g_ref, w_ref, za_ref, zb_ref, zc_ref, zd_ref):
    h = _rms(x_ref[...], g_ref[...])
    z = jnp.dot(h.astype(BF16), w_ref[...], preferred_element_type=F32)
    za_ref[...] = z[:, 0:2 * GROUP_W]
    zb_ref[...] = z[:, 2 * GROUP_W:3 * GROUP_W]
    zc_ref[...] = z[:, 3 * GROUP_W:6 * GROUP_W]
    zd_ref[...] = z[:, 6 * GROUP_W:]


def _inproj(x, g, w, tm):
    rows = x.shape[0]
    widths = (2 * GROUP_W, GROUP_W, 3 * GROUP_W, D_TM)
    return pl.pallas_call(
        _inproj_kernel,
        grid=(rows // tm,),
        in_specs=[pl.BlockSpec((tm, D_MODEL), lambda i: (i, 0)),
                  _full((1, D_MODEL)),
                  _full((D_MODEL, IN_COLS))],
        out_specs=[pl.BlockSpec((tm, wd), lambda i: (i, 0)) for wd in widths],
        out_shape=[jax.ShapeDtypeStruct((rows, wd), F32) for wd in widths],
        compiler_params=_cparams(("parallel",)),
    )(x, g, w)


def _gm_norm(zav, ln_g, ln_b):
    vf = jax.nn.gelu(zav)
    mu = jnp.mean(vf, axis=-1, keepdims=True)
    var = jnp.mean(jnp.square(vf - mu), axis=-1, keepdims=True)
    return (vf - mu) * lax.rsqrt(var + GM_LN_EPS) * ln_g + ln_b


def _gmlp_conv_kernel(za_ref, zc_ref, lng_ref, lnb_ref, wcat_ref, bias_ref, cw_ref, cb_ref,
                      ya_ref, yc_ref, tail_ref, prev_ref, *, tile):
    j = pl.program_id(1)

    @pl.when(j == 0)
    def _():
        prev_ref[...] = jnp.zeros_like(prev_ref)

    za = za_ref[...]
    u = jax.nn.gelu(za[:, :GROUP_W])
    vn = _gm_norm(za[:, GROUP_W:], lng_ref[...], lnb_ref[...])
    kc = GM_HEADS * GM_CHUNK
    t_i = lax.broadcasted_iota(jnp.int32, (GM_CHUNK, kc), 0)
    s_i = lax.broadcasted_iota(jnp.int32, (GM_CHUNK, kc), 1) % GM_CHUNK
    wm = jnp.where(s_i <= t_i, wcat_ref[...], 0.0).astype(BF16)
    r_h = lax.broadcasted_iota(jnp.int32, (kc, GROUP_W), 0) // GM_CHUNK
    c_h = lax.broadcasted_iota(jnp.int32, (kc, GROUP_W), 1) // (GROUP_W // GM_HEADS)
    head_mask = r_h == c_h
    for c in range(tile // GM_CHUNK):
        rows = slice(c * GM_CHUNK, (c + 1) * GM_CHUNK)
        vc = vn[rows].astype(BF16)
        rhs = jnp.where(head_mask, jnp.concatenate([vc] * GM_HEADS, axis=0), jnp.zeros((), BF16))
        s = jnp.dot(wm, rhs, preferred_element_type=F32) + bias_ref[...]
        ya_ref[rows, :] = u[rows] * s

    zc = zc_ref[...]
    z = zc[:, 2 * GROUP_W:] * zc[:, :GROUP_W]
    row = lax.broadcasted_iota(jnp.int32, z.shape, 0)
    prev = prev_ref[...]
    z1 = jnp.where(row == 0, prev[7:8], pltpu.roll(z, 1, 0))
    z2 = jnp.where(row == 0, prev[6:7], jnp.where(row == 1, prev[7:8], pltpu.roll(z, 2, 0)))
    cw = cw_ref[...]
    y = cb_ref[...] + cw[0:1] * z2 + cw[1:2] * z1 + cw[2:3] * z
    yc_ref[...] = zc[:, GROUP_W:2 * GROUP_W] * y
    prev_ref[...] = z[tile - 8:]
    tail_ref[...] = z[tile - 8:]


def _gmlp_conv(za, zc, lng, lnb, wcat, bias, cw, cb, batch, seq, tile):
    nt = seq // tile
    rows = batch * seq
    row_blk = lambda wd: pl.BlockSpec((tile, wd), lambda b, j: (b * nt + j, 0))
    return pl.pallas_call(
        functools.partial(_gmlp_conv_kernel, tile=tile),
        grid=(batch, nt),
        in_specs=[row_blk(2 * GROUP_W), row_blk(3 * GROUP_W),
                  _full((1, GROUP_W)), _full((1, GROUP_W)),
                  _full((GM_CHUNK, GM_HEADS * GM_CHUNK)), _full((GM_CHUNK, GROUP_W)),
                  _full((3, GROUP_W)), _full((1, GROUP_W))],
        out_specs=[row_blk(GROUP_W), row_blk(GROUP_W),
                   pl.BlockSpec((None, 8, GROUP_W), lambda b, j: (b, 0, 0))],
        out_shape=[jax.ShapeDtypeStruct((rows, GROUP_W), F32),
                   jax.ShapeDtypeStruct((rows, GROUP_W), F32),
                   jax.ShapeDtypeStruct((batch, 8, GROUP_W), F32)],
        scratch_shapes=[pltpu.VMEM((8, GROUP_W), F32)],
        compiler_params=_cparams(("parallel", "arbitrary")),
    )(za, zc, lng, lnb, wcat, bias, cw, cb)


def _s5_prep_kernel(are_ref, aim_ref, ldt_ref, bre_ref, bim_ref, cre_ref, cim_ref,
                    lam_ref, bb_ref, cc_ref):
    lam_re = jnp.minimum(are_ref[...], -1e-4)
    lam_im = aim_ref[...]
    dt = jnp.exp(ldt_ref[...])
    mag = jnp.exp(lam_re * dt)
    lb_re = mag * jnp.cos(lam_im * dt)
    lb_im = mag * jnp.sin(lam_im * dt)
    den = lam_re * lam_re + lam_im * lam_im
    f_re = ((lb_re - 1.0) * lam_re + lb_im * lam_im) / den
    f_im = (lb_im * lam_re - (lb_re - 1.0) * lam_im) / den
    lam_ref[0:1, :] = lb_re
    lam_ref[1:2, :] = lb_im
    br, bi = bre_ref[...], bim_ref[...]
    grp_r = lax.broadcasted_iota(jnp.int32, (GROUP_W, SSM_S), 0) // SSM_CH
    grp_c = lax.broadcasted_iota(jnp.int32, (GROUP_W, SSM_S), 1) // SSM_P
    m = grp_r == grp_c
    bb_ref[:, :SSM_S] = jnp.where(m, f_re * br - f_im * bi, 0.0).astype(BF16)
    bb_ref[:, SSM_S:] = jnp.where(m, f_re * bi + f_im * br, 0.0).astype(BF16)
    grp_r2 = lax.broadcasted_iota(jnp.int32, (SSM_S, GROUP_W), 0) // SSM_P
    grp_c2 = lax.broadcasted_iota(jnp.int32, (SSM_S, GROUP_W), 1) // SSM_CH
    m2 = grp_r2 == grp_c2
    cc_ref[:SSM_S, :] = jnp.where(m2, cre_ref[...], 0.0).astype(BF16)
    cc_ref[SSM_S:, :] = jnp.where(m2, -cim_ref[...], 0.0).astype(BF16)


def _s5_prep(a_re, a_im, log_dt, b_re, b_im, c_re, c_im):
    flat = lambda p: p.reshape(1, SSM_S)
    b_exp = lambda b: jnp.tile(jnp.transpose(b, (2, 0, 1)).reshape(SSM_CH, SSM_S), (SSM_GROUPS, 1))
    c_exp = lambda c: jnp.tile(jnp.transpose(c, (0, 2, 1)).reshape(SSM_S, SSM_CH), (1, SSM_GROUPS))
    return pl.pallas_call(
        _s5_prep_kernel,
        out_shape=[jax.ShapeDtypeStruct((2, SSM_S), F32),
                   jax.ShapeDtypeStruct((GROUP_W, 2 * SSM_S), BF16),
                   jax.ShapeDtypeStruct((2 * SSM_S, GROUP_W), BF16)],
        compiler_params=pltpu.CompilerParams(vmem_limit_bytes=VMEM_LIMIT),
    )(flat(a_re), flat(a_im), flat(log_dt), b_exp(b_re), b_exp(b_im), c_exp(c_re), c_exp(c_im))


def _s5_output(st, u, cc, d, glu_w, glu_b):
    y = jnp.dot(st.astype(BF16), cc, preferred_element_type=F32) + d * u
    y = jax.nn.gelu(y)
    return y * jax.nn.sigmoid(jnp.dot(y.astype(BF16), glu_w, preferred_element_type=F32) + glu_b)


def _s5_kernel(u_ref, lam_ref, bb_ref, cc_ref, d_ref, gw_ref, gb_ref, y_ref, fin_ref,
               bu_ref, st_ref, *, batch, tsteps):
    @pl.when(pl.program_id(0) == 0)
    def _():
        st_ref[...] = jnp.zeros_like(st_ref)

    u = u_ref[...]
    bu_ref[...] = jnp.dot(u.astype(BF16), bb_ref[...], preferred_element_type=F32)
    lam = lam_ref[...]
    lr = jnp.broadcast_to(lam[0:1], (batch, SSM_S))
    li = jnp.broadcast_to(lam[1:2], (batch, SSM_S))

    def step(t, carry):
        s_re, s_im = carry
        rows = pl.ds(pl.multiple_of(t * batch, batch), batch)
        n_re = lr * s_re - li * s_im + bu_ref[rows, :SSM_S]
        n_im = lr * s_im + li * s_re + bu_ref[rows, SSM_S:]
        bu_ref[rows, :SSM_S] = n_re
        bu_ref[rows, SSM_S:] = n_im
        return n_re, n_im

    st0 = st_ref[...]
    s_re, s_im = lax.fori_loop(0, tsteps, step, (st0[:, :SSM_S], st0[:, SSM_S:]), unroll=4)
    st_ref[:, :SSM_S] = s_re
    st_ref[:, SSM_S:] = s_im
    fin_ref[:, :SSM_S] = s_re
    fin_ref[:, SSM_S:] = s_im
    y_ref[...] = _s5_output(bu_ref[...], u, cc_ref[...], d_ref[...], gw_ref[...], gb_ref[...])


def _s5(u_tm, lam, bb, cc, d, glu_w, glu_b, batch, seq):
    tsteps = min(S5_TCHUNK, seq)
    rows = tsteps * batch
    return pl.pallas_call(
        functools.partial(_s5_kernel, batch=batch, tsteps=tsteps),
        grid=(seq // tsteps,),
        in_specs=[pl.BlockSpec((rows, GROUP_W), lambda i: (i, 0)),
                  _full((2, SSM_S)), _full((GROUP_W, 2 * SSM_S)), _full((2 * SSM_S, GROUP_W)),
                  _full((1, GROUP_W)), _full((GROUP_W, GROUP_W)), _full((1, GROUP_W))],
        out_specs=[pl.BlockSpec((rows, GROUP_W), lambda i: (i, 0)),
                   _full((batch, 2 * SSM_S))],
        out_shape=[jax.ShapeDtypeStruct((seq * batch, GROUP_W), F32),
                   jax.ShapeDtypeStruct((batch, 2 * SSM_S), F32)],
        scratch_shapes=[pltpu.VMEM((rows, 2 * SSM_S), F32),
                        pltpu.VMEM((batch, 2 * SSM_S), F32)],
        compiler_params=_cparams(("arbitrary",)),
    )(u_tm, lam, bb, cc, d, glu_w, glu_b)


def _wkv_inputs(zd, zprev, p, bd_ones):
    zs = zd + p["mu"] * (zprev - zd)
    r = zs[:, 0:GROUP_W]
    k = zs[:, GROUP_W:2 * GROUP_W]
    v = zs[:, 2 * GROUP_W:3 * GROUP_W]
    lora = zs[:, 3 * GROUP_W:]
    w = -_softplus(-(p["w0"] + _dot(jnp.tanh(lora), p["w2"]))) - 0.5
    logd = -jnp.exp(w)
    a = jax.nn.sigmoid(p["a0"] + _dot(lora, p["a2"]))
    g = _dot(jax.nn.sigmoid(lora), p["g2"])
    kk = k * p["k_k"]
    nrm = jnp.sqrt(_dot_split(kk * kk, bd_ones))
    kk = kk / jnp.maximum(nrm, 1e-12)
    k2 = k * (1.0 + (a - 1.0) * p["k_a"])
    return r, logd, k2, v, kk, a, g


def _wkv_output(o, r, k2, v, g, p, bd_ones):
    inv_n = 1.0 / WKV_N
    m = _dot_split(o, bd_ones) * inv_n
    var = _dot_split(jnp.square(o - m), bd_ones) * inv_n
    on = (o - m) * lax.rsqrt(var + WKV_LN_EPS) * p["ln_g"] + p["ln_b"]
    bonus = _dot_split(r * k2 * p["r_k"], bd_ones) * v
    return (on + bonus) * g


_WKV_PARAMS = ("mu", "w0", "w2", "a0", "a2", "g2", "k_k", "k_a", "r_k", "ln_g", "ln_b")
_WKV_PARAM_SHAPES = {"mu": (1, D_TM), "w2": (LORA_PAD, GROUP_W), "a2": (LORA_PAD, GROUP_W),
                     "g2": (LORA_PAD, GROUP_W)}


def _wkv_param_specs():
    return [_full(_WKV_PARAM_SHAPES.get(n, (1, GROUP_W))) for n in _WKV_PARAMS]


def _bd_mask(n):
    hr = lax.broadcasted_iota(jnp.int32, (n, n), 0) // (n // WKV_HEADS)
    hc = lax.broadcasted_iota(jnp.int32, (n, n), 1) // (n // WKV_HEADS)
    return hr == hc


def _expand(xp, mask):
    return jnp.where(mask, jnp.concatenate([xp] * WKV_HEADS, axis=0), 0.0)


def _wkv_chunk(r, logd, k2, v, kk, a, s_bd, tri_ones, bd256):
    c = WKV_CHUNK
    t_i = lax.broadcasted_iota(jnp.int32, (c, GROUP_W), 0)
    s_i = lax.broadcasted_iota(jnp.int32, (c, GROUP_W), 1) % c
    cum = jnp.dot(tri_ones, logd, preferred_element_type=F32, precision=lax.Precision.HIGHEST)
    g_last = cum[c - 1:c]
    e_pos = jnp.exp(cum)
    e_neg = jnp.exp(-cum)
    e_end = jnp.exp(g_last - cum)
    bvec = kk * a
    a_t = -kk * jnp.exp(cum - logd)
    r_t = r * e_pos
    b_t = bvec * e_neg
    k_t = k2 * e_neg
    b_h = bvec * e_end
    k_h = k2 * e_end

    ar = jnp.concatenate([a_t, r_t], axis=0)
    p_b = _dot_nt(ar, _expand(b_t, bd256))
    p_k = _dot_nt(ar, _expand(k_t, bd256))
    strict = s_i < t_i
    incl = s_i <= t_i
    l_p = jnp.where(strict, p_b[:c], 0.0)
    aak = jnp.where(strict, p_k[:c], 0.0)
    rb = jnp.where(incl, p_b[c:], 0.0)
    rk = jnp.where(incl, p_k[c:], 0.0)

    t_p = jnp.where(s_i == t_i, 1.0, 0.0)
    m = 1
    while m < c:
        off = (t_i // (2 * m) == s_i // (2 * m)) & (t_i % (2 * m) >= m) & (s_i % (2 * m) < m)
        lb = jnp.where(off, l_p, 0.0)
        t_p = t_p + _dot(_dot(t_p, _expand(lb, bd256)), _expand(t_p, bd256))
        m *= 2

    v_bd = _expand(v, bd256)
    a2 = _dot(t_p, _expand(a_t, bd256))
    w0 = _dot(t_p, _expand(_dot(aak, v_bd), bd256))
    w = _dot_nt(a2, s_bd) + w0
    o = _dot_nt(r_t, s_bd) + _dot(rb, _expand(w, bd256)) + _dot(rk, v_bd)
    upd = _dot_tn(jnp.concatenate([w, v], axis=0), jnp.concatenate([b_h, k_h], axis=0))
    s_new = s_bd * jnp.exp(g_last) + jnp.where(bd256, upd, 0.0)
    return o, s_new


def _rwkv_kernel(zd_ref, *rest):
    prm = {n: ref[...] for n, ref in zip(_WKV_PARAMS, rest)}
    y_ref, sfin_ref, prev_ref, s_ref = rest[len(_WKV_PARAMS):]
    c = WKV_CHUNK

    @pl.when(pl.program_id(1) == 0)
    def _():
        prev_ref[...] = jnp.zeros_like(prev_ref)
        s_ref[...] = jnp.zeros_like(s_ref)

    bd256 = _bd_mask(GROUP_W)
    bd_ones = jnp.where(bd256, 1.0, 0.0).astype(BF16)
    tri_ones = jnp.where(lax.broadcasted_iota(jnp.int32, (c, c), 1)
                         <= lax.broadcasted_iota(jnp.int32, (c, c), 0), 1.0, 0.0)
    zd = zd_ref[...]
    row = lax.broadcasted_iota(jnp.int32, zd.shape, 0)
    zprev = jnp.where(row == 0, prev_ref[7:8], pltpu.roll(zd, 1, 0))
    r, logd, k2, v, kk, a, g = _wkv_inputs(zd, zprev, prm, bd_ones)
    o, s_new = _wkv_chunk(r, logd, k2, v, kk, a, s_ref[...], tri_ones, bd256)
    y_ref[...] = _wkv_output(o, r, k2, v, g, prm, bd_ones)
    s_ref[...] = s_new
    sfin_ref[...] = s_new
    prev_ref[...] = zd[c - 8:]


def _rwkv(zd, prm, batch, seq):
    c = WKV_CHUNK
    nc = seq // c
    return pl.pallas_call(
        _rwkv_kernel,
        grid=(batch, nc),
        in_specs=[pl.BlockSpec((c, D_TM), lambda b, j: (b * nc + j, 0))] + _wkv_param_specs(),
        out_specs=[pl.BlockSpec((c, GROUP_W), lambda b, j: (b * nc + j, 0)),
                   pl.BlockSpec((None, GROUP_W, GROUP_W), lambda b, j: (b, 0, 0))],
        out_shape=[jax.ShapeDtypeStruct((batch * seq, GROUP_W), F32),
                   jax.ShapeDtypeStruct((batch, GROUP_W, GROUP_W), F32)],
        scratch_shapes=[pltpu.VMEM((8, D_TM), F32), pltpu.VMEM((GROUP_W, GROUP_W), F32)],
        compiler_params=_cparams(("parallel", "arbitrary")),
    )(zd, *[prm[n] for n in _WKV_PARAMS])


def _sample_mix_kernel(za_ref, zb_ref, zc_ref, zd_ref, shift_ref, s_ref, ssm_ref, cv0_ref, cv1_ref,
                       lng_ref, lnb_ref, w00_ref, b0_ref, lam_ref, bb_ref, cc_ref, d_ref, gw_ref,
                       gb_ref, cw_ref, cb_ref, *rest):
    prm = {n: ref[...] for n, ref in zip(_WKV_PARAMS, rest)}
    ya_ref, yb_ref, yc_ref, yd_ref, vn_ref, snew_ref, ssmnew_ref, znew_ref = rest[len(_WKV_PARAMS):]

    za = za_ref[...]
    vn = _gm_norm(za[:, GROUP_W:], lng_ref[...], lnb_ref[...])
    vn_ref[...] = vn
    ya_ref[...] = jax.nn.gelu(za[:, :GROUP_W]) * (w00_ref[...] * vn + b0_ref[...])

    u = zb_ref[...]
    bu = jnp.dot(u.astype(BF16), bb_ref[...], preferred_element_type=F32)
    lam = lam_ref[...]
    lr, li = lam[0:1], lam[1:2]
    st = ssm_ref[...]
    s_re, s_im = st[:, :SSM_S], st[:, SSM_S:]
    st_new = jnp.concatenate([lr * s_re - li * s_im + bu[:, :SSM_S],
                              lr * s_im + li * s_re + bu[:, SSM_S:]], axis=1)
    ssmnew_ref[...] = st_new
    yb_ref[...] = _s5_output(st_new, u, cc_ref[...], d_ref[...], gw_ref[...], gb_ref[...])

    zc = zc_ref[...]
    z = zc[:, 2 * GROUP_W:] * zc[:, :GROUP_W]
    cw = cw_ref[...]
    y = cb_ref[...] + cw[0:1] * cv0_ref[...] + cw[1:2] * cv1_ref[...] + cw[2:3] * z
    yc_ref[...] = zc[:, GROUP_W:2 * GROUP_W] * y
    znew_ref[...] = z

    bd256 = _bd_mask(GROUP_W)
    bd_ones = jnp.where(bd256, 1.0, 0.0).astype(BF16)
    zd = zd_ref[...]
    r, logd, k2, v, kk, a, g = _wkv_inputs(zd, shift_ref[...], prm, bd_ones)
    bt = zd.shape[0]
    s = s_ref[...]
    eye4 = (lax.broadcasted_iota(jnp.int32, (WKV_N, GROUP_W), 0)
            == lax.broadcasted_iota(jnp.int32, (WKV_N, GROUP_W), 1) % WKV_N)

    def head_sum(x3):
        return _dot_split(x3.reshape(bt * WKV_N, GROUP_W), bd_ones).reshape(bt, WKV_N, GROUP_W)

    sa = head_sum(s * (-kk)[:, None, :])
    vcol = head_sum(jnp.where(eye4[None], v[:, None, :], 0.0))
    s_new = (s * jnp.exp(logd)[:, None, :] + sa * (kk * a)[:, None, :] + vcol * k2[:, None, :])
    snew_ref[...] = s_new
    o_rep = head_sum(s_new * r[:, None, :])
    o = jnp.sum(jnp.where(eye4[None], o_rep, 0.0), axis=1)
    yd_ref[...] = _wkv_output(o, r, k2, v, g, prm, bd_ones)


def _sample_mix(za, zb, zc, zd, shift, s_t, ssm, cv0, cv1, gm, s5p, cw, cb, prm, bt):
    rows = za.shape[0]
    row_blk = lambda wd: pl.BlockSpec((bt, wd), lambda i: (i, 0))
    s_blk = pl.BlockSpec((bt, WKV_N, GROUP_W), lambda i: (i, 0, 0))
    vec = _full((1, GROUP_W))
    lam, bb, cc, d, glu_w, glu_b = s5p
    lng, lnb, w00, b0 = gm
    return pl.pallas_call(
        _sample_mix_kernel,
        grid=(rows // bt,),
        in_specs=[row_blk(2 * GROUP_W), row_blk(GROUP_W), row_blk(3 * GROUP_W), row_blk(D_TM),
                  row_blk(D_TM), s_blk, row_blk(2 * SSM_S), row_blk(GROUP_W), row_blk(GROUP_W),
                  vec, vec, vec, vec,
                  _full((2, SSM_S)), _full((GROUP_W, 2 * SSM_S)), _full((2 * SSM_S, GROUP_W)),
                  vec, _full((GROUP_W, GROUP_W)), vec, _full((3, GROUP_W)), vec] + _wkv_param_specs(),
        out_specs=[row_blk(GROUP_W)] * 5 + [s_blk, row_blk(2 * SSM_S), row_blk(GROUP_W)],
        out_shape=[jax.ShapeDtypeStruct((rows, GROUP_W), F32)] * 5
        + [jax.ShapeDtypeStruct((rows, WKV_N, GROUP_W), F32),
           jax.ShapeDtypeStruct((rows, 2 * SSM_S), F32),
           jax.ShapeDtypeStruct((rows, GROUP_W), F32)],
        compiler_params=_cparams(("parallel",)),
    )(za, zb, zc, zd, shift, s_t, ssm, cv0, cv1, lng, lnb, w00, b0, lam, bb, cc, d, glu_w, glu_b,
      cw, cb, *[prm[n] for n in _WKV_PARAMS])


FF_SPLIT = 2


def _out_ffn_kernel(x_ref, ya_ref, yb_ref, yc_ref, yd_ref, wo_ref, g2_ref, wg_ref, wu_ref, wd_ref,
                    gf_ref, o_ref, h_ref, *, final):
    c = pl.program_id(1)

    @pl.when(c == 0)
    def _():
        x = x_ref[...]
        for i, y_ref in enumerate((ya_ref, yb_ref, yc_ref, yd_ref)):
            x = x + jnp.dot(y_ref[...].astype(BF16), wo_ref[i * GROUP_W:(i + 1) * GROUP_W, :],
                            preferred_element_type=F32)
        o_ref[...] = x
        h_ref[...] = _rms(x, g2_ref[...]).astype(BF16)

    h = h_ref[...]
    gate = jnp.dot(h, wg_ref[...], preferred_element_type=F32)
    up = jnp.dot(h, wu_ref[...], preferred_element_type=F32)
    act = (gate * jax.nn.sigmoid(gate) * up).astype(BF16)
    o_ref[...] += jnp.dot(act, wd_ref[...], preferred_element_type=F32)

    if final:
        @pl.when(c == FF_SPLIT - 1)
        def _():
            o_ref[...] = _rms(o_ref[...], gf_ref[...])


def _out_ffn(x, ys, wo, g2, wgu, wd, gf, tm, final):
    rows = x.shape[0]
    fc = D_FF // FF_SPLIT
    row_blk = lambda wd_: pl.BlockSpec((tm, wd_), lambda i, c: (i, 0))
    const = lambda shape: pl.BlockSpec(shape, lambda i, c: (0, 0))
    return pl.pallas_call(
        functools.partial(_out_ffn_kernel, final=final),
        grid=(rows // tm, FF_SPLIT),
        in_specs=[row_blk(D_MODEL)] + [row_blk(GROUP_W)] * 4
        + [const((D_MODEL, D_MODEL)), const((1, D_MODEL)),
           pl.BlockSpec((D_MODEL, fc), lambda i, c: (0, c)),
           pl.BlockSpec((D_MODEL, fc), lambda i, c: (0, FF_SPLIT + c)),
           pl.BlockSpec((fc, D_MODEL), lambda i, c: (c, 0)),
           const((1, D_MODEL))],
        out_specs=row_blk(D_MODEL),
        out_shape=jax.ShapeDtypeStruct((rows, D_MODEL), F32),
        scratch_shapes=[pltpu.VMEM((tm, D_MODEL), BF16)],
        compiler_params=_cparams(("parallel", "arbitrary")),
    )(x, *ys, wo, g2, wgu, wgu, wd, gf)


def _pad_lora(w, start):
    return jnp.zeros((LORA_PAD, GROUP_W), BF16).at[start:start + w.shape[0]].set(w.astype(BF16))


def kernel(x_prompt, x_sample, state_wkv, state_shift, state_ssm_re, state_ssm_im, state_conv,
           norm1_g, w_in, gm_ln_g, gm_ln_b, gm_ws, gm_bs,
           ssm_a_re, ssm_a_im, ssm_log_dt, ssm_b_re, ssm_b_im, ssm_c_re, ssm_c_im, ssm_d,
           ssm_glu_w, ssm_glu_b, conv_w, conv_b,
           tm_mu, tm_w0, tm_w2, tm_a0, tm_a2, tm_g2, tm_k_k, tm_k_a, tm_r_k, tm_ln_g, tm_ln_b,
           w_out, norm2_g, ffn_w_gu, ffn_w_down, norm_f_g):
    depth = w_in.shape[0]
    bp, seq, _ = x_prompt.shape
    bs = x_sample.shape[0]
    assert x_sample.shape[1] == 1 and seq % GM_CHUNK == 0 and seq % WKV_CHUNK == 0
    head_d = GROUP_W // GM_HEADS
    row = lambda p: p.reshape(1, -1)

    xp = x_prompt.reshape(bp * seq, D_MODEL)
    xs = x_sample.reshape(bs, D_MODEL)
    gf = row(norm_f_g)
    wkv_s_in = jnp.transpose(state_wkv, (0, 1, 3, 2, 4)).reshape(depth, bs, WKV_N, GROUP_W)
    ssm_s_in = jnp.concatenate([state_ssm_re.reshape(depth, bs, SSM_S),
                                state_ssm_im.reshape(depth, bs, SSM_S)], axis=-1)

    outs = {k: [] for k in ("wkv_p", "wkv_s", "sh_p", "sh_s", "ssm_p", "ssm_s", "cv_p", "cv_s", "chv")}
    tm_p = 512 if (bp * seq) % 512 == 0 else GM_CHUNK
    tile_gc = 512 if seq % 512 == 0 else GM_CHUNK
    for l in range(depth):
        w_in_l = w_in[l].astype(BF16)
        wo_l, wgu_l, wd_l = w_out[l].astype(BF16), ffn_w_gu[l].astype(BF16), ffn_w_down[l].astype(BF16)
        g1, g2 = row(norm1_g[l]), row(norm2_g[l])
        lng, lnb = row(gm_ln_g[l]), row(gm_ln_b[l])
        wcat = jnp.transpose(gm_ws[l], (1, 0, 2)).reshape(GM_CHUNK, GM_HEADS * GM_CHUNK)
        bias = jnp.repeat(jnp.transpose(gm_bs[l]), head_d, axis=1)
        w00 = row(jnp.repeat(gm_ws[l][:, 0, 0], head_d))
        b0 = row(jnp.repeat(gm_bs[l][:, 0], head_d))
        lam, bb, cc = _s5_prep(ssm_a_re[l], ssm_a_im[l], ssm_log_dt[l], ssm_b_re[l], ssm_b_im[l],
                               ssm_c_re[l], ssm_c_im[l])
        s5p = (lam, bb, cc, row(ssm_d[l]), ssm_glu_w[l].astype(BF16), row(ssm_glu_b[l]))
        cw, cb = conv_w[l], row(conv_b[l])
        prm = {"mu": row(tm_mu[l]), "w0": row(tm_w0[l]), "w2": _pad_lora(tm_w2[l], 0),
               "a0": row(tm_a0[l]), "a2": _pad_lora(tm_a2[l], 32), "g2": _pad_lora(tm_g2[l], 64),
               "k_k": row(tm_k_k[l]), "k_a": row(tm_k_a[l]), "r_k": row(tm_r_k[l]),
               "ln_g": row(tm_ln_g[l]), "ln_b": row(tm_ln_b[l])}
        final = l == depth - 1

        za, zb, zc, zd = _inproj(xp, g1, w_in_l, tm_p)
        ya, yc, tail = _gmlp_conv(za, zc, lng, lnb, wcat, bias, cw, cb, bp, seq, tile_gc)
        zb_tm = jnp.transpose(zb.reshape(bp, seq, GROUP_W), (1, 0, 2)).reshape(seq * bp, GROUP_W)
        yb_tm, ssm_fin = _s5(zb_tm, *s5p, bp, seq)
        yb = jnp.transpose(yb_tm.reshape(seq, bp, GROUP_W), (1, 0, 2)).reshape(bp * seq, GROUP_W)
        yd, wkv_fin = _rwkv(zd, prm, bp, seq)
        xp = _out_ffn(xp, (ya, yb, yc, yd), wo_l, g2, wgu_l, wd_l, gf, tm_p, final)
        outs["wkv_p"].append(wkv_fin)
        outs["sh_p"].append(zd.reshape(bp, seq, D_TM)[:, -1])
        outs["ssm_p"].append(ssm_fin)
        outs["cv_p"].append(tail[:, 6:8])

        za, zb, zc, zd = _inproj(xs, g1, w_in_l, bs)
        ya, yb, yc, yd, vn, s_new, ssm_new, z_new = _sample_mix(
            za, zb, zc, zd, state_shift[l], wkv_s_in[l], ssm_s_in[l],
            state_conv[l][:, 0], state_conv[l][:, 1], (lng, lnb, w00, b0), s5p, cw, cb, prm,
            32 if bs % 32 == 0 else bs)
        xs = _out_ffn(xs, (ya, yb, yc, yd), wo_l, g2, wgu_l, wd_l, gf, bs, final)
        outs["wkv_s"].append(s_new)
        outs["sh_s"].append(zd)
        outs["ssm_s"].append(ssm_new)
        outs["cv_s"].append(jnp.stack([state_conv[l][:, 1], z_new], axis=1))
        outs["chv"].append(vn.reshape(bs, 1, GROUP_W))

    def wkv_blocks(s_bd):
        s5d = s_bd.reshape(depth, -1, WKV_HEADS, WKV_N, WKV_HEADS, WKV_N)
        return jnp.stack([s5d[:, :, h, :, h, :] for h in range(WKV_HEADS)], axis=2)

    wkv_p = wkv_blocks(jnp.stack(outs["wkv_p"]))
    wkv_s = jnp.transpose(jnp.stack(outs["wkv_s"]).reshape(depth, bs, WKV_N, WKV_HEADS, WKV_N),
                          (0, 1, 3, 2, 4))
    ssm_p = jnp.stack(outs["ssm_p"])
    ssm_s = jnp.stack(outs["ssm_s"])
    split = lambda s, i: s[..., i * SSM_S:(i + 1) * SSM_S].reshape(depth, -1, SSM_GROUPS, SSM_P)
    return (xp.reshape(bp, seq, D_MODEL), xs.reshape(bs, 1, D_MODEL),
            wkv_p, wkv_s,
            jnp.stack(outs["sh_p"]), jnp.stack(outs["sh_s"]),
            split(ssm_p, 0), split(ssm_s, 0), split(ssm_p, 1), split(ssm_s, 1),
            jnp.stack(outs["cv_p"]), jnp.stack(outs["cv_s"]),
            jnp.stack(outs["chv"]))
```

```python
import functools
import math

import jax
import jax.numpy as jnp
from jax import lax
from jax.experimental import pallas as pl
from jax.experimental.pallas import tpu as pltpu

F32 = jnp.float32
BF16 = jnp.bfloat16

D_MODEL = 1024
GROUP_W = 256
GM_CHUNK = 128
GM_HEADS = 4
SSM_CH = 16
SSM_GROUPS = 16
SSM_P = 64
SSM_S = SSM_GROUPS * SSM_P
WKV_N = 64
WKV_HEADS = 4
LORA_PAD = 128
D_TM = 3 * GROUP_W + LORA_PAD
IN_COLS = 6 * GROUP_W + D_TM
D_FF = 2816
NORM_EPS = 1e-6
GM_LN_EPS = 1e-5
WKV_LN_EPS = 64e-5

WKV_CHUNK = 64
S5_TCHUNK = 128
VMEM_LIMIT = 56 * 1024 * 1024


def _cparams(sem):
    return pltpu.CompilerParams(dimension_semantics=sem, vmem_limit_bytes=VMEM_LIMIT)


def _full(shape):
    n = len(shape)
    return pl.BlockSpec(shape, lambda *_: (0,) * n)


def _dot(a, b):
    return jnp.dot(a.astype(BF16), b.astype(BF16), preferred_element_type=F32)


def _dot_nt(a, b):
    return lax.dot_general(a.astype(BF16), b.astype(BF16), (((1,), (1,)), ((), ())),
                           preferred_element_type=F32)


def _dot_tn(a, b):
    return lax.dot_general(a.astype(BF16), b.astype(BF16), (((0,), (0,)), ((), ())),
                           preferred_element_type=F32)


def _dot_split(x, ones_bf16):
    hi = x.astype(BF16)
    lo = (x - hi.astype(F32)).astype(BF16)
    return (jnp.dot(hi, ones_bf16, preferred_element_type=F32)
            + jnp.dot(lo, ones_bf16, preferred_element_type=F32))


def _rms(x, g):
    return x * lax.rsqrt(jnp.mean(x * x, axis=-1, keepdims=True) + NORM_EPS) * g


def _softplus(y):
    return jnp.maximum(y, 0.0) + jnp.log1p(jnp.exp(-jnp.abs(y)))


def _inproj_kernel(x_ref, g_ref, w_ref, za_ref, zb_ref, zc_ref, zd_ref):
    h = _rms(x_ref[...], g_ref[...])
    z = jnp.dot(h.astype(BF16), w_ref[...], preferred_element_type=F32)
    za_ref[...] = z[:, 0:2 * GROUP_W]
    zb_ref[...] = z[:, 2 * GROUP_W:3 * GROUP_W]
    zc_ref[...] = z[:, 3 * GROUP_W:6 * GROUP_W]
    zd_ref[...] = z[:, 6 * GROUP_W:]


def _inproj(x, g, w, tm):
    rows = x.shape[0]
    widths = (2 * GROUP_W, GROUP_W, 3 * GROUP_W, D_TM)
    return pl.pallas_call(
        _inproj_kernel,
        grid=(rows // tm,),
        in_specs=[pl.BlockSpec((tm, D_MODEL), lambda i: (i, 0)),
                  _full((1, D_MODEL)),
                  _full((D_MODEL, IN_COLS))],
        out_specs=[pl.BlockSpec((tm, wd), lambda i: (i, 0)) for wd in widths],
        out_shape=[jax.ShapeDtypeStruct((rows, wd), F32) for wd in widths],
        compiler_params=_cparams(("parallel",)),
    )(x, g, w)


def _gm_norm(zav, ln_g, ln_b):
    vf = jax.nn.gelu(zav)
    mu = jnp.mean(vf, axis=-1, keepdims=True)
    var = jnp.mean(jnp.square(vf - mu), axis=-1, keepdims=True)
    return (vf - mu) * lax.rsqrt(var + GM_LN_EPS) * ln_g + ln_b


def _gmlp_conv_kernel(za_ref, zc_ref, lng_ref, lnb_ref, wcat_ref, bias_ref, cw_ref, cb_ref,
                      ya_ref, yc_ref, tail_ref, prev_ref, *, tile):
    j = pl.program_id(1)

    @pl.when(j == 0)
    def _():
        prev_ref[...] = jnp.zeros_like(prev_ref)

    za = za_ref[...]
    u = jax.nn.gelu(za[:, :GROUP_W])
    vn = _gm_norm(za[:, GROUP_W:], lng_ref[...], lnb_ref[...])
    kc = GM_HEADS * GM_CHUNK
    t_i = lax.broadcasted_iota(jnp.int32, (GM_CHUNK, kc), 0)
    s_i = lax.broadcasted_iota(jnp.int32, (GM_CHUNK, kc), 1) % GM_CHUNK
    wm = jnp.where(s_i <= t_i, wcat_ref[...], 0.0).astype(BF16)
    r_h = lax.broadcasted_iota(jnp.int32, (kc, GROUP_W), 0) // GM_CHUNK
    c_h = lax.broadcasted_iota(jnp.int32, (kc, GROUP_W), 1) // (GROUP_W // GM_HEADS)
    head_mask = r_h == c_h
    for c in range(tile // GM_CHUNK):
        rows = slice(c * GM_CHUNK, (c + 1) * GM_CHUNK)
        vc = vn[rows].astype(BF16)
        rhs = jnp.where(head_mask, jnp.concatenate([vc] * GM_HEADS, axis=0), jnp.zeros((), BF16))
        s = jnp.dot(wm, rhs, preferred_element_type=F32) + bias_ref[...]
        ya_ref[rows, :] = u[rows] * s

    zc = zc_ref[...]
    z = zc[:, 2 * GROUP_W:] * zc[:, :GROUP_W]
    row = lax.broadcasted_iota(jnp.int32, z.shape, 0)
    prev = prev_ref[...]
    z1 = jnp.where(row == 0, prev[7:8], pltpu.roll(z, 1, 0))
    z2 = jnp.where(row == 0, prev[6:7], jnp.where(row == 1, prev[7:8], pltpu.roll(z, 2, 0)))
    cw = cw_ref[...]
    y = cb_ref[...] + cw[0:1] * z2 + cw[1:2] * z1 + cw[2:3] * z
    yc_ref[...] = zc[:, GROUP_W:2 * GROUP_W] * y
    prev_ref[...] = z[tile - 8:]
    tail_ref[...] = z[tile - 8:]


def _gmlp_conv(za, zc, lng, lnb, wcat, bias, cw, cb, batch, seq, tile):
    nt = seq // tile
    rows = batch * seq
    row_blk = lambda wd: pl.BlockSpec((tile, wd), lambda b, j: (b * nt + j, 0))
    return pl.pallas_call(
        functools.partial(_gmlp_conv_kernel, tile=tile),
        grid=(batch, nt),
        in_specs=[row_blk(2 * GROUP_W), row_blk(3 * GROUP_W),
                  _full((1, GROUP_W)), _full((1, GROUP_W)),
                  _full((GM_CHUNK, GM_HEADS * GM_CHUNK)), _full((GM_CHUNK, GROUP_W)),
                  _full((3, GROUP_W)), _full((1, GROUP_W))],
        out_specs=[row_blk(GROUP_W), row_blk(GROUP_W),
                   pl.BlockSpec((None, 8, GROUP_W), lambda b, j: (b, 0, 0))],
        out_shape=[jax.ShapeDtypeStruct((rows, GROUP_W), F32),
                   jax.ShapeDtypeStruct((rows, GROUP_W), F32),
                   jax.ShapeDtypeStruct((batch, 8, GROUP_W), F32)],
        scratch_shapes=[pltpu.VMEM((8, GROUP_W), F32)],
        compiler_params=_cparams(("parallel", "arbitrary")),
    )(za, zc, lng, lnb, wcat, bias, cw, cb)


def _s5_prep_kernel(are_ref, aim_ref, ldt_ref, bre_ref, bim_ref, cre_ref, cim_ref,
                    lam_ref, bb_ref, cc_ref):
    lam_re = jnp.minimum(are_ref[...], -1e-4)
    lam_im = aim_ref[...]
    dt = jnp.exp(ldt_ref[...])
    mag = jnp.exp(lam_re * dt)
    lb_re = mag * jnp.cos(lam_im * dt)
    lb_im = mag * jnp.sin(lam_im * dt)
    den = lam_re * lam_re + lam_im * lam_im
    f_re = ((lb_re - 1.0) * lam_re + lb_im * lam_im) / den
    f_im = (lb_im * lam_re - (lb_re - 1.0) * lam_im) / den
    lam_ref[0:1, :] = lb_re
    lam_ref[1:2, :] = lb_im
    br, bi = bre_ref[...], bim_ref[...]
    grp_r = lax.broadcasted_iota(jnp.int32, (GROUP_W, SSM_S), 0) // SSM_CH
    grp_c = lax.broadcasted_iota(jnp.int32, (GROUP_W, SSM_S), 1) // SSM_P
    m = grp_r == grp_c
    bb_ref[:, :SSM_S] = jnp.where(m, f_re * br - f_im * bi, 0.0).astype(BF16)
    bb_ref[:, SSM_S:] = jnp.where(m, f_re * bi + f_im * br, 0.0).astype(BF16)
    grp_r2 = lax.broadcasted_iota(jnp.int32, (SSM_S, GROUP_W), 0) // SSM_P
    grp_c2 = lax.broadcasted_iota(jnp.int32, (SSM_S, GROUP_W), 1) // SSM_CH
    m2 = grp_r2 == grp_c2
    cc_ref[:SSM_S, :] = jnp.where(m2, cre_ref[...], 0.0).astype(BF16)
    cc_ref[SSM_S:, :] = jnp.where(m2, -cim_ref[...], 0.0).astype(BF16)


def _s5_prep(a_re, a_im, log_dt, b_re, b_im, c_re, c_im):
    flat = lambda p: p.reshape(1, SSM_S)
    b_exp = lambda b: jnp.tile(jnp.transpose(b, (2, 0, 1)).reshape(SSM_CH, SSM_S), (SSM_GROUPS, 1))
    c_exp = lambda c: jnp.tile(jnp.transpose(c, (0, 2, 1)).reshape(SSM_S, SSM_CH), (1, SSM_GROUPS))
    return pl.pallas_call(
        _s5_prep_kernel,
        out_shape=[jax.ShapeDtypeStruct((2, SSM_S), F32),
                   jax.ShapeDtypeStruct((GROUP_W, 2 * SSM_S), BF16),
                   jax.ShapeDtypeStruct((2 * SSM_S, GROUP_W), BF16)],
        compiler_params=pltpu.CompilerParams(vmem_limit_bytes=VMEM_LIMIT),
    )(flat(a_re), flat(a_im), flat(log_dt), b_exp(b_re), b_exp(b_im), c_exp(c_re), c_exp(c_im))


def _s5_output(st, u, cc, d, glu_w, glu_b):
    y = jnp.dot(st.astype(BF16), cc, preferred_element_type=F32) + d * u
    y = jax.nn.gelu(y)
    return y * jax.nn.sigmoid(jnp.dot(y.astype(BF16), glu_w, preferred_element_type=F32) + glu_b)


def _s5_kernel(u_ref, lam_ref, bb_ref, cc_ref, d_ref, gw_ref, gb_ref, y_ref, fin_ref,
               bu_ref, st_ref, *, batch, tsteps):
    @pl.when(pl.program_id(0) == 0)
    def _():
        st_ref[...] = jnp.zeros_like(st_ref)

    u = u_ref[...]
    bu_ref[...] = jnp.dot(u.astype(BF16), bb_ref[...], preferred_element_type=F32)
    lam = lam_ref[...]
    lr = jnp.broadcast_to(lam[0:1], (batch, SSM_S))
    li = jnp.broadcast_to(lam[1:2], (batch, SSM_S))

    def step(t, carry):
        s_re, s_im = carry
        rows = pl.ds(pl.multiple_of(t * batch, batch), batch)
        n_re = lr * s_re - li * s_im + bu_ref[rows, :SSM_S]
        n_im = lr * s_im + li * s_re + bu_ref[rows, SSM_S:]
        bu_ref[rows, :SSM_S] = n_re
        bu_ref[rows, SSM_S:] = n_im
        return n_re, n_im

    st0 = st_ref[...]
    s_re, s_im = lax.fori_loop(0, tsteps, step, (st0[:, :SSM_S], st0[:, SSM_S:]), unroll=4)
    st_ref[:, :SSM_S] = s_re
    st_ref[:, SSM_S:] = s_im
    fin_ref[:, :SSM_S] = s_re
    fin_ref[:, SSM_S:] = s_im
    y_ref[...] = _s5_output(bu_ref[...], u, cc_ref[...], d_ref[...], gw_ref[...], gb_ref[...])


def _s5(u_tm, lam, bb, cc, d, glu_w, glu_b, batch, seq):
    tsteps = min(S5_TCHUNK, seq)
    rows = tsteps * batch
    return pl.pallas_call(
        functools.partial(_s5_kernel, batch=batch, tsteps=tsteps),
        grid=(seq // tsteps,),
        in_specs=[pl.BlockSpec((rows, GROUP_W), lambda i: (i, 0)),
                  _full((2, SSM_S)), _full((GROUP_W, 2 * SSM_S)), _full((2 * SSM_S, GROUP_W)),
                  _full((1, GROUP_W)), _full((GROUP_W, GROUP_W)), _full((1, GROUP_W))],
        out_specs=[pl.BlockSpec((rows, GROUP_W), lambda i: (i, 0)),
                   _full((batch, 2 * SSM_S))],
        out_shape=[jax.ShapeDtypeStruct((seq * batch, GROUP_W), F32),
                   jax.ShapeDtypeStruct((batch, 2 * SSM_S), F32)],
        scratch_shapes=[pltpu.VMEM((rows, 2 * SSM_S), F32),
                        pltpu.VMEM((batch, 2 * SSM_S), F32)],
        compiler_params=_cparams(("arbitrary",)),
    )(u_tm, lam, bb, cc, d, glu_w, glu_b)


def _wkv_inputs(zd, zprev, p, bd_ones):
    zs = zd + p["mu"] * (zprev - zd)
    r = zs[:, 0:GROUP_W]
    k = zs[:, GROUP_W:2 * GROUP_W]
    v = zs[:, 2 * GROUP_W:3 * GROUP_W]
    lora = zs[:, 3 * GROUP_W:]
    w = -_softplus(-(p["w0"] + _dot(jnp.tanh(lora), p["w2"]))) - 0.5
    logd = -jnp.exp(w)
    a = jax.nn.sigmoid(p["a0"] + _dot(lora, p["a2"]))
    g = _dot(jax.nn.sigmoid(lora), p["g2"])
    kk = k * p["k_k"]
    nrm = jnp.sqrt(_dot_split(kk * kk, bd_ones))
    kk = kk / jnp.maximum(nrm, 1e-12)
    k2 = k * (1.0 + (a - 1.0) * p["k_a"])
    return r, logd, k2, v, kk, a, g


def _wkv_output(o, r, k2, v, g, p, bd_ones):
    inv_n = 1.0 / WKV_N
    m = _dot_split(o, bd_ones) * inv_n
    var = _dot_split(jnp.square(o - m), bd_ones) * inv_n
    on = (o - m) * lax.rsqrt(var + WKV_LN_EPS) * p["ln_g"] + p["ln_b"]
    bonus = _dot_split(r * k2 * p["r_k"], bd_ones) * v
    return (on + bonus) * g


_WKV_PARAMS = ("mu", "w0", "w2", "a0", "a2", "g2", "k_k", "k_a", "r_k", "ln_g", "ln_b")
_WKV_PARAM_SHAPES = {"mu": (1, D_TM), "w2": (LORA_PAD, GROUP_W), "a2": (LORA_PAD, GROUP_W),
                     "g2": (LORA_PAD, GROUP_W)}


def _wkv_param_specs():
    return [_full(_WKV_PARAM_SHAPES.get(n, (1, GROUP_W))) for n in _WKV_PARAMS]


def _bd_mask(n):
    hr = lax.broadcasted_iota(jnp.int32, (n, n), 0) // (n // WKV_HEADS)
    hc = lax.broadcasted_iota(jnp.int32, (n, n), 1) // (n // WKV_HEADS)
    return hr == hc


def _expand(xp, mask):
    return jnp.where(mask, jnp.concatenate([xp] * WKV_HEADS, axis=0), 0.0)


def _wkv_chunks(r, logd, k2, v, kk, a, states, tri_ones, bd256):
    c = WKV_CHUNK
    n = len(states)
    rows = [slice(i * c, (i + 1) * c) for i in range(n)]
    t_i = lax.broadcasted_iota(jnp.int32, (c, GROUP_W), 0)
    s_i = lax.broadcasted_iota(jnp.int32, (c, GROUP_W), 1) % c
    strict = s_i < t_i
    incl = s_i <= t_i
    ex = lambda xp: _expand(xp, bd256)

    cum = [jnp.dot(tri_ones, logd[rw], preferred_element_type=F32, precision=lax.Precision.HIGHEST)
           for rw in rows]
    g_last = [cm[c - 1:c] for cm in cum]
    a_t, r_t, b_t, k_t, b_h, k_h, v_c = [], [], [], [], [], [], []
    for i, rw in enumerate(rows):
        e_neg = jnp.exp(-cum[i])
        e_end = jnp.exp(g_last[i] - cum[i])
        bvec = kk[rw] * a[rw]
        a_t.append(-kk[rw] * jnp.exp(cum[i] - logd[rw]))
        r_t.append(r[rw] * jnp.exp(cum[i]))
        b_t.append(bvec * e_neg)
        k_t.append(k2[rw] * e_neg)
        b_h.append(bvec * e_end)
        k_h.append(k2[rw] * e_end)
        v_c.append(v[rw])

    ar = [jnp.concatenate([a_t[i], r_t[i]], axis=0) for i in range(n)]
    p_b = [_dot_nt(ar[i], ex(b_t[i])) for i in range(n)]
    p_k = [_dot_nt(ar[i], ex(k_t[i])) for i in range(n)]
    l_p = [jnp.where(strict, p[:c], 0.0) for p in p_b]
    aak = [jnp.where(strict, p[:c], 0.0) for p in p_k]
    rb = [jnp.where(incl, p[c:], 0.0) for p in p_b]
    rk = [jnp.where(incl, p[c:], 0.0) for p in p_k]

    t_p = [jnp.where(s_i == t_i, 1.0, 0.0)] * n
    m = 1
    while m < c:
        off = (t_i // (2 * m) == s_i // (2 * m)) & (t_i % (2 * m) >= m) & (s_i % (2 * m) < m)
        x = [_dot(t_p[i], ex(jnp.where(off, l_p[i], 0.0))) for i in range(n)]
        t_p = [t_p[i] + _dot(x[i], ex(t_p[i])) for i in range(n)]
        m *= 2

    v_bd = [ex(v_c[i]) for i in range(n)]
    a2 = [_dot(t_p[i], ex(a_t[i])) for i in range(n)]
    av = [_dot(aak[i], v_bd[i]) for i in range(n)]
    w0 = [_dot(t_p[i], ex(av[i])) for i in range(n)]
    w = [_dot_nt(a2[i], states[i]) + w0[i] for i in range(n)]
    o = [_dot_nt(r_t[i], states[i]) + _dot(rb[i], ex(w[i])) + _dot(rk[i], v_bd[i]) for i in range(n)]
    upd = [_dot_tn(jnp.concatenate([w[i], v_c[i]], axis=0), jnp.concatenate([b_h[i], k_h[i]], axis=0))
           for i in range(n)]
    s_new = [states[i] * jnp.exp(g_last[i]) + jnp.where(bd256, upd[i], 0.0) for i in range(n)]
    return jnp.concatenate(o, axis=0), s_new


def _rwkv_kernel(zd_ref, *rest, nb):
    prm = {n: ref[...] for n, ref in zip(_WKV_PARAMS, rest)}
    y_ref, sfin_ref, prev_ref, s_ref = rest[len(_WKV_PARAMS):]
    c = WKV_CHUNK

    @pl.when(pl.program_id(0) == 0)
    def _():
        prev_ref[...] = jnp.zeros_like(prev_ref)
        s_ref[...] = jnp.zeros_like(s_ref)

    bd256 = _bd_mask(GROUP_W)
    bd_ones = jnp.where(bd256, 1.0, 0.0).astype(BF16)
    tri_ones = jnp.where(lax.broadcasted_iota(jnp.int32, (c, c), 1)
                         <= lax.broadcasted_iota(jnp.int32, (c, c), 0), 1.0, 0.0)
    zd3 = zd_ref[...]
    zd = zd3.reshape(nb * c, D_TM)
    first = lax.broadcasted_iota(jnp.int32, (nb, c, D_TM), 1) == 0
    carried = jnp.broadcast_to(prev_ref[:, 7:8, :], (nb, c, D_TM))
    zprev = jnp.where(first, carried, pltpu.roll(zd, 1, 0).reshape(nb, c, D_TM)).reshape(nb * c, D_TM)
    r, logd, k2, v, kk, a, g = _wkv_inputs(zd, zprev, prm, bd_ones)
    o, s_new = _wkv_chunks(r, logd, k2, v, kk, a, [s_ref[b] for b in range(nb)], tri_ones, bd256)
    y_ref[...] = _wkv_output(o, r, k2, v, g, prm, bd_ones).reshape(nb, c, GROUP_W)
    for b in range(nb):
        s_ref[b] = s_new[b]
        sfin_ref[b] = s_new[b]
    prev_ref[...] = zd3[:, c - 8:, :]


def _rwkv(zd, prm, batch, seq):
    c = WKV_CHUNK
    return pl.pallas_call(
        functools.partial(_rwkv_kernel, nb=batch),
        grid=(seq // c,),
        in_specs=[pl.BlockSpec((batch, c, D_TM), lambda j: (0, j, 0))] + _wkv_param_specs(),
        out_specs=[pl.BlockSpec((batch, c, GROUP_W), lambda j: (0, j, 0)),
                   _full((batch, GROUP_W, GROUP_W))],
        out_shape=[jax.ShapeDtypeStruct((batch, seq, GROUP_W), F32),
                   jax.ShapeDtypeStruct((batch, GROUP_W, GROUP_W), F32)],
        scratch_shapes=[pltpu.VMEM((batch, 8, D_TM), F32), pltpu.VMEM((batch, GROUP_W, GROUP_W), F32)],
        compiler_params=_cparams(("arbitrary",)),
    )(zd.reshape(batch, seq, D_TM), *[prm[n] for n in _WKV_PARAMS])


def _sample_mix_kernel(za_ref, zb_ref, zc_ref, zd_ref, shift_ref, s_ref, ssm_ref, cv0_ref, cv1_ref,
                       lng_ref, lnb_ref, w00_ref, b0_ref, lam_ref, bb_ref, cc_ref, d_ref, gw_ref,
                       gb_ref, cw_ref, cb_ref, *rest):
    prm = {n: ref[...] for n, ref in zip(_WKV_PARAMS, rest)}
    ya_ref, yb_ref, yc_ref, yd_ref, vn_ref, snew_ref, ssmnew_ref, znew_ref = rest[len(_WKV_PARAMS):]

    za = za_ref[...]
    vn = _gm_norm(za[:, GROUP_W:], lng_ref[...], lnb_ref[...])
    vn_ref[...] = vn
    ya_ref[...] = jax.nn.gelu(za[:, :GROUP_W]) * (w00_ref[...] * vn + b0_ref[...])

    u = zb_ref[...]
    bu = jnp.dot(u.astype(BF16), bb_ref[...], preferred_element_type=F32)
    lam = lam_ref[...]
    lr, li = lam[0:1], lam[1:2]
    st = ssm_ref[...]
    s_re, s_im = st[:, :SSM_S], st[:, SSM_S:]
    st_new = jnp.concatenate([lr * s_re - li * s_im + bu[:, :SSM_S],
                              lr * s_im + li * s_re + bu[:, SSM_S:]], axis=1)
    ssmnew_ref[...] = st_new
    yb_ref[...] = _s5_output(st_new, u, cc_ref[...], d_ref[...], gw_ref[...], gb_ref[...])

    zc = zc_ref[...]
    z = zc[:, 2 * GROUP_W:] * zc[:, :GROUP_W]
    cw = cw_ref[...]
    y = cb_ref[...] + cw[0:1] * cv0_ref[...] + cw[1:2] * cv1_ref[...] + cw[2:3] * z
    yc_ref[...] = zc[:, GROUP_W:2 * GROUP_W] * y
    znew_ref[...] = z

    bd256 = _bd_mask(GROUP_W)
    bd_ones = jnp.where(bd256, 1.0, 0.0).astype(BF16)
    zd = zd_ref[...]
    r, logd, k2, v, kk, a, g = _wkv_inputs(zd, shift_ref[...], prm, bd_ones)
    bt = zd.shape[0]
    s = s_ref[...]
    eye4 = (lax.broadcasted_iota(jnp.int32, (WKV_N, GROUP_W), 0)
            == lax.broadcasted_iota(jnp.int32, (WKV_N, GROUP_W), 1) % WKV_N)

    def head_sum(x3):
        return _dot_split(x3.reshape(bt * WKV_N, GROUP_W), bd_ones).reshape(bt, WKV_N, GROUP_W)

    sa = head_sum(s * (-kk)[:, None, :])
    vcol = head_sum(jnp.where(eye4[None], v[:, None, :], 0.0))
    s_new = (s * jnp.exp(logd)[:, None, :] + sa * (kk * a)[:, None, :] + vcol * k2[:, None, :])
    snew_ref[...] = s_new
    o_rep = head_sum(s_new * r[:, None, :])
    o = jnp.sum(jnp.where(eye4[None], o_rep, 0.0), axis=1)
    yd_ref[...] = _wkv_output(o, r, k2, v, g, prm, bd_ones)


def _sample_mix(za, zb, zc, zd, shift, s_t, ssm, cv0, cv1, gm, s5p, cw, cb, prm, bt):
    rows = za.shape[0]
    row_blk = lambda wd: pl.BlockSpec((bt, wd), lambda i: (i, 0))
    s_blk = pl.BlockSpec((bt, WKV_N, GROUP_W), lambda i: (i, 0, 0))
    vec = _full((1, GROUP_W))
    lam, bb, cc, d, glu_w, glu_b = s5p
    lng, lnb, w00, b0 = gm
    return pl.pallas_call(
        _sample_mix_kernel,
        grid=(rows // bt,),
        in_specs=[row_blk(2 * GROUP_W), row_blk(GROUP_W), row_blk(3 * GROUP_W), row_blk(D_TM),
                  row_blk(D_TM), s_blk, row_blk(2 * SSM_S), row_blk(GROUP_W), row_blk(GROUP_W),
                  vec, vec, vec, vec,
                  _full((2, SSM_S)), _full((GROUP_W, 2 * SSM_S)), _full((2 * SSM_S, GROUP_W)),
                  vec, _full((GROUP_W, GROUP_W)), vec, _full((3, GROUP_W)), vec] + _wkv_param_specs(),
        out_specs=[row_blk(GROUP_W)] * 5 + [s_blk, row_blk(2 * SSM_S), row_blk(GROUP_W)],
        out_shape=[jax.ShapeDtypeStruct((rows, GROUP_W), F32)] * 5
        + [jax.ShapeDtypeStruct((rows, WKV_N, GROUP_W), F32),
           jax.ShapeDtypeStruct((rows, 2 * SSM_S), F32),
           jax.ShapeDtypeStruct((rows, GROUP_W), F32)],
        compiler_params=_cparams(("parallel",)),
    )(za, zb, zc, zd, shift, s_t, ssm, cv0, cv1, lng, lnb, w00, b0, lam, bb, cc, d, glu_w, glu_b,
      cw, cb, *[prm[n] for n in _WKV_PARAMS])


FF_SPLIT = 2


def _out_ffn_kernel(x_ref, ya_ref, yb_ref, yc_ref, yd_ref, wo_ref, g2_ref, wg_ref, wu_ref, wd_ref,
                    gf_ref, o_ref, h_ref, *, final):
    c = pl.program_id(1)

    @pl.when(c == 0)
    def _():
        x = x_ref[...]
        for i, y_ref in enumerate((ya_ref, yb_ref, yc_ref, yd_ref)):
            x = x + jnp.dot(y_ref[...].astype(BF16), wo_ref[i * GROUP_W:(i + 1) * GROUP_W, :],
                            preferred_element_type=F32)
        o_ref[...] = x
        h_ref[...] = _rms(x, g2_ref[...]).astype(BF16)

    h = h_ref[...]
    gate = jnp.dot(h, wg_ref[...], preferred_element_type=F32)
    up = jnp.dot(h, wu_ref[...], preferred_element_type=F32)
    act = (gate * jax.nn.sigmoid(gate) * up).astype(BF16)
    o_ref[...] += jnp.dot(act, wd_ref[...], preferred_element_type=F32)

    if final:
        @pl.when(c == FF_SPLIT - 1)
        def _():
            o_ref[...] = _rms(o_ref[...], gf_ref[...])


def _out_ffn(x, ys, wo, g2, wgu, wd, gf, tm, final):
    rows = x.shape[0]
    fc = D_FF // FF_SPLIT
    row_blk = lambda wd_: pl.BlockSpec((tm, wd_), lambda i, c: (i, 0))
    const = lambda shape: pl.BlockSpec(shape, lambda i, c: (0, 0))
    return pl.pallas_call(
        functools.partial(_out_ffn_kernel, final=final),
        grid=(rows // tm, FF_SPLIT),
        in_specs=[row_blk(D_MODEL)] + [row_blk(GROUP_W)] * 4
        + [const((D_MODEL, D_MODEL)), const((1, D_MODEL)),
           pl.BlockSpec((D_MODEL, fc), lambda i, c: (0, c)),
           pl.BlockSpec((D_MODEL, fc), lambda i, c: (0, FF_SPLIT + c)),
           pl.BlockSpec((fc, D_MODEL), lambda i, c: (c, 0)),
           const((1, D_MODEL))],
        out_specs=row_blk(D_MODEL),
        out_shape=jax.ShapeDtypeStruct((rows, D_MODEL), F32),
        scratch_shapes=[pltpu.VMEM((tm, D_MODEL), BF16)],
        compiler_params=_cparams(("parallel", "arbitrary")),
    )(x, *ys, wo, g2, wgu, wgu, wd, gf)


def _pad_lora(w, start):
    return jnp.zeros((LORA_PAD, GROUP_W), BF16).at[start:start + w.shape[0]].set(w.astype(BF16))


def kernel(x_prompt, x_sample, state_wkv, state_shift, state_ssm_re, state_ssm_im, state_conv,
           norm1_g, w_in, gm_ln_g, gm_ln_b, gm_ws, gm_bs,
           ssm_a_re, ssm_a_im, ssm_log_dt, ssm_b_re, ssm_b_im, ssm_c_re, ssm_c_im, ssm_d,
           ssm_glu_w, ssm_glu_b, conv_w, conv_b,
           tm_mu, tm_w0, tm_w2, tm_a0, tm_a2, tm_g2, tm_k_k, tm_k_a, tm_r_k, tm_ln_g, tm_ln_b,
           w_out, norm2_g, ffn_w_gu, ffn_w_down, norm_f_g):
    depth = w_in.shape[0]
    bp, seq, _ = x_prompt.shape
    bs = x_sample.shape[0]
    assert x_sample.shape[1] == 1 and seq % GM_CHUNK == 0 and seq % WKV_CHUNK == 0
    head_d = GROUP_W // GM_HEADS
    row = lambda p: p.reshape(1, -1)

    xp = x_prompt.reshape(bp * seq, D_MODEL)
    xs = x_sample.reshape(bs, D_MODEL)
    gf = row(norm_f_g)
    wkv_s_in = jnp.transpose(state_wkv, (0, 1, 3, 2, 4)).reshape(depth, bs, WKV_N, GROUP_W)
    ssm_s_in = jnp.concatenate([state_ssm_re.reshape(depth, bs, SSM_S),
                                state_ssm_im.reshape(depth, bs, SSM_S)], axis=-1)

    outs = {k: [] for k in ("wkv_p", "wkv_s", "sh_p", "sh_s", "ssm_p", "ssm_s", "cv_p", "cv_s", "chv")}
    tm_p = 512 if (bp * seq) % 512 == 0 else GM_CHUNK
    tile_gc = 512 if seq % 512 == 0 else GM_CHUNK
    for l in range(depth):
        w_in_l = w_in[l].astype(BF16)
        wo_l, wgu_l, wd_l = w_out[l].astype(BF16), ffn_w_gu[l].astype(BF16), ffn_w_down[l].astype(BF16)
        g1, g2 = row(norm1_g[l]), row(norm2_g[l])
        lng, lnb = row(gm_ln_g[l]), row(gm_ln_b[l])
        wcat = jnp.transpose(gm_ws[l], (1, 0, 2)).reshape(GM_CHUNK, GM_HEADS * GM_CHUNK)
        bias = jnp.repeat(jnp.transpose(gm_bs[l]), head_d, axis=1)
        w00 = row(jnp.repeat(gm_ws[l][:, 0, 0], head_d))
        b0 = row(jnp.repeat(gm_bs[l][:, 0], head_d))
        lam, bb, cc = _s5_prep(ssm_a_re[l], ssm_a_im[l], ssm_log_dt[l], ssm_b_re[l], ssm_b_im[l],
                               ssm_c_re[l], ssm_c_im[l])
        s5p = (lam, bb, cc, row(ssm_d[l]), ssm_glu_w[l].astype(BF16), row(ssm_glu_b[l]))
        cw, cb = conv_w[l], row(conv_b[l])
        prm = {"mu": row(tm_mu[l]), "w0": row(tm_w0[l]), "w2": _pad_lora(tm_w2[l], 0),
               "a0": row(tm_a0[l]), "a2": _pad_lora(tm_a2[l], 32), "g2": _pad_lora(tm_g2[l], 64),
               "k_k": row(tm_k_k[l]), "k_a": row(tm_k_a[l]), "r_k": row(tm_r_k[l]),
               "ln_g": row(tm_ln_g[l]), "ln_b": row(tm_ln_b[l])}
        final = l == depth - 1

        za, zb, zc, zd = _inproj(xp, g1, w_in_l, tm_p)
        ya, yc, tail = _gmlp_conv(za, zc, lng, lnb, wcat, bias, cw, cb, bp, seq, tile_gc)
        zb_tm = jnp.transpose(zb.reshape(bp, seq, GROUP_W), (1, 0, 2)).reshape(seq * bp, GROUP_W)
        yb_tm, ssm_fin = _s5(zb_tm, *s5p, bp, seq)
        yb = jnp.transpose(yb_tm.reshape(seq, bp, GROUP_W), (1, 0, 2)).reshape(bp * seq, GROUP_W)
        yd, wkv_fin = _rwkv(zd, prm, bp, seq)
        yd = yd.reshape(bp * seq, GROUP_W)
        xp = _out_ffn(xp, (ya, yb, yc, yd), wo_l, g2, wgu_l, wd_l, gf, tm_p, final)
        outs["wkv_p"].append(wkv_fin)
        outs["sh_p"].append(zd.reshape(bp, seq, D_TM)[:, -1])
        outs["ssm_p"].append(ssm_fin)
        outs["cv_p"].append(tail[:, 6:8])

        za, zb, zc, zd = _inproj(xs, g1, w_in_l, bs)
        ya, yb, yc, yd, vn, s_new, ssm_new, z_new = _sample_mix(
            za, zb, zc, zd, state_shift[l], wkv_s_in[l], ssm_s_in[l],
            state_conv[l][:, 0], state_conv[l][:, 1], (lng, lnb, w00, b0), s5p, cw, cb, prm,
            32 if bs % 32 == 0 else bs)
        xs = _out_ffn(xs, (ya, yb, yc, yd), wo_l, g2, wgu_l, wd_l, gf, bs, final)
        outs["wkv_s"].append(s_new)
        outs["sh_s"].append(zd)
        outs["ssm_s"].append(ssm_new)
        outs["cv_s"].append(jnp.stack([state_conv[l][:, 1], z_new], axis=1))
        outs["chv"].append(vn.reshape(bs, 1, GROUP_W))

    def wkv_blocks(s_bd):
        s5d = s_bd.reshape(depth, -1, WKV_HEADS, WKV_N, WKV_HEADS, WKV_N)
        return jnp.stack([s5d[:, :, h, :, h, :] for h in range(WKV_HEADS)], axis=2)

    wkv_p = wkv_blocks(jnp.stack(outs["wkv_p"]))
    wkv_s = jnp.transpose(jnp.stack(outs["wkv_s"]).reshape(depth, bs, WKV_N, WKV_HEADS, WKV_N),
                          (0, 1, 3, 2, 4))
    ssm_p = jnp.stack(outs["ssm_p"])
    ssm_s = jnp.stack(outs["ssm_s"])
    split = lambda s, i: s[..., i * SSM_S:(i + 1) * SSM_S].reshape(depth, -1, SSM_GROUPS, SSM_P)
    return (xp.reshape(bp, seq, D_MODEL), xs.reshape(bs, 1, D_MODEL),
            wkv_p, wkv_s,
            jnp.stack(outs["sh_p"]), jnp.stack(outs["sh_s"]),
            split(ssm_p, 0), split(ssm_s, 0), split(ssm_p, 1), split(ssm_s, 1),
            jnp.stack(outs["cv_p"]), jnp.stack(outs["cv_s"]),
            jnp.stack(outs["chv"]))
```

```python
import functools

import jax
import jax.numpy as jnp
from jax import lax
from jax.experimental import pallas as pl
from jax.experimental.pallas import tpu as pltpu

F32 = jnp.float32
BF16 = jnp.bfloat16

D_MODEL = 1024
GROUP_W = 256
GM_CHUNK = 128
GM_HEADS = 4
SSM_CH = 16
SSM_GROUPS = 16
SSM_P = 64
SSM_S = SSM_GROUPS * SSM_P
WKV_N = 64
WKV_HEADS = 4
LORA_PAD = 128
D_TM = 3 * GROUP_W + LORA_PAD
IN_COLS = 6 * GROUP_W + D_TM
D_FF = 2816
FF_SPLIT = 2
NORM_EPS = 1e-6
GM_LN_EPS = 1e-5
WKV_LN_EPS = 64e-5

WKV_CHUNK = 64
S5_TCHUNK = 128
ROW_TILE = 512
SAMPLE_TILE = 32
VMEM_LIMIT = 56 * 1024 * 1024


def _cparams(sem):
    return pltpu.CompilerParams(dimension_semantics=sem, vmem_limit_bytes=VMEM_LIMIT)


def _full(shape):
    n = len(shape)
    return pl.BlockSpec(shape, lambda *_: (0,) * n)


def _layer(shape, l):
    n = len(shape)
    return pl.BlockSpec((None,) + tuple(shape), lambda *_: (l,) + (0,) * n)


def _dot(a, b):
    return jnp.dot(a.astype(BF16), b.astype(BF16), preferred_element_type=F32)


def _dot_nt(a, b):
    return lax.dot_general(a.astype(BF16), b.astype(BF16), (((1,), (1,)), ((), ())),
                           preferred_element_type=F32)


def _dot_tn(a, b):
    return lax.dot_general(a.astype(BF16), b.astype(BF16), (((0,), (0,)), ((), ())),
                           preferred_element_type=F32)


def _dot_split(x, ones_bf16):
    hi = x.astype(BF16)
    lo = (x - hi.astype(F32)).astype(BF16)
    return (jnp.dot(hi, ones_bf16, preferred_element_type=F32)
            + jnp.dot(lo, ones_bf16, preferred_element_type=F32))


def _rms(x, g):
    return x * lax.rsqrt(jnp.mean(x * x, axis=-1, keepdims=True) + NORM_EPS) * g


def _softplus(y):
    return jnp.maximum(y, 0.0) + jnp.log1p(jnp.exp(-jnp.abs(y)))


def _gm_norm(zav, ln_g, ln_b):
    vf = jax.nn.gelu(zav)
    mu = jnp.mean(vf, axis=-1, keepdims=True)
    var = jnp.mean(jnp.square(vf - mu), axis=-1, keepdims=True)
    return (vf - mu) * lax.rsqrt(var + GM_LN_EPS) * ln_g + ln_b


def _inproj_kernel(x_ref, g_ref, w_ref, za_ref, zb_ref, zc_ref, zd_ref):
    h = _rms(x_ref[...], g_ref[...])
    z = jnp.dot(h.astype(BF16), w_ref[...], preferred_element_type=F32)
    za_ref[...] = z[:, 0:2 * GROUP_W]
    zb_ref[...] = z[:, 2 * GROUP_W:3 * GROUP_W]
    zc_ref[...] = z[:, 3 * GROUP_W:6 * GROUP_W]
    zd_ref[...] = z[:, 6 * GROUP_W:]


def _inproj(x, g, w, l):
    rows = x.shape[0]
    widths = (2 * GROUP_W, GROUP_W, 3 * GROUP_W, D_TM)
    return pl.pallas_call(
        _inproj_kernel,
        grid=(1,),
        in_specs=[_full((rows, D_MODEL)), _layer((1, D_MODEL), l), _layer((D_MODEL, IN_COLS), l)],
        out_specs=[_full((rows, wd)) for wd in widths],
        out_shape=[jax.ShapeDtypeStruct((rows, wd), F32) for wd in widths],
        compiler_params=_cparams(("arbitrary",)),
    )(x, g, w)


def _inproj_mix_kernel(x_ref, g_ref, w_ref, lng_ref, lnb_ref, wcat_ref, bias_ref, cw_ref, cb_ref,
                       ya_ref, zb_ref, yc_ref, zd_ref, tail_ref, prev_ref, *, tile):
    @pl.when(pl.program_id(1) == 0)
    def _():
        prev_ref[...] = jnp.zeros_like(prev_ref)

    h = _rms(x_ref[...], g_ref[...])
    z = jnp.dot(h.astype(BF16), w_ref[...], preferred_element_type=F32)
    zb_ref[...] = z[:, 2 * GROUP_W:3 * GROUP_W]
    zd_ref[...] = z[:, 6 * GROUP_W:]

    u = jax.nn.gelu(z[:, :GROUP_W])
    vn = _gm_norm(z[:, GROUP_W:2 * GROUP_W], lng_ref[...], lnb_ref[...])
    kc = GM_HEADS * GM_CHUNK
    t_i = lax.broadcasted_iota(jnp.int32, (GM_CHUNK, kc), 0)
    s_i = lax.broadcasted_iota(jnp.int32, (GM_CHUNK, kc), 1) % GM_CHUNK
    wm = jnp.where(s_i <= t_i, wcat_ref[...], 0.0).astype(BF16)
    r_h = lax.broadcasted_iota(jnp.int32, (kc, GROUP_W), 0) // GM_CHUNK
    c_h = lax.broadcasted_iota(jnp.int32, (kc, GROUP_W), 1) // (GROUP_W // GM_HEADS)
    head_mask = r_h == c_h
    for c in range(tile // GM_CHUNK):
        rows = slice(c * GM_CHUNK, (c + 1) * GM_CHUNK)
        vc = vn[rows].astype(BF16)
        rhs = jnp.where(head_mask, jnp.concatenate([vc] * GM_HEADS, axis=0), jnp.zeros((), BF16))
        s = jnp.dot(wm, rhs, preferred_element_type=F32) + bias_ref[...]
        ya_ref[rows, :] = u[rows] * s

    zz = z[:, 5 * GROUP_W:6 * GROUP_W] * z[:, 3 * GROUP_W:4 * GROUP_W]
    row = lax.broadcasted_iota(jnp.int32, zz.shape, 0)
    prev = prev_ref[...]
    z1 = jnp.where(row == 0, prev[7:8], pltpu.roll(zz, 1, 0))
    z2 = jnp.where(row == 0, prev[6:7], jnp.where(row == 1, prev[7:8], pltpu.roll(zz, 2, 0)))
    cw = cw_ref[...]
    y = cb_ref[...] + cw[0:1] * z2 + cw[1:2] * z1 + cw[2:3] * zz
    yc_ref[...] = z[:, 4 * GROUP_W:5 * GROUP_W] * y
    prev_ref[...] = zz[tile - 8:]
    tail_ref[...] = zz[tile - 8:]


def _inproj_mix(x, p, l, batch, seq, tile):
    nt = seq // tile
    rows = batch * seq
    row_blk = lambda wd: pl.BlockSpec((tile, wd), lambda b, j: (b * nt + j, 0))
    widths = (GROUP_W, GROUP_W, GROUP_W, D_TM)
    return pl.pallas_call(
        functools.partial(_inproj_mix_kernel, tile=tile),
        grid=(batch, nt),
        in_specs=[row_blk(D_MODEL), _layer((1, D_MODEL), l), _layer((D_MODEL, IN_COLS), l),
                  _layer((1, GROUP_W), l), _layer((1, GROUP_W), l),
                  _layer((GM_CHUNK, GM_HEADS * GM_CHUNK), l), _layer((GM_CHUNK, GROUP_W), l),
                  _layer((3, GROUP_W), l), _layer((1, GROUP_W), l)],
        out_specs=[row_blk(wd) for wd in widths]
        + [pl.BlockSpec((None, 8, GROUP_W), lambda b, j: (b, 0, 0))],
        out_shape=[jax.ShapeDtypeStruct((rows, wd), F32) for wd in widths]
        + [jax.ShapeDtypeStruct((batch, 8, GROUP_W), F32)],
        scratch_shapes=[pltpu.VMEM((8, GROUP_W), F32)],
        compiler_params=_cparams(("parallel", "arbitrary")),
    )(x, p["n1"], p["w_in"], p["lng"], p["lnb"], p["wcat"], p["bias"], p["cw"], p["cb"])


def _s5_prep_kernel(are_ref, aim_ref, ldt_ref, bre_ref, bim_ref, cre_ref, cim_ref,
                    lam_ref, bb_ref, cc_ref):
    lam_re = jnp.minimum(are_ref[...], -1e-4)
    lam_im = aim_ref[...]
    dt = jnp.exp(ldt_ref[...])
    mag = jnp.exp(lam_re * dt)
    lb_re = mag * jnp.cos(lam_im * dt)
    lb_im = mag * jnp.sin(lam_im * dt)
    den = lam_re * lam_re + lam_im * lam_im
    f_re = ((lb_re - 1.0) * lam_re + lb_im * lam_im) / den
    f_im = (lb_im * lam_re - (lb_re - 1.0) * lam_im) / den
    lam_ref[0:1, :] = lb_re
    lam_ref[1:2, :] = lb_im
    br, bi = bre_ref[...], bim_ref[...]
    grp_r = lax.broadcasted_iota(jnp.int32, (GROUP_W, SSM_S), 0) // SSM_CH
    grp_c = lax.broadcasted_iota(jnp.int32, (GROUP_W, SSM_S), 1) // SSM_P
    m = grp_r == grp_c
    bb_ref[:, :SSM_S] = jnp.where(m, f_re * br - f_im * bi, 0.0).astype(BF16)
    bb_ref[:, SSM_S:] = jnp.where(m, f_re * bi + f_im * br, 0.0).astype(BF16)
    grp_r2 = lax.broadcasted_iota(jnp.int32, (SSM_S, GROUP_W), 0) // SSM_P
    grp_c2 = lax.broadcasted_iota(jnp.int32, (SSM_S, GROUP_W), 1) // SSM_CH
    m2 = grp_r2 == grp_c2
    cc_ref[:SSM_S, :] = jnp.where(m2, cre_ref[...], 0.0).astype(BF16)
    cc_ref[SSM_S:, :] = jnp.where(m2, -cim_ref[...], 0.0).astype(BF16)


def _s5_prep(a_re, a_im, log_dt, b_re, b_im, c_re, c_im):
    depth = a_re.shape[0]
    flat = lambda p: p.reshape(depth, 1, SSM_S)
    b_exp = lambda b: jnp.tile(jnp.transpose(b, (0, 3, 1, 2)).reshape(depth, SSM_CH, SSM_S),
                               (1, SSM_GROUPS, 1))
    c_exp = lambda c: jnp.tile(jnp.transpose(c, (0, 1, 3, 2)).reshape(depth, SSM_S, SSM_CH),
                               (1, 1, SSM_GROUPS))
    lyr = lambda shape: pl.BlockSpec((None,) + shape, lambda l: (l, 0, 0))
    return pl.pallas_call(
        _s5_prep_kernel,
        grid=(depth,),
        in_specs=[lyr((1, SSM_S))] * 3 + [lyr((GROUP_W, SSM_S))] * 2 + [lyr((SSM_S, GROUP_W))] * 2,
        out_specs=[lyr((2, SSM_S)), lyr((GROUP_W, 2 * SSM_S)), lyr((2 * SSM_S, GROUP_W))],
        out_shape=[jax.ShapeDtypeStruct((depth, 2, SSM_S), F32),
                   jax.ShapeDtypeStruct((depth, GROUP_W, 2 * SSM_S), BF16),
                   jax.ShapeDtypeStruct((depth, 2 * SSM_S, GROUP_W), BF16)],
        compiler_params=_cparams(("arbitrary",)),
    )(flat(a_re), flat(a_im), flat(log_dt), b_exp(b_re), b_exp(b_im), c_exp(c_re), c_exp(c_im))


def _s5_output(st, u, cc, d, glu_w, glu_b):
    y = jnp.dot(st.astype(BF16), cc, preferred_element_type=F32) + d * u
    y = jax.nn.gelu(y)
    return y * jax.nn.sigmoid(jnp.dot(y.astype(BF16), glu_w, preferred_element_type=F32) + glu_b)


_S5_PARAMS = ("lam", "bb", "cc", "ssm_d", "glu_w", "glu_b")
_S5_SHAPES = ((2, SSM_S), (GROUP_W, 2 * SSM_S), (2 * SSM_S, GROUP_W), (1, GROUP_W),
              (GROUP_W, GROUP_W), (1, GROUP_W))


def _s5_kernel(u_ref, lam_ref, bb_ref, cc_ref, d_ref, gw_ref, gb_ref, y_ref, fin_ref,
               bu_ref, st_ref, *, batch, tsteps):
    @pl.when(pl.program_id(0) == 0)
    def _():
        st_ref[...] = jnp.zeros_like(st_ref)

    u = u_ref[...]
    bu_ref[...] = jnp.dot(u.astype(BF16), bb_ref[...], preferred_element_type=F32)
    lam = lam_ref[...]
    lr = jnp.broadcast_to(lam[0:1], (batch, SSM_S))
    li = jnp.broadcast_to(lam[1:2], (batch, SSM_S))

    def step(t, carry):
        s_re, s_im = carry
        rows = pl.ds(pl.multiple_of(t * batch, batch), batch)
        n_re = lr * s_re - li * s_im + bu_ref[rows, :SSM_S]
        n_im = lr * s_im + li * s_re + bu_ref[rows, SSM_S:]
        bu_ref[rows, :SSM_S] = n_re
        bu_ref[rows, SSM_S:] = n_im
        return n_re, n_im

    st0 = st_ref[...]
    s_re, s_im = lax.fori_loop(0, tsteps, step, (st0[:, :SSM_S], st0[:, SSM_S:]), unroll=4)
    st_ref[:, :SSM_S] = s_re
    st_ref[:, SSM_S:] = s_im
    fin_ref[:, :SSM_S] = s_re
    fin_ref[:, SSM_S:] = s_im
    y_ref[...] = _s5_output(bu_ref[...], u, cc_ref[...], d_ref[...], gw_ref[...], gb_ref[...])


def _s5(u_tm, p, l, batch, seq):
    tsteps = min(S5_TCHUNK, seq)
    rows = tsteps * batch
    return pl.pallas_call(
        functools.partial(_s5_kernel, batch=batch, tsteps=tsteps),
        grid=(seq // tsteps,),
        in_specs=[pl.BlockSpec((rows, GROUP_W), lambda i: (i, 0))]
        + [_layer(s, l) for s in _S5_SHAPES],
        out_specs=[pl.BlockSpec((rows, GROUP_W), lambda i: (i, 0)),
                   _full((batch, 2 * SSM_S))],
        out_shape=[jax.ShapeDtypeStruct((seq * batch, GROUP_W), F32),
                   jax.ShapeDtypeStruct((batch, 2 * SSM_S), F32)],
        scratch_shapes=[pltpu.VMEM((rows, 2 * SSM_S), F32),
                        pltpu.VMEM((batch, 2 * SSM_S), F32)],
        compiler_params=_cparams(("arbitrary",)),
    )(u_tm, *[p[n] for n in _S5_PARAMS])


def _wkv_inputs(zd, zprev, p, bd_ones):
    zs = zd + p["mu"] * (zprev - zd)
    r = zs[:, 0:GROUP_W]
    k = zs[:, GROUP_W:2 * GROUP_W]
    v = zs[:, 2 * GROUP_W:3 * GROUP_W]
    lora = zs[:, 3 * GROUP_W:]
    w = -_softplus(-(p["w0"] + _dot(jnp.tanh(lora), p["w2"]))) - 0.5
    logd = -jnp.exp(w)
    a = jax.nn.sigmoid(p["a0"] + _dot(lora, p["a2"]))
    g = _dot(jax.nn.sigmoid(lora), p["g2"])
    kk = k * p["k_k"]
    nrm = jnp.sqrt(_dot_split(kk * kk, bd_ones))
    kk = kk / jnp.maximum(nrm, 1e-12)
    k2 = k * (1.0 + (a - 1.0) * p["k_a"])
    return r, logd, k2, v, kk, a, g


def _wkv_output(o, r, k2, v, g, p, bd_ones):
    inv_n = 1.0 / WKV_N
    m = _dot_split(o, bd_ones) * inv_n
    var = _dot_split(jnp.square(o - m), bd_ones) * inv_n
    on = (o - m) * lax.rsqrt(var + WKV_LN_EPS) * p["ln_g"] + p["ln_b"]
    bonus = _dot_split(r * k2 * p["r_k"], bd_ones) * v
    return (on + bonus) * g


_WKV_PARAMS = ("mu", "w0", "w2", "a0", "a2", "g2", "k_k", "k_a", "r_k", "ln_g", "ln_b")
_WKV_PARAM_SHAPES = {"mu": (1, D_TM), "w2": (LORA_PAD, GROUP_W), "a2": (LORA_PAD, GROUP_W),
                     "g2": (LORA_PAD, GROUP_W)}


def _wkv_param_specs(l):
    return [_layer(_WKV_PARAM_SHAPES.get(n, (1, GROUP_W)), l) for n in _WKV_PARAMS]


def _bd_mask(n):
    hr = lax.broadcasted_iota(jnp.int32, (n, n), 0) // (n // WKV_HEADS)
    hc = lax.broadcasted_iota(jnp.int32, (n, n), 1) // (n // WKV_HEADS)
    return hr == hc


def _expand(xp, bd_bf16):
    return jnp.concatenate([xp.astype(BF16)] * WKV_HEADS, axis=0) * bd_bf16


def _wkv_chunks(r, logd, k2, v, kk, a, states, tri_ones, bd256):
    c = WKV_CHUNK
    n = len(states)
    rows = [slice(i * c, (i + 1) * c) for i in range(n)]
    t_i = lax.broadcasted_iota(jnp.int32, (c, GROUP_W), 0)
    s_i = lax.broadcasted_iota(jnp.int32, (c, GROUP_W), 1) % c
    strict = s_i < t_i
    incl = s_i <= t_i
    bd_bf16 = jnp.where(bd256, 1.0, 0.0).astype(BF16)
    ex = lambda xp: _expand(xp, bd_bf16)

    cum = [jnp.dot(tri_ones, logd[rw], preferred_element_type=F32, precision=lax.Precision.HIGHEST)
           for rw in rows]
    g_last = [cm[c - 1:c] for cm in cum]
    a_t, r_t, b_t, k_t, b_h, k_h, v_c = [], [], [], [], [], [], []
    for i, rw in enumerate(rows):
        e_neg = jnp.exp(-cum[i])
        e_end = jnp.exp(g_last[i] - cum[i])
        bvec = kk[rw] * a[rw]
        a_t.append(-kk[rw] * jnp.exp(cum[i] - logd[rw]))
        r_t.append(r[rw] * jnp.exp(cum[i]))
        b_t.append(bvec * e_neg)
        k_t.append(k2[rw] * e_neg)
        b_h.append(bvec * e_end)
        k_h.append(k2[rw] * e_end)
        v_c.append(v[rw])

    ar = [jnp.concatenate([a_t[i], r_t[i]], axis=0) for i in range(n)]
    p_b = [_dot_nt(ar[i], ex(b_t[i])) for i in range(n)]
    p_k = [_dot_nt(ar[i], ex(k_t[i])) for i in range(n)]
    l_p = [jnp.where(strict, p[:c], 0.0) for p in p_b]
    aak = [jnp.where(strict, p[:c], 0.0) for p in p_k]
    rb = [jnp.where(incl, p[c:], 0.0) for p in p_b]
    rk = [jnp.where(incl, p[c:], 0.0) for p in p_k]

    def off_diag(m):
        return (t_i // (2 * m) == s_i // (2 * m)) & (t_i % (2 * m) >= m) & (s_i % (2 * m) < m)

    eye_p = jnp.where(s_i == t_i, 1.0, 0.0)
    t_p = [eye_p + jnp.where(off_diag(1), l_p[i], 0.0) for i in range(n)]
    m = 2
    while m < c:
        off = off_diag(m)
        x = [_dot(t_p[i], ex(jnp.where(off, l_p[i], 0.0))) for i in range(n)]
        t_p = [t_p[i] + _dot(x[i], ex(t_p[i])) for i in range(n)]
        m *= 2

    v_bd = [ex(v_c[i]) for i in range(n)]
    a2 = [_dot(t_p[i], ex(a_t[i])) for i in range(n)]
    av = [_dot(aak[i], v_bd[i]) for i in range(n)]
    w0 = [_dot(t_p[i], ex(av[i])) for i in range(n)]
    from_s = [_dot_nt(jnp.concatenate([a2[i], r_t[i]], axis=0), states[i]) for i in range(n)]
    w = [from_s[i][:c] + w0[i] for i in range(n)]
    o = [from_s[i][c:] + _dot(jnp.concatenate([rb[i], rk[i]], axis=1),
                              jnp.concatenate([ex(w[i]), v_bd[i]], axis=0)) for i in range(n)]
    upd = [_dot_tn(jnp.concatenate([w[i], v_c[i]], axis=0), jnp.concatenate([b_h[i], k_h[i]], axis=0))
           for i in range(n)]
    s_new = [states[i] * jnp.exp(g_last[i]) + jnp.where(bd256, upd[i], 0.0) for i in range(n)]
    return jnp.concatenate(o, axis=0), s_new


def _rwkv_kernel(zd_ref, *rest, nb):
    prm = {n: ref[...] for n, ref in zip(_WKV_PARAMS, rest)}
    y_ref, sfin_ref, prev_ref, s_ref = rest[len(_WKV_PARAMS):]
    c = WKV_CHUNK

    @pl.when(pl.program_id(0) == 0)
    def _():
        prev_ref[...] = jnp.zeros_like(prev_ref)
        s_ref[...] = jnp.zeros_like(s_ref)

    bd256 = _bd_mask(GROUP_W)
    bd_ones = jnp.where(bd256, 1.0, 0.0).astype(BF16)
    tri_ones = jnp.where(lax.broadcasted_iota(jnp.int32, (c, c), 1)
                         <= lax.broadcasted_iota(jnp.int32, (c, c), 0), 1.0, 0.0)
    zd3 = zd_ref[...]
    zd = zd3.reshape(nb * c, D_TM)
    first = lax.broadcasted_iota(jnp.int32, (nb, c, D_TM), 1) == 0
    carried = jnp.broadcast_to(prev_ref[:, 7:8, :], (nb, c, D_TM))
    zprev = jnp.where(first, carried, pltpu.roll(zd, 1, 0).reshape(nb, c, D_TM)).reshape(nb * c, D_TM)
    r, logd, k2, v, kk, a, g = _wkv_inputs(zd, zprev, prm, bd_ones)
    o, s_new = _wkv_chunks(r, logd, k2, v, kk, a, [s_ref[b] for b in range(nb)], tri_ones, bd256)
    y_ref[...] = _wkv_output(o, r, k2, v, g, prm, bd_ones).reshape(nb, c, GROUP_W)
    for b in range(nb):
        s_ref[b] = s_new[b]
        sfin_ref[b] = s_new[b]
    prev_ref[...] = zd3[:, c - 8:, :]


def _rwkv(zd, p, l, batch, seq):
    c = WKV_CHUNK
    return pl.pallas_call(
        functools.partial(_rwkv_kernel, nb=batch),
        grid=(seq // c,),
        in_specs=[pl.BlockSpec((batch, c, D_TM), lambda j: (0, j, 0))] + _wkv_param_specs(l),
        out_specs=[pl.BlockSpec((batch, c, GROUP_W), lambda j: (0, j, 0)),
                   _full((batch, GROUP_W, GROUP_W))],
        out_shape=[jax.ShapeDtypeStruct((batch, seq, GROUP_W), F32),
                   jax.ShapeDtypeStruct((batch, GROUP_W, GROUP_W), F32)],
        scratch_shapes=[pltpu.VMEM((batch, 8, D_TM), F32), pltpu.VMEM((batch, GROUP_W, GROUP_W), F32)],
        compiler_params=_cparams(("arbitrary",)),
    )(zd.reshape(batch, seq, D_TM), *[p[n] for n in _WKV_PARAMS])


def _sample_mix_kernel(za_ref, zb_ref, zc_ref, zd_ref, shift_ref, s_ref, ssm_ref, conv_ref,
                       lng_ref, lnb_ref, w00_ref, b0_ref, lam_ref, bb_ref, cc_ref, d_ref, gw_ref,
                       gb_ref, cw_ref, cb_ref, *rest):
    prm = {n: ref[...] for n, ref in zip(_WKV_PARAMS, rest)}
    ya_ref, yb_ref, yc_ref, yd_ref, vn_ref, snew_ref, ssmnew_ref, convnew_ref = rest[len(_WKV_PARAMS):]

    za = za_ref[...]
    vn = _gm_norm(za[:, GROUP_W:], lng_ref[...], lnb_ref[...])
    vn_ref[...] = vn
    ya_ref[...] = jax.nn.gelu(za[:, :GROUP_W]) * (w00_ref[...] * vn + b0_ref[...])

    u = zb_ref[...]
    bu = jnp.dot(u.astype(BF16), bb_ref[...], preferred_element_type=F32)
    lam = lam_ref[...]
    lr, li = lam[0:1], lam[1:2]
    st = ssm_ref[...]
    s_re, s_im = st[:, :SSM_S], st[:, SSM_S:]
    st_new = jnp.concatenate([lr * s_re - li * s_im + bu[:, :SSM_S],
                              lr * s_im + li * s_re + bu[:, SSM_S:]], axis=1)
    ssmnew_ref[...] = st_new
    yb_ref[...] = _s5_output(st_new, u, cc_ref[...], d_ref[...], gw_ref[...], gb_ref[...])

    zc = zc_ref[...]
    z = zc[:, 2 * GROUP_W:] * zc[:, :GROUP_W]
    cw = cw_ref[...]
    buf = conv_ref[...]
    y = cb_ref[...] + cw[0:1] * buf[:, :GROUP_W] + cw[1:2] * buf[:, GROUP_W:] + cw[2:3] * z
    yc_ref[...] = zc[:, GROUP_W:2 * GROUP_W] * y
    convnew_ref[:, :GROUP_W] = buf[:, GROUP_W:]
    convnew_ref[:, GROUP_W:] = z

    bd256 = _bd_mask(GROUP_W)
    bd_ones = jnp.where(bd256, 1.0, 0.0).astype(BF16)
    zd = zd_ref[...]
    r, logd, k2, v, kk, a, g = _wkv_inputs(zd, shift_ref[...], prm, bd_ones)
    bt = zd.shape[0]
    s = s_ref[...]
    eye4 = (lax.broadcasted_iota(jnp.int32, (WKV_N, GROUP_W), 0)
            == lax.broadcasted_iota(jnp.int32, (WKV_N, GROUP_W), 1) % WKV_N)

    def head_sum(x3):
        return _dot_split(x3.reshape(bt * WKV_N, GROUP_W), bd_ones).reshape(bt, WKV_N, GROUP_W)

    sa = head_sum(s * (-kk)[:, None, :])
    vcol = head_sum(jnp.where(eye4[None], v[:, None, :], 0.0))
    s_new = (s * jnp.exp(logd)[:, None, :] + sa * (kk * a)[:, None, :] + vcol * k2[:, None, :])
    snew_ref[...] = s_new
    o_rep = head_sum(s_new * r[:, None, :])
    o = jnp.sum(jnp.where(eye4[None], o_rep, 0.0), axis=1)
    yd_ref[...] = _wkv_output(o, r, k2, v, g, prm, bd_ones)


def _sample_mix(za, zb, zc, zd, shift, s_t, ssm, conv, p, l, bt):
    rows = za.shape[0]
    row_blk = lambda wd: pl.BlockSpec((bt, wd), lambda i: (i, 0))
    st_blk = lambda wd: pl.BlockSpec((None, bt, wd), lambda i: (l, i, 0))
    s_in = pl.BlockSpec((None, bt, WKV_N, GROUP_W), lambda i: (l, i, 0, 0))
    s_out = pl.BlockSpec((bt, WKV_N, GROUP_W), lambda i: (i, 0, 0))
    vec = _layer((1, GROUP_W), l)
    return pl.pallas_call(
        _sample_mix_kernel,
        grid=(rows // bt,),
        in_specs=[row_blk(2 * GROUP_W), row_blk(GROUP_W), row_blk(3 * GROUP_W), row_blk(D_TM),
                  st_blk(D_TM), s_in, st_blk(2 * SSM_S), st_blk(2 * GROUP_W),
                  vec, vec, vec, vec]
        + [_layer(s, l) for s in _S5_SHAPES]
        + [_layer((3, GROUP_W), l), vec] + _wkv_param_specs(l),
        out_specs=[row_blk(GROUP_W)] * 5 + [s_out, row_blk(2 * SSM_S), row_blk(2 * GROUP_W)],
        out_shape=[jax.ShapeDtypeStruct((rows, GROUP_W), F32)] * 5
        + [jax.ShapeDtypeStruct((rows, WKV_N, GROUP_W), F32),
           jax.ShapeDtypeStruct((rows, 2 * SSM_S), F32),
           jax.ShapeDtypeStruct((rows, 2 * GROUP_W), F32)],
        compiler_params=_cparams(("parallel",)),
    )(za, zb, zc, zd, shift, s_t, ssm, conv, p["lng"], p["lnb"], p["w00"], p["b0"],
      *[p[n] for n in _S5_PARAMS], p["cw"], p["cb"], *[p[n] for n in _WKV_PARAMS])


def _out_ffn_kernel(x_ref, ya_ref, yb_ref, yc_ref, yd_ref, wo_ref, g2_ref, wg_ref, wu_ref, wd_ref,
                    gf_ref, o_ref, h_ref, *, final):
    c = pl.program_id(1)

    @pl.when(c == 0)
    def _():
        x = x_ref[...]
        for i, y_ref in enumerate((ya_ref, yb_ref, yc_ref, yd_ref)):
            x = x + jnp.dot(y_ref[...].astype(BF16), wo_ref[i * GROUP_W:(i + 1) * GROUP_W, :],
                            preferred_element_type=F32)
        o_ref[...] = x
        h_ref[...] = _rms(x, g2_ref[...]).astype(BF16)

    h = h_ref[...]
    gate = jnp.dot(h, wg_ref[...], preferred_element_type=F32)
    up = jnp.dot(h, wu_ref[...], preferred_element_type=F32)
    act = (gate * jax.nn.sigmoid(gate) * up).astype(BF16)
    o_ref[...] += jnp.dot(act, wd_ref[...], preferred_element_type=F32)

    if final:
        @pl.when(c == FF_SPLIT - 1)
        def _():
            o_ref[...] = _rms(o_ref[...], gf_ref[...])


def _out_ffn(x, ys, p, l, tm, final):
    rows = x.shape[0]
    fc = D_FF // FF_SPLIT
    row_blk = lambda wd_: pl.BlockSpec((tm, wd_), lambda i, c: (i, 0))
    return pl.pallas_call(
        functools.partial(_out_ffn_kernel, final=final),
        grid=(rows // tm, FF_SPLIT),
        in_specs=[row_blk(D_MODEL)] + [row_blk(GROUP_W)] * 4
        + [_layer((D_MODEL, D_MODEL), l), _layer((1, D_MODEL), l),
           pl.BlockSpec((None, D_MODEL, fc), lambda i, c: (l, 0, c)),
           pl.BlockSpec((None, D_MODEL, fc), lambda i, c: (l, 0, FF_SPLIT + c)),
           pl.BlockSpec((None, fc, D_MODEL), lambda i, c: (l, c, 0)),
           _full((1, D_MODEL))],
        out_specs=row_blk(D_MODEL),
        out_shape=jax.ShapeDtypeStruct((rows, D_MODEL), F32),
        scratch_shapes=[pltpu.VMEM((tm, D_MODEL), BF16)],
        compiler_params=_cparams(("parallel", "arbitrary")),
    )(x, *ys, p["w_out"], p["n2"], p["w_gu"], p["w_gu"], p["w_down"], p["gf"])


def _pad_lora(w, start):
    return jnp.pad(w.astype(BF16), ((0, 0), (start, LORA_PAD - start - w.shape[1]), (0, 0)))


def kernel(x_prompt, x_sample, state_wkv, state_shift, state_ssm_re, state_ssm_im, state_conv,
           norm1_g, w_in, gm_ln_g, gm_ln_b, gm_ws, gm_bs,
           ssm_a_re, ssm_a_im, ssm_log_dt, ssm_b_re, ssm_b_im, ssm_c_re, ssm_c_im, ssm_d,
           ssm_glu_w, ssm_glu_b, conv_w, conv_b,
           tm_mu, tm_w0, tm_w2, tm_a0, tm_a2, tm_g2, tm_k_k, tm_k_a, tm_r_k, tm_ln_g, tm_ln_b,
           w_out, norm2_g, ffn_w_gu, ffn_w_down, norm_f_g):
    depth = w_in.shape[0]
    bp, seq, _ = x_prompt.shape
    bs = x_sample.shape[0]
    assert x_sample.shape[1] == 1 and seq % GM_CHUNK == 0 and seq % WKV_CHUNK == 0
    head_d = GROUP_W // GM_HEADS
    rowv = lambda p: p.reshape(depth, 1, -1)

    lam, bb, cc = _s5_prep(ssm_a_re, ssm_a_im, ssm_log_dt, ssm_b_re, ssm_b_im, ssm_c_re, ssm_c_im)
    p = {
        "n1": rowv(norm1_g), "n2": rowv(norm2_g), "w_in": w_in.astype(BF16),
        "lng": rowv(gm_ln_g), "lnb": rowv(gm_ln_b),
        "wcat": jnp.transpose(gm_ws, (0, 2, 1, 3)).reshape(depth, GM_CHUNK, GM_HEADS * GM_CHUNK),
        "bias": jnp.repeat(jnp.transpose(gm_bs, (0, 2, 1)), head_d, axis=2),
        "w00": rowv(jnp.repeat(gm_ws[:, :, 0, 0], head_d, axis=1)),
        "b0": rowv(jnp.repeat(gm_bs[:, :, 0], head_d, axis=1)),
        "lam": lam, "bb": bb, "cc": cc, "ssm_d": rowv(ssm_d),
        "glu_w": ssm_glu_w.astype(BF16), "glu_b": rowv(ssm_glu_b),
        "cw": conv_w, "cb": rowv(conv_b),
        "mu": rowv(tm_mu), "w0": rowv(tm_w0), "w2": _pad_lora(tm_w2, 0),
        "a0": rowv(tm_a0), "a2": _pad_lora(tm_a2, 32), "g2": _pad_lora(tm_g2, 64),
        "k_k": rowv(tm_k_k), "k_a": rowv(tm_k_a), "r_k": rowv(tm_r_k),
        "ln_g": rowv(tm_ln_g), "ln_b": rowv(tm_ln_b),
        "w_out": w_out.astype(BF16), "w_gu": ffn_w_gu.astype(BF16), "w_down": ffn_w_down.astype(BF16),
        "gf": norm_f_g.reshape(1, D_MODEL),
    }

    xp = x_prompt.reshape(bp * seq, D_MODEL)
    xs = x_sample.reshape(bs, D_MODEL)
    wkv_s_in = jnp.transpose(state_wkv, (0, 1, 3, 2, 4)).reshape(depth, bs, WKV_N, GROUP_W)
    ssm_s_in = jnp.concatenate([state_ssm_re.reshape(depth, bs, SSM_S),
                                state_ssm_im.reshape(depth, bs, SSM_S)], axis=-1)
    conv_s_in = state_conv.reshape(depth, bs, 2 * GROUP_W)

    outs = {k: [] for k in ("wkv_p", "wkv_s", "sh_p", "sh_s", "ssm_p", "ssm_s", "cv_p", "cv_s", "chv")}
    tm_p = ROW_TILE if seq % ROW_TILE == 0 else GM_CHUNK
    bt_s = SAMPLE_TILE if bs % SAMPLE_TILE == 0 else bs
    for l in range(depth):
        final = l == depth - 1

        ya, zb, yc, zd, tail = _inproj_mix(xp, p, l, bp, seq, tm_p)
        zb_tm = jnp.transpose(zb.reshape(bp, seq, GROUP_W), (1, 0, 2)).reshape(seq * bp, GROUP_W)
        yb_tm, ssm_fin = _s5(zb_tm, p, l, bp, seq)
        yb = jnp.transpose(yb_tm.reshape(seq, bp, GROUP_W), (1, 0, 2)).reshape(bp * seq, GROUP_W)
        yd, wkv_fin = _rwkv(zd, p, l, bp, seq)
        xp = _out_ffn(xp, (ya, yb, yc, yd.reshape(bp * seq, GROUP_W)), p, l, tm_p, final)
        outs["wkv_p"].append(wkv_fin)
        outs["sh_p"].append(zd.reshape(bp, seq, D_TM)[:, -1])
        outs["ssm_p"].append(ssm_fin)
        outs["cv_p"].append(tail[:, 6:8])

        za, zb, zc, zd = _inproj(xs, p["n1"], p["w_in"], l)
        ya, yb, yc, yd, vn, s_new, ssm_new, conv_new = _sample_mix(
            za, zb, zc, zd, state_shift, wkv_s_in, ssm_s_in, conv_s_in, p, l, bt_s)
        xs = _out_ffn(xs, (ya, yb, yc, yd), p, l, bs, final)
        outs["wkv_s"].append(s_new)
        outs["sh_s"].append(zd)
        outs["ssm_s"].append(ssm_new)
        outs["cv_s"].append(conv_new.reshape(bs, 2, GROUP_W))
        outs["chv"].append(vn.reshape(bs, 1, GROUP_W))

    def wkv_blocks(s_bd):
        s5d = s_bd.reshape(depth, -1, WKV_HEADS, WKV_N, WKV_HEADS, WKV_N)
        return jnp.stack([s5d[:, :, h, :, h, :] for h in range(WKV_HEADS)], axis=2)

    wkv_p = wkv_blocks(jnp.stack(outs["wkv_p"]))
    wkv_s = jnp.transpose(jnp.stack(outs["wkv_s"]).reshape(depth, bs, WKV_N, WKV_HEADS, WKV_N),
                          (0, 1, 3, 2, 4))
    ssm_p = jnp.stack(outs["ssm_p"])
    ssm_s = jnp.stack(outs["ssm_s"])
    split = lambda s, i: s[..., i * SSM_S:(i + 1) * SSM_S].reshape(depth, -1, SSM_GROUPS, SSM_P)
    return (xp.reshape(bp, seq, D_MODEL), xs.reshape(bs, 1, D_MODEL),
            wkv_p, wkv_s,
            jnp.stack(outs["sh_p"]), jnp.stack(outs["sh_s"]),
            split(ssm_p, 0), split(ssm_s, 0), split(ssm_p, 1), split(ssm_s, 1),
            jnp.stack(outs["cv_p"]), jnp.stack(outs["cv_s"]),
            jnp.stack(outs["chv"]))
```

```python
import functools

import jax
import jax.numpy as jnp
from jax import lax
from jax.experimental import pallas as pl
from jax.experimental.pallas import tpu as pltpu

F32 = jnp.float32
BF16 = jnp.bfloat16

D_MODEL = 1024
GROUP_W = 256
GM_CHUNK = 128
GM_HEADS = 4
SSM_CH = 16
SSM_GROUPS = 16
SSM_P = 64
SSM_S = SSM_GROUPS * SSM_P
WKV_N = 64
WKV_HEADS = 4
LORA_PAD = 128
D_TM = 3 * GROUP_W + LORA_PAD
IN_COLS = 6 * GROUP_W + D_TM
D_FF = 2816
FF_SPLIT_PROMPT = 2
FF_SPLIT_SAMPLE = 11
NORM_EPS = 1e-6
GM_LN_EPS = 1e-5
WKV_LN_EPS = 64e-5

WKV_CHUNK = 64
S5_TCHUNK = 128
ROW_TILE = 512
SAMPLE_TILE = 32
VMEM_LIMIT = 56 * 1024 * 1024


def _cparams(sem):
    return pltpu.CompilerParams(dimension_semantics=sem, vmem_limit_bytes=VMEM_LIMIT)


def _full(shape):
    n = len(shape)
    return pl.BlockSpec(shape, lambda *_: (0,) * n)


def _layer(shape, l):
    n = len(shape)
    return pl.BlockSpec((None,) + tuple(shape), lambda *_: (l,) + (0,) * n)


def _dot(a, b):
    return jnp.dot(a.astype(BF16), b.astype(BF16), preferred_element_type=F32)


def _dot_nt(a, b):
    return lax.dot_general(a.astype(BF16), b.astype(BF16), (((1,), (1,)), ((), ())),
                           preferred_element_type=F32)


def _dot_tn(a, b):
    return lax.dot_general(a.astype(BF16), b.astype(BF16), (((0,), (0,)), ((), ())),
                           preferred_element_type=F32)


def _dot_split(x, ones_bf16):
    hi = x.astype(BF16)
    lo = (x - hi.astype(F32)).astype(BF16)
    return (jnp.dot(hi, ones_bf16, preferred_element_type=F32)
            + jnp.dot(lo, ones_bf16, preferred_element_type=F32))


def _rms(x, g):
    return x * lax.rsqrt(jnp.mean(x * x, axis=-1, keepdims=True) + NORM_EPS) * g


def _softplus(y):
    return jnp.maximum(y, 0.0) + jnp.log1p(jnp.exp(-jnp.abs(y)))


def _gm_norm(zav, ln_g, ln_b):
    vf = jax.nn.gelu(zav)
    mu = jnp.mean(vf, axis=-1, keepdims=True)
    var = jnp.mean(jnp.square(vf - mu), axis=-1, keepdims=True)
    return (vf - mu) * lax.rsqrt(var + GM_LN_EPS) * ln_g + ln_b


def _inproj_kernel(x_ref, g_ref, w_ref, za_ref, zb_ref, zc_ref, zd_ref):
    h = _rms(x_ref[...], g_ref[...])
    z = jnp.dot(h.astype(BF16), w_ref[...], preferred_element_type=F32)
    za_ref[...] = z[:, 0:2 * GROUP_W]
    zb_ref[...] = z[:, 2 * GROUP_W:3 * GROUP_W]
    zc_ref[...] = z[:, 3 * GROUP_W:6 * GROUP_W]
    zd_ref[...] = z[:, 6 * GROUP_W:]


def _inproj(x, g, w, l):
    rows = x.shape[0]
    widths = (2 * GROUP_W, GROUP_W, 3 * GROUP_W, D_TM)
    return pl.pallas_call(
        _inproj_kernel,
        grid=(1,),
        in_specs=[_full((rows, D_MODEL)), _layer((1, D_MODEL), l), _layer((D_MODEL, IN_COLS), l)],
        out_specs=[_full((rows, wd)) for wd in widths],
        out_shape=[jax.ShapeDtypeStruct((rows, wd), F32) for wd in widths],
        compiler_params=_cparams(("arbitrary",)),
    )(x, g, w)


def _inproj_mix_kernel(x_ref, g_ref, w_ref, lng_ref, lnb_ref, wcat_ref, bias_ref, cw_ref, cb_ref,
                       ya_ref, zb_ref, yc_ref, zd_ref, tail_ref, prev_ref, *, tile):
    @pl.when(pl.program_id(1) == 0)
    def _():
        prev_ref[...] = jnp.zeros_like(prev_ref)

    h = _rms(x_ref[...], g_ref[...])
    z = jnp.dot(h.astype(BF16), w_ref[...], preferred_element_type=F32)
    zb_ref[...] = z[:, 2 * GROUP_W:3 * GROUP_W]
    zd_ref[...] = z[:, 6 * GROUP_W:]

    u = jax.nn.gelu(z[:, :GROUP_W])
    vn = _gm_norm(z[:, GROUP_W:2 * GROUP_W], lng_ref[...], lnb_ref[...])
    kc = GM_HEADS * GM_CHUNK
    t_i = lax.broadcasted_iota(jnp.int32, (GM_CHUNK, kc), 0)
    s_i = lax.broadcasted_iota(jnp.int32, (GM_CHUNK, kc), 1) % GM_CHUNK
    wm = jnp.where(s_i <= t_i, wcat_ref[...], 0.0).astype(BF16)
    r_h = lax.broadcasted_iota(jnp.int32, (kc, GROUP_W), 0) // GM_CHUNK
    c_h = lax.broadcasted_iota(jnp.int32, (kc, GROUP_W), 1) // (GROUP_W // GM_HEADS)
    head_mask = r_h == c_h
    for c in range(tile // GM_CHUNK):
        rows = slice(c * GM_CHUNK, (c + 1) * GM_CHUNK)
        vc = vn[rows].astype(BF16)
        rhs = jnp.where(head_mask, jnp.concatenate([vc] * GM_HEADS, axis=0), jnp.zeros((), BF16))
        s = jnp.dot(wm, rhs, preferred_element_type=F32) + bias_ref[...]
        ya_ref[rows, :] = u[rows] * s

    zz = z[:, 5 * GROUP_W:6 * GROUP_W] * z[:, 3 * GROUP_W:4 * GROUP_W]
    row = lax.broadcasted_iota(jnp.int32, zz.shape, 0)
    prev = prev_ref[...]
    z1 = jnp.where(row == 0, prev[7:8], pltpu.roll(zz, 1, 0))
    z2 = jnp.where(row == 0, prev[6:7], jnp.where(row == 1, prev[7:8], pltpu.roll(zz, 2, 0)))
    cw = cw_ref[...]
    y = cb_ref[...] + cw[0:1] * z2 + cw[1:2] * z1 + cw[2:3] * zz
    yc_ref[...] = z[:, 4 * GROUP_W:5 * GROUP_W] * y
    prev_ref[...] = zz[tile - 8:]
    tail_ref[...] = zz[tile - 8:]


def _inproj_mix(x, p, l, batch, seq, tile):
    nt = seq // tile
    rows = batch * seq
    row_blk = lambda wd: pl.BlockSpec((tile, wd), lambda b, j: (b * nt + j, 0))
    widths = (GROUP_W, GROUP_W, GROUP_W, D_TM)
    return pl.pallas_call(
        functools.partial(_inproj_mix_kernel, tile=tile),
        grid=(batch, nt),
        in_specs=[row_blk(D_MODEL), _layer((1, D_MODEL), l), _layer((D_MODEL, IN_COLS), l),
                  _layer((1, GROUP_W), l), _layer((1, GROUP_W), l),
                  _layer((GM_CHUNK, GM_HEADS * GM_CHUNK), l), _layer((GM_CHUNK, GROUP_W), l),
                  _layer((3, GROUP_W), l), _layer((1, GROUP_W), l)],
        out_specs=[row_blk(wd) for wd in widths]
        + [pl.BlockSpec((None, 8, GROUP_W), lambda b, j: (b, 0, 0))],
        out_shape=[jax.ShapeDtypeStruct((rows, wd), F32) for wd in widths]
        + [jax.ShapeDtypeStruct((batch, 8, GROUP_W), F32)],
        scratch_shapes=[pltpu.VMEM((8, GROUP_W), F32)],
        compiler_params=_cparams(("parallel", "arbitrary")),
    )(x, p["n1"], p["w_in"], p["lng"], p["lnb"], p["wcat"], p["bias"], p["cw"], p["cb"])


def _s5_prep_kernel(are_ref, aim_ref, ldt_ref, bre_ref, bim_ref, cre_ref, cim_ref,
                    lam_ref, bb_ref, cc_ref):
    lam_re = jnp.minimum(are_ref[...], -1e-4)
    lam_im = aim_ref[...]
    dt = jnp.exp(ldt_ref[...])
    mag = jnp.exp(lam_re * dt)
    lb_re = mag * jnp.cos(lam_im * dt)
    lb_im = mag * jnp.sin(lam_im * dt)
    den = lam_re * lam_re + lam_im * lam_im
    f_re = ((lb_re - 1.0) * lam_re + lb_im * lam_im) / den
    f_im = (lb_im * lam_re - (lb_re - 1.0) * lam_im) / den
    lam_ref[0:1, :] = lb_re
    lam_ref[1:2, :] = lb_im
    br, bi = bre_ref[...], bim_ref[...]
    grp_r = lax.broadcasted_iota(jnp.int32, (GROUP_W, SSM_S), 0) // SSM_CH
    grp_c = lax.broadcasted_iota(jnp.int32, (GROUP_W, SSM_S), 1) // SSM_P
    m = grp_r == grp_c
    bb_ref[:, :SSM_S] = jnp.where(m, f_re * br - f_im * bi, 0.0).astype(BF16)
    bb_ref[:, SSM_S:] = jnp.where(m, f_re * bi + f_im * br, 0.0).astype(BF16)
    grp_r2 = lax.broadcasted_iota(jnp.int32, (SSM_S, GROUP_W), 0) // SSM_P
    grp_c2 = lax.broadcasted_iota(jnp.int32, (SSM_S, GROUP_W), 1) // SSM_CH
    m2 = grp_r2 == grp_c2
    cc_ref[:SSM_S, :] = jnp.where(m2, cre_ref[...], 0.0).astype(BF16)
    cc_ref[SSM_S:, :] = jnp.where(m2, -cim_ref[...], 0.0).astype(BF16)


def _s5_prep(a_re, a_im, log_dt, b_re, b_im, c_re, c_im):
    depth = a_re.shape[0]
    flat = lambda p: p.reshape(depth, 1, SSM_S)
    b_exp = lambda b: jnp.tile(jnp.transpose(b, (0, 3, 1, 2)).reshape(depth, SSM_CH, SSM_S),
                               (1, SSM_GROUPS, 1))
    c_exp = lambda c: jnp.tile(jnp.transpose(c, (0, 1, 3, 2)).reshape(depth, SSM_S, SSM_CH),
                               (1, 1, SSM_GROUPS))
    lyr = lambda shape: pl.BlockSpec((None,) + shape, lambda l: (l, 0, 0))
    return pl.pallas_call(
        _s5_prep_kernel,
        grid=(depth,),
        in_specs=[lyr((1, SSM_S))] * 3 + [lyr((GROUP_W, SSM_S))] * 2 + [lyr((SSM_S, GROUP_W))] * 2,
        out_specs=[lyr((2, SSM_S)), lyr((GROUP_W, 2 * SSM_S)), lyr((2 * SSM_S, GROUP_W))],
        out_shape=[jax.ShapeDtypeStruct((depth, 2, SSM_S), F32),
                   jax.ShapeDtypeStruct((depth, GROUP_W, 2 * SSM_S), BF16),
                   jax.ShapeDtypeStruct((depth, 2 * SSM_S, GROUP_W), BF16)],
        compiler_params=_cparams(("arbitrary",)),
    )(flat(a_re), flat(a_im), flat(log_dt), b_exp(b_re), b_exp(b_im), c_exp(c_re), c_exp(c_im))


def _s5_output(st, u, cc, d, glu_w, glu_b):
    y = jnp.dot(st.astype(BF16), cc, preferred_element_type=F32) + d * u
    y = jax.nn.gelu(y)
    return y * jax.nn.sigmoid(jnp.dot(y.astype(BF16), glu_w, preferred_element_type=F32) + glu_b)


_S5_PARAMS = ("lam", "bb", "cc", "ssm_d", "glu_w", "glu_b")
_S5_SHAPES = ((2, SSM_S), (GROUP_W, 2 * SSM_S), (2 * SSM_S, GROUP_W), (1, GROUP_W),
              (GROUP_W, GROUP_W), (1, GROUP_W))


S5_SLABS = 2 * SSM_S // 128
S5_PITCH = S5_TCHUNK + 8


def _s5_kernel(u_ref, lam_ref, bb_ref, cc_ref, d_ref, gw_ref, gb_ref, y_ref, fin_ref,
               bu_ref, st_ref, *, batch, tsteps):
    half = S5_SLABS // 2

    @pl.when(pl.program_id(0) == 0)
    def _():
        st_ref[...] = jnp.zeros_like(st_ref)

    u = u_ref[...].reshape(batch * tsteps, GROUP_W)
    bu = jnp.dot(u.astype(BF16), bb_ref[...], preferred_element_type=F32)
    for b in range(batch):
        for s in range(S5_SLABS):
            bu_ref[s, b * S5_PITCH:b * S5_PITCH + tsteps, :] = (
                bu[b * tsteps:(b + 1) * tsteps, s * 128:(s + 1) * 128])
    lam = lam_ref[...]
    lr = [jnp.broadcast_to(lam[0:1, s * 128:(s + 1) * 128], (batch, 128)) for s in range(half)]
    li = [jnp.broadcast_to(lam[1:2, s * 128:(s + 1) * 128], (batch, 128)) for s in range(half)]

    def step(t, carry):
        rows = pl.ds(t, batch, stride=S5_PITCH)
        new = [None] * S5_SLABS
        for s in range(half):
            s_re, s_im = carry[s], carry[half + s]
            new[s] = lr[s] * s_re - li[s] * s_im + bu_ref[s, rows, :]
            new[half + s] = lr[s] * s_im + li[s] * s_re + bu_ref[half + s, rows, :]
            bu_ref[s, rows, :] = new[s]
            bu_ref[half + s, rows, :] = new[half + s]
        return tuple(new)

    st0 = st_ref[...]
    fin = lax.fori_loop(0, tsteps, step,
                        tuple(st0[:, s * 128:(s + 1) * 128] for s in range(S5_SLABS)), unroll=4)
    for s in range(S5_SLABS):
        st_ref[:, s * 128:(s + 1) * 128] = fin[s]
        fin_ref[:, s * 128:(s + 1) * 128] = fin[s]
    st = jnp.concatenate(
        [jnp.concatenate([bu_ref[s, b * S5_PITCH:b * S5_PITCH + tsteps, :] for s in range(S5_SLABS)],
                         axis=1) for b in range(batch)], axis=0)
    y = _s5_output(st, u, cc_ref[...], d_ref[...], gw_ref[...], gb_ref[...])
    y_ref[...] = y.reshape(batch, tsteps, GROUP_W)


def _s5(u, p, l, batch, seq):
    tsteps = min(S5_TCHUNK, seq)
    blk = pl.BlockSpec((batch, tsteps, GROUP_W), lambda i: (0, i, 0))
    return pl.pallas_call(
        functools.partial(_s5_kernel, batch=batch, tsteps=tsteps),
        grid=(seq // tsteps,),
        in_specs=[blk] + [_layer(s, l) for s in _S5_SHAPES],
        out_specs=[blk, _full((batch, 2 * SSM_S))],
        out_shape=[jax.ShapeDtypeStruct((batch, seq, GROUP_W), F32),
                   jax.ShapeDtypeStruct((batch, 2 * SSM_S), F32)],
        scratch_shapes=[pltpu.VMEM((S5_SLABS, batch * S5_PITCH, 128), F32),
                        pltpu.VMEM((batch, 2 * SSM_S), F32)],
        compiler_params=_cparams(("arbitrary",)),
    )(u, *[p[n] for n in _S5_PARAMS])


def _wkv_inputs(zd, zprev, p, bd_ones):
    zs = zd + p["mu"] * (zprev - zd)
    r = zs[:, 0:GROUP_W]
    k = zs[:, GROUP_W:2 * GROUP_W]
    v = zs[:, 2 * GROUP_W:3 * GROUP_W]
    lora = zs[:, 3 * GROUP_W:]
    w = -_softplus(-(p["w0"] + _dot(jnp.tanh(lora), p["w2"]))) - 0.5
    logd = -jnp.exp(w)
    a = jax.nn.sigmoid(p["a0"] + _dot(lora, p["a2"]))
    g = _dot(jax.nn.sigmoid(lora), p["g2"])
    kk = k * p["k_k"]
    nrm = jnp.sqrt(_dot_split(kk * kk, bd_ones))
    kk = kk / jnp.maximum(nrm, 1e-12)
    k2 = k * (1.0 + (a - 1.0) * p["k_a"])
    return r, logd, k2, v, kk, a, g


def _wkv_output(o, r, k2, v, g, p, bd_ones):
    inv_n = 1.0 / WKV_N
    m = _dot_split(o, bd_ones) * inv_n
    var = _dot_split(jnp.square(o - m), bd_ones) * inv_n
    on = (o - m) * lax.rsqrt(var + WKV_LN_EPS) * p["ln_g"] + p["ln_b"]
    bonus = _dot_split(r * k2 * p["r_k"], bd_ones) * v
    return (on + bonus) * g


_WKV_PARAMS = ("mu", "w0", "w2", "a0", "a2", "g2", "k_k", "k_a", "r_k", "ln_g", "ln_b")
_WKV_PARAM_SHAPES = {"mu": (1, D_TM), "w2": (LORA_PAD, GROUP_W), "a2": (LORA_PAD, GROUP_W),
                     "g2": (LORA_PAD, GROUP_W)}


def _wkv_param_specs(l):
    return [_layer(_WKV_PARAM_SHAPES.get(n, (1, GROUP_W)), l) for n in _WKV_PARAMS]


def _bd_mask(n):
    hr = lax.broadcasted_iota(jnp.int32, (n, n), 0) // (n // WKV_HEADS)
    hc = lax.broadcasted_iota(jnp.int32, (n, n), 1) // (n // WKV_HEADS)
    return hr == hc


def _expand(xp, lo_bf16):
    xb = xp.astype(BF16)
    zero = jnp.zeros_like(lo_bf16)
    hi_bf16 = 1 - lo_bf16
    t0, t1 = xb[:, :128], xb[:, 128:]
    return jnp.concatenate([jnp.concatenate([t0 * lo_bf16, zero], axis=1),
                            jnp.concatenate([t0 * hi_bf16, zero], axis=1),
                            jnp.concatenate([zero, t1 * lo_bf16], axis=1),
                            jnp.concatenate([zero, t1 * hi_bf16], axis=1)], axis=0)


def _wkv_chunks(r, logd, k2, v, kk, a, states, tri_ones, bd256):
    c = WKV_CHUNK
    n = len(states)
    rows = [slice(i * c, (i + 1) * c) for i in range(n)]
    t_i = lax.broadcasted_iota(jnp.int32, (c, GROUP_W), 0)
    s_i = lax.broadcasted_iota(jnp.int32, (c, GROUP_W), 1) % c
    strict = s_i < t_i
    incl = s_i <= t_i
    lo_bf16 = jnp.where(lax.broadcasted_iota(jnp.int32, (c, 128), 1) < WKV_N, 1.0, 0.0).astype(BF16)
    ex = lambda xp: _expand(xp, lo_bf16)

    cum = [jnp.dot(tri_ones, logd[rw], preferred_element_type=F32, precision=lax.Precision.HIGHEST)
           for rw in rows]
    g_last = [cm[c - 1:c] for cm in cum]
    a_t, r_t, b_t, k_t, b_h, k_h, v_c = [], [], [], [], [], [], []
    for i, rw in enumerate(rows):
        e_neg = jnp.exp(-cum[i])
        e_end = jnp.exp(g_last[i] - cum[i])
        bvec = kk[rw] * a[rw]
        a_t.append(-kk[rw] * jnp.exp(cum[i] - logd[rw]))
        r_t.append(r[rw] * jnp.exp(cum[i]))
        b_t.append(bvec * e_neg)
        k_t.append(k2[rw] * e_neg)
        b_h.append(bvec * e_end)
        k_h.append(k2[rw] * e_end)
        v_c.append(v[rw])

    ar = [jnp.concatenate([a_t[i], r_t[i]], axis=0) for i in range(n)]
    p_b = [_dot_nt(ar[i], ex(b_t[i])) for i in range(n)]
    p_k = [_dot_nt(ar[i], ex(k_t[i])) for i in range(n)]
    l_p = [jnp.where(strict, p[:c], 0.0) for p in p_b]
    aak = [jnp.where(strict, p[:c], 0.0) for p in p_k]
    rb = [jnp.where(incl, p[c:], 0.0) for p in p_b]
    rk = [jnp.where(incl, p[c:], 0.0) for p in p_k]

    def off_diag(m):
        return (t_i // (2 * m) == s_i // (2 * m)) & (t_i % (2 * m) >= m) & (s_i % (2 * m) < m)

    eye_p = jnp.where(s_i == t_i, 1.0, 0.0)
    t_p = [eye_p + jnp.where(off_diag(1), l_p[i], 0.0) for i in range(n)]
    m = 2
    while m < c:
        off = off_diag(m)
        x = [_dot(t_p[i], ex(jnp.where(off, l_p[i], 0.0))) for i in range(n)]
        t_p = [t_p[i] + _dot(x[i], ex(t_p[i])) for i in range(n)]
        m *= 2

    v_bd = [ex(v_c[i]) for i in range(n)]
    a2 = [_dot(t_p[i], ex(a_t[i])) for i in range(n)]
    av = [_dot(aak[i], v_bd[i]) for i in range(n)]
    w0 = [_dot(t_p[i], ex(av[i])) for i in range(n)]
    from_s = [_dot_nt(jnp.concatenate([a2[i], r_t[i]], axis=0), states[i]) for i in range(n)]
    w = [from_s[i][:c] + w0[i] for i in range(n)]
    o = [from_s[i][c:] + _dot(jnp.concatenate([rb[i], rk[i]], axis=1),
                              jnp.concatenate([ex(w[i]), v_bd[i]], axis=0)) for i in range(n)]
    upd = [_dot_tn(jnp.concatenate([w[i], v_c[i]], axis=0), jnp.concatenate([b_h[i], k_h[i]], axis=0))
           for i in range(n)]
    s_new = [states[i] * jnp.exp(g_last[i]) + jnp.where(bd256, upd[i], 0.0) for i in range(n)]
    return jnp.concatenate(o, axis=0), s_new


def _rwkv_kernel(zd_ref, *rest, nb):
    prm = {n: ref[...] for n, ref in zip(_WKV_PARAMS, rest)}
    y_ref, sfin_ref, prev_ref, s_ref = rest[len(_WKV_PARAMS):]
    c = WKV_CHUNK

    @pl.when(pl.program_id(0) == 0)
    def _():
        prev_ref[...] = jnp.zeros_like(prev_ref)
        s_ref[...] = jnp.zeros_like(s_ref)

    bd256 = _bd_mask(GROUP_W)
    bd_ones = jnp.where(bd256, 1.0, 0.0).astype(BF16)
    tri_ones = jnp.where(lax.broadcasted_iota(jnp.int32, (c, c), 1)
                         <= lax.broadcasted_iota(jnp.int32, (c, c), 0), 1.0, 0.0)
    zd3 = zd_ref[...]
    zd = zd3.reshape(nb * c, D_TM)
    first = lax.broadcasted_iota(jnp.int32, (nb, c, D_TM), 1) == 0
    carried = jnp.broadcast_to(prev_ref[:, 7:8, :], (nb, c, D_TM))
    zprev = jnp.where(first, carried, pltpu.roll(zd, 1, 0).reshape(nb, c, D_TM)).reshape(nb * c, D_TM)
    r, logd, k2, v, kk, a, g = _wkv_inputs(zd, zprev, prm, bd_ones)
    o, s_new = _wkv_chunks(r, logd, k2, v, kk, a, [s_ref[b] for b in range(nb)], tri_ones, bd256)
    y_ref[...] = _wkv_output(o, r, k2, v, g, prm, bd_ones).reshape(nb, c, GROUP_W)
    for b in range(nb):
        s_ref[b] = s_new[b]
        sfin_ref[b] = s_new[b]
    prev_ref[...] = zd3[:, c - 8:, :]


def _rwkv(zd, p, l, batch, seq):
    c = WKV_CHUNK
    return pl.pallas_call(
        functools.partial(_rwkv_kernel, nb=batch),
        grid=(seq // c,),
        in_specs=[pl.BlockSpec((batch, c, D_TM), lambda j: (0, j, 0))] + _wkv_param_specs(l),
        out_specs=[pl.BlockSpec((batch, c, GROUP_W), lambda j: (0, j, 0)),
                   _full((batch, GROUP_W, GROUP_W))],
        out_shape=[jax.ShapeDtypeStruct((batch, seq, GROUP_W), F32),
                   jax.ShapeDtypeStruct((batch, GROUP_W, GROUP_W), F32)],
        scratch_shapes=[pltpu.VMEM((batch, 8, D_TM), F32), pltpu.VMEM((batch, GROUP_W, GROUP_W), F32)],
        compiler_params=_cparams(("arbitrary",)),
    )(zd.reshape(batch, seq, D_TM), *[p[n] for n in _WKV_PARAMS])


def _sample_mix_kernel(za_ref, zb_ref, zc_ref, zd_ref, shift_ref, s_ref, ssm_ref, conv_ref,
                       lng_ref, lnb_ref, w00_ref, b0_ref, lam_ref, bb_ref, cc_ref, d_ref, gw_ref,
                       gb_ref, cw_ref, cb_ref, *rest):
    prm = {n: ref[...] for n, ref in zip(_WKV_PARAMS, rest)}
    ya_ref, yb_ref, yc_ref, yd_ref, vn_ref, snew_ref, ssmnew_ref, convnew_ref = rest[len(_WKV_PARAMS):]

    za = za_ref[...]
    vn = _gm_norm(za[:, GROUP_W:], lng_ref[...], lnb_ref[...])
    vn_ref[...] = vn
    ya_ref[...] = jax.nn.gelu(za[:, :GROUP_W]) * (w00_ref[...] * vn + b0_ref[...])

    u = zb_ref[...]
    bu = jnp.dot(u.astype(BF16), bb_ref[...], preferred_element_type=F32)
    lam = lam_ref[...]
    lr, li = lam[0:1], lam[1:2]
    st = ssm_ref[...]
    s_re, s_im = st[:, :SSM_S], st[:, SSM_S:]
    st_new = jnp.concatenate([lr * s_re - li * s_im + bu[:, :SSM_S],
                              lr * s_im + li * s_re + bu[:, SSM_S:]], axis=1)
    ssmnew_ref[...] = st_new
    yb_ref[...] = _s5_output(st_new, u, cc_ref[...], d_ref[...], gw_ref[...], gb_ref[...])

    zc = zc_ref[...]
    z = zc[:, 2 * GROUP_W:] * zc[:, :GROUP_W]
    cw = cw_ref[...]
    buf = conv_ref[...]
    y = cb_ref[...] + cw[0:1] * buf[:, :GROUP_W] + cw[1:2] * buf[:, GROUP_W:] + cw[2:3] * z
    yc_ref[...] = zc[:, GROUP_W:2 * GROUP_W] * y
    convnew_ref[:, :GROUP_W] = buf[:, GROUP_W:]
    convnew_ref[:, GROUP_W:] = z

    bd256 = _bd_mask(GROUP_W)
    bd_ones = jnp.where(bd256, 1.0, 0.0).astype(BF16)
    zd = zd_ref[...]
    r, logd, k2, v, kk, a, g = _wkv_inputs(zd, shift_ref[...], prm, bd_ones)
    bt = zd.shape[0]
    s = s_ref[...]
    eye4 = (lax.broadcasted_iota(jnp.int32, (WKV_N, GROUP_W), 0)
            == lax.broadcasted_iota(jnp.int32, (WKV_N, GROUP_W), 1) % WKV_N)

    def head_sum(x3):
        return _dot_split(x3.reshape(bt * WKV_N, GROUP_W), bd_ones).reshape(bt, WKV_N, GROUP_W)

    sa = head_sum(s * (-kk)[:, None, :])
    vcol = head_sum(jnp.where(eye4[None], v[:, None, :], 0.0))
    s_new = (s * jnp.exp(logd)[:, None, :] + sa * (kk * a)[:, None, :] + vcol * k2[:, None, :])
    snew_ref[...] = s_new
    o_rep = head_sum(s_new * r[:, None, :])
    o = jnp.sum(jnp.where(eye4[None], o_rep, 0.0), axis=1)
    yd_ref[...] = _wkv_output(o, r, k2, v, g, prm, bd_ones)


def _sample_mix(za, zb, zc, zd, shift, s_t, ssm, conv, p, l, bt):
    rows = za.shape[0]
    row_blk = lambda wd: pl.BlockSpec((bt, wd), lambda i: (i, 0))
    st_blk = lambda wd: pl.BlockSpec((None, bt, wd), lambda i: (l, i, 0))
    s_blk = pl.BlockSpec((None, bt, WKV_N, GROUP_W), lambda i: (l, i, 0, 0))
    vec = _layer((1, GROUP_W), l)
    s_arg = 5
    return pl.pallas_call(
        _sample_mix_kernel,
        grid=(rows // bt,),
        in_specs=[row_blk(2 * GROUP_W), row_blk(GROUP_W), row_blk(3 * GROUP_W), row_blk(D_TM),
                  st_blk(D_TM), s_blk, st_blk(2 * SSM_S), st_blk(2 * GROUP_W),
                  vec, vec, vec, vec]
        + [_layer(s, l) for s in _S5_SHAPES]
        + [_layer((3, GROUP_W), l), vec] + _wkv_param_specs(l),
        out_specs=[row_blk(GROUP_W)] * 5 + [s_blk, row_blk(2 * SSM_S), row_blk(2 * GROUP_W)],
        out_shape=[jax.ShapeDtypeStruct((rows, GROUP_W), F32)] * 5
        + [jax.ShapeDtypeStruct(s_t.shape, F32),
           jax.ShapeDtypeStruct((rows, 2 * SSM_S), F32),
           jax.ShapeDtypeStruct((rows, 2 * GROUP_W), F32)],
        input_output_aliases={s_arg: 5},
        compiler_params=_cparams(("parallel",)),
    )(za, zb, zc, zd, shift, s_t, ssm, conv, p["lng"], p["lnb"], p["w00"], p["b0"],
      *[p[n] for n in _S5_PARAMS], p["cw"], p["cb"], *[p[n] for n in _WKV_PARAMS])


def _out_ffn_kernel(x_ref, ya_ref, yb_ref, yc_ref, yd_ref, wo_ref, g2_ref, wg_ref, wu_ref, wd_ref,
                    gf_ref, o_ref, h_ref, *, final, ff_split):
    c = pl.program_id(1)

    @pl.when(c == 0)
    def _():
        x = x_ref[...]
        for i, y_ref in enumerate((ya_ref, yb_ref, yc_ref, yd_ref)):
            x = x + jnp.dot(y_ref[...].astype(BF16), wo_ref[i * GROUP_W:(i + 1) * GROUP_W, :],
                            preferred_element_type=F32)
        o_ref[...] = x
        h_ref[...] = _rms(x, g2_ref[...]).astype(BF16)

    h = h_ref[...]
    gate = jnp.dot(h, wg_ref[...], preferred_element_type=F32)
    up = jnp.dot(h, wu_ref[...], preferred_element_type=F32)
    act = (gate * jax.nn.sigmoid(gate) * up).astype(BF16)
    o_ref[...] += jnp.dot(act, wd_ref[...], preferred_element_type=F32)

    if final:
        @pl.when(c == ff_split - 1)
        def _():
            o_ref[...] = _rms(o_ref[...], gf_ref[...])


def _out_ffn(x, ys, p, l, tm, final, ff_split):
    rows = x.shape[0]
    fc = D_FF // ff_split
    row_blk = lambda wd_: pl.BlockSpec((tm, wd_), lambda i, c: (i, 0))
    return pl.pallas_call(
        functools.partial(_out_ffn_kernel, final=final, ff_split=ff_split),
        grid=(rows // tm, ff_split),
        in_specs=[row_blk(D_MODEL)] + [row_blk(GROUP_W)] * 4
        + [_layer((D_MODEL, D_MODEL), l), _layer((1, D_MODEL), l),
           pl.BlockSpec((None, D_MODEL, fc), lambda i, c: (l, 0, c)),
           pl.BlockSpec((None, D_MODEL, fc), lambda i, c: (l, 0, ff_split + c)),
           pl.BlockSpec((None, fc, D_MODEL), lambda i, c: (l, c, 0)),
           _full((1, D_MODEL))],
        out_specs=row_blk(D_MODEL),
        out_shape=jax.ShapeDtypeStruct((rows, D_MODEL), F32),
        scratch_shapes=[pltpu.VMEM((tm, D_MODEL), BF16)],
        compiler_params=_cparams(("parallel", "arbitrary")),
    )(x, *ys, p["w_out"], p["n2"], p["w_gu"], p["w_gu"], p["w_down"], p["gf"])


def _pad_lora(w, start):
    return jnp.pad(w.astype(BF16), ((0, 0), (start, LORA_PAD - start - w.shape[1]), (0, 0)))


def kernel(x_prompt, x_sample, state_wkv, state_shift, state_ssm_re, state_ssm_im, state_conv,
           norm1_g, w_in, gm_ln_g, gm_ln_b, gm_ws, gm_bs,
           ssm_a_re, ssm_a_im, ssm_log_dt, ssm_b_re, ssm_b_im, ssm_c_re, ssm_c_im, ssm_d,
           ssm_glu_w, ssm_glu_b, conv_w, conv_b,
           tm_mu, tm_w0, tm_w2, tm_a0, tm_a2, tm_g2, tm_k_k, tm_k_a, tm_r_k, tm_ln_g, tm_ln_b,
           w_out, norm2_g, ffn_w_gu, ffn_w_down, norm_f_g):
    depth = w_in.shape[0]
    bp, seq, _ = x_prompt.shape
    bs = x_sample.shape[0]
    assert x_sample.shape[1] == 1 and seq % GM_CHUNK == 0 and seq % WKV_CHUNK == 0
    head_d = GROUP_W // GM_HEADS
    rowv = lambda p: p.reshape(depth, 1, -1)

    lam, bb, cc = _s5_prep(ssm_a_re, ssm_a_im, ssm_log_dt, ssm_b_re, ssm_b_im, ssm_c_re, ssm_c_im)
    p = {
        "n1": rowv(norm1_g), "n2": rowv(norm2_g), "w_in": w_in.astype(BF16),
        "lng": rowv(gm_ln_g), "lnb": rowv(gm_ln_b),
        "wcat": jnp.transpose(gm_ws, (0, 2, 1, 3)).reshape(depth, GM_CHUNK, GM_HEADS * GM_CHUNK),
        "bias": jnp.repeat(jnp.transpose(gm_bs, (0, 2, 1)), head_d, axis=2),
        "w00": rowv(jnp.repeat(gm_ws[:, :, 0, 0], head_d, axis=1)),
        "b0": rowv(jnp.repeat(gm_bs[:, :, 0], head_d, axis=1)),
        "lam": lam, "bb": bb, "cc": cc, "ssm_d": rowv(ssm_d),
        "glu_w": ssm_glu_w.astype(BF16), "glu_b": rowv(ssm_glu_b),
        "cw": conv_w, "cb": rowv(conv_b),
        "mu": rowv(tm_mu), "w0": rowv(tm_w0), "w2": _pad_lora(tm_w2, 0),
        "a0": rowv(tm_a0), "a2": _pad_lora(tm_a2, 32), "g2": _pad_lora(tm_g2, 64),
        "k_k": rowv(tm_k_k), "k_a": rowv(tm_k_a), "r_k": rowv(tm_r_k),
        "ln_g": rowv(tm_ln_g), "ln_b": rowv(tm_ln_b),
        "w_out": w_out.astype(BF16), "w_gu": ffn_w_gu.astype(BF16), "w_down": ffn_w_down.astype(BF16),
        "gf": norm_f_g.reshape(1, D_MODEL),
    }

    xp = x_prompt.reshape(bp * seq, D_MODEL)
    xs = x_sample.reshape(bs, D_MODEL)
    wkv_s_buf = jnp.transpose(state_wkv, (0, 1, 3, 2, 4)).reshape(depth, bs, WKV_N, GROUP_W)
    ssm_s_in = jnp.concatenate([state_ssm_re.reshape(depth, bs, SSM_S),
                                state_ssm_im.reshape(depth, bs, SSM_S)], axis=-1)
    conv_s_in = state_conv.reshape(depth, bs, 2 * GROUP_W)

    outs = {k: [] for k in ("wkv_p", "sh_p", "sh_s", "ssm_p", "ssm_s", "cv_p", "cv_s", "chv")}
    tm_p = ROW_TILE if seq % ROW_TILE == 0 else GM_CHUNK
    bt_s = SAMPLE_TILE if bs % SAMPLE_TILE == 0 else bs
    for l in range(depth):
        final = l == depth - 1

        ya, zb, yc, zd, tail = _inproj_mix(xp, p, l, bp, seq, tm_p)
        yb, ssm_fin = _s5(zb.reshape(bp, seq, GROUP_W), p, l, bp, seq)
        yd, wkv_fin = _rwkv(zd, p, l, bp, seq)
        flat = lambda y: y.reshape(bp * seq, GROUP_W)
        xp = _out_ffn(xp, (ya, flat(yb), yc, flat(yd)), p, l, tm_p, final, FF_SPLIT_PROMPT)
        outs["wkv_p"].append(wkv_fin)
        outs["sh_p"].append(zd.reshape(bp, seq, D_TM)[:, -1])
        outs["ssm_p"].append(ssm_fin)
        outs["cv_p"].append(tail[:, 6:8])

        za, zb, zc, zd = _inproj(xs, p["n1"], p["w_in"], l)
        ya, yb, yc, yd, vn, wkv_s_buf, ssm_new, conv_new = _sample_mix(
            za, zb, zc, zd, state_shift, wkv_s_buf, ssm_s_in, conv_s_in, p, l, bt_s)
        xs = _out_ffn(xs, (ya, yb, yc, yd), p, l, bs, final, FF_SPLIT_SAMPLE)
        outs["sh_s"].append(zd)
        outs["ssm_s"].append(ssm_new)
        outs["cv_s"].append(conv_new.reshape(bs, 2, GROUP_W))
        outs["chv"].append(vn.reshape(bs, 1, GROUP_W))

    def wkv_blocks(s_bd):
        s5d = s_bd.reshape(depth, -1, WKV_HEADS, WKV_N, WKV_HEADS, WKV_N)
        return jnp.stack([s5d[:, :, h, :, h, :] for h in range(WKV_HEADS)], axis=2)

    wkv_p = wkv_blocks(jnp.stack(outs["wkv_p"]))
    wkv_s = jnp.transpose(wkv_s_buf.reshape(depth, bs, WKV_N, WKV_HEADS, WKV_N), (0, 1, 3, 2, 4))
    ssm_p = jnp.stack(outs["ssm_p"])
    ssm_s = jnp.stack(outs["ssm_s"])
    split = lambda s, i: s[..., i * SSM_S:(i + 1) * SSM_S].reshape(depth, -1, SSM_GROUPS, SSM_P)
    return (xp.reshape(bp, seq, D_MODEL), xs.reshape(bs, 1, D_MODEL),
            wkv_p, wkv_s,
            jnp.stack(outs["sh_p"]), jnp.stack(outs["sh_s"]),
            split(ssm_p, 0), split(ssm_s, 0), split(ssm_p, 1), split(ssm_s, 1),
            jnp.stack(outs["cv_p"]), jnp.stack(outs["cv_s"]),
            jnp.stack(outs["chv"]))
```

```python
import functools

import jax
import jax.numpy as jnp
from jax import lax
from jax.experimental import pallas as pl
from jax.experimental.pallas import tpu as pltpu

F32 = jnp.float32
BF16 = jnp.bfloat16

D_MODEL = 1024
GROUP_W = 256
GM_CHUNK = 128
GM_HEADS = 4
SSM_CH = 16
SSM_GROUPS = 16
SSM_P = 64
SSM_S = SSM_GROUPS * SSM_P
WKV_N = 64
WKV_HEADS = 4
LORA_PAD = 128
D_TM = 3 * GROUP_W + LORA_PAD
IN_COLS = 6 * GROUP_W + D_TM
D_FF = 2816
FF_SPLIT_PROMPT = 2
FF_SPLIT_SAMPLE = 11
NORM_EPS = 1e-6
GM_LN_EPS = 1e-5
WKV_LN_EPS = 64e-5

WKV_CHUNK = 64
S5_TCHUNK = 128
ROW_TILE = 512
SAMPLE_TILE = 32
VMEM_LIMIT = 56 * 1024 * 1024


def _cparams(sem):
    return pltpu.CompilerParams(dimension_semantics=sem, vmem_limit_bytes=VMEM_LIMIT)


def _full(shape):
    n = len(shape)
    return pl.BlockSpec(shape, lambda *_: (0,) * n)


def _layer(shape, l):
    n = len(shape)
    return pl.BlockSpec((None,) + tuple(shape), lambda *_: (l,) + (0,) * n)


def _dot(a, b):
    return jnp.dot(a.astype(BF16), b.astype(BF16), preferred_element_type=F32)


def _dot_nt(a, b):
    return lax.dot_general(a.astype(BF16), b.astype(BF16), (((1,), (1,)), ((), ())),
                           preferred_element_type=F32)


def _dot_tn(a, b):
    return lax.dot_general(a.astype(BF16), b.astype(BF16), (((0,), (0,)), ((), ())),
                           preferred_element_type=F32)


def _dot_split(x, ones_bf16):
    hi = x.astype(BF16)
    lo = (x - hi.astype(F32)).astype(BF16)
    return (jnp.dot(hi, ones_bf16, preferred_element_type=F32)
            + jnp.dot(lo, ones_bf16, preferred_element_type=F32))


def _rms(x, g):
    return x * lax.rsqrt(jnp.mean(x * x, axis=-1, keepdims=True) + NORM_EPS) * g


def _softplus(y):
    return jnp.maximum(y, 0.0) + jnp.log1p(jnp.exp(-jnp.abs(y)))


def _gm_norm(zav, ln_g, ln_b):
    vf = jax.nn.gelu(zav)
    mu = jnp.mean(vf, axis=-1, keepdims=True)
    var = jnp.mean(jnp.square(vf - mu), axis=-1, keepdims=True)
    return (vf - mu) * lax.rsqrt(var + GM_LN_EPS) * ln_g + ln_b


def _inproj_kernel(x_ref, g_ref, w_ref, za_ref, zb_ref, zc_ref, zd_ref):
    h = _rms(x_ref[...], g_ref[...])
    z = jnp.dot(h.astype(BF16), w_ref[...], preferred_element_type=F32)
    za_ref[...] = z[:, 0:2 * GROUP_W]
    zb_ref[...] = z[:, 2 * GROUP_W:3 * GROUP_W]
    zc_ref[...] = z[:, 3 * GROUP_W:6 * GROUP_W]
    zd_ref[...] = z[:, 6 * GROUP_W:]


def _inproj(x, g, w, l):
    rows = x.shape[0]
    widths = (2 * GROUP_W, GROUP_W, 3 * GROUP_W, D_TM)
    return pl.pallas_call(
        _inproj_kernel,
        grid=(1,),
        in_specs=[_full((rows, D_MODEL)), _layer((1, D_MODEL), l), _layer((D_MODEL, IN_COLS), l)],
        out_specs=[_full((rows, wd)) for wd in widths],
        out_shape=[jax.ShapeDtypeStruct((rows, wd), F32) for wd in widths],
        compiler_params=_cparams(("arbitrary",)),
    )(x, g, w)


def _inproj_mix_kernel(x_ref, g_ref, w_ref, lng_ref, lnb_ref, wcat_ref, bias_ref, cw_ref, cb_ref,
                       ya_ref, zb_ref, yc_ref, zd_ref, tail_ref, prev_ref, *, tile):
    @pl.when(pl.program_id(1) == 0)
    def _():
        prev_ref[...] = jnp.zeros_like(prev_ref)

    h = _rms(x_ref[...], g_ref[...])
    z = jnp.dot(h.astype(BF16), w_ref[...], preferred_element_type=F32)
    zb_ref[...] = z[:, 2 * GROUP_W:3 * GROUP_W]
    zd_ref[...] = z[:, 6 * GROUP_W:]

    u = jax.nn.gelu(z[:, :GROUP_W])
    vn = _gm_norm(z[:, GROUP_W:2 * GROUP_W], lng_ref[...], lnb_ref[...])
    kc = GM_HEADS * GM_CHUNK
    t_i = lax.broadcasted_iota(jnp.int32, (GM_CHUNK, kc), 0)
    s_i = lax.broadcasted_iota(jnp.int32, (GM_CHUNK, kc), 1) % GM_CHUNK
    wm = jnp.where(s_i <= t_i, wcat_ref[...], 0.0).astype(BF16)
    r_h = lax.broadcasted_iota(jnp.int32, (kc, GROUP_W), 0) // GM_CHUNK
    c_h = lax.broadcasted_iota(jnp.int32, (kc, GROUP_W), 1) // (GROUP_W // GM_HEADS)
    head_mask = r_h == c_h
    for c in range(tile // GM_CHUNK):
        rows = slice(c * GM_CHUNK, (c + 1) * GM_CHUNK)
        vc = vn[rows].astype(BF16)
        rhs = jnp.where(head_mask, jnp.concatenate([vc] * GM_HEADS, axis=0), jnp.zeros((), BF16))
        s = jnp.dot(wm, rhs, preferred_element_type=F32) + bias_ref[...]
        ya_ref[rows, :] = u[rows] * s

    zz = z[:, 5 * GROUP_W:6 * GROUP_W] * z[:, 3 * GROUP_W:4 * GROUP_W]
    row = lax.broadcasted_iota(jnp.int32, zz.shape, 0)
    prev = prev_ref[...]
    z1 = jnp.where(row == 0, prev[7:8], pltpu.roll(zz, 1, 0))
    z2 = jnp.where(row == 0, prev[6:7], jnp.where(row == 1, prev[7:8], pltpu.roll(zz, 2, 0)))
    cw = cw_ref[...]
    y = cb_ref[...] + cw[0:1] * z2 + cw[1:2] * z1 + cw[2:3] * zz
    yc_ref[...] = z[:, 4 * GROUP_W:5 * GROUP_W] * y
    prev_ref[...] = zz[tile - 8:]
    tail_ref[...] = zz[tile - 8:]


def _inproj_mix(x, p, l, batch, seq, tile):
    nt = seq // tile
    rows = batch * seq
    row_blk = lambda wd: pl.BlockSpec((tile, wd), lambda b, j: (b * nt + j, 0))
    widths = (GROUP_W, GROUP_W, GROUP_W, D_TM)
    return pl.pallas_call(
        functools.partial(_inproj_mix_kernel, tile=tile),
        grid=(batch, nt),
        in_specs=[row_blk(D_MODEL), _layer((1, D_MODEL), l), _layer((D_MODEL, IN_COLS), l),
                  _layer((1, GROUP_W), l), _layer((1, GROUP_W), l),
                  _layer((GM_CHUNK, GM_HEADS * GM_CHUNK), l), _layer((GM_CHUNK, GROUP_W), l),
                  _layer((3, GROUP_W), l), _layer((1, GROUP_W), l)],
        out_specs=[row_blk(wd) for wd in widths]
        + [pl.BlockSpec((None, 8, GROUP_W), lambda b, j: (b, 0, 0))],
        out_shape=[jax.ShapeDtypeStruct((rows, wd), F32) for wd in widths]
        + [jax.ShapeDtypeStruct((batch, 8, GROUP_W), F32)],
        scratch_shapes=[pltpu.VMEM((8, GROUP_W), F32)],
        compiler_params=_cparams(("parallel", "arbitrary")),
    )(x, p["n1"], p["w_in"], p["lng"], p["lnb"], p["wcat"], p["bias"], p["cw"], p["cb"])


def _s5_prep_kernel(are_ref, aim_ref, ldt_ref, bre_ref, bim_ref, cre_ref, cim_ref,
                    lam_ref, bb_ref, cc_ref):
    lam_re = jnp.minimum(are_ref[...], -1e-4)
    lam_im = aim_ref[...]
    dt = jnp.exp(ldt_ref[...])
    mag = jnp.exp(lam_re * dt)
    lb_re = mag * jnp.cos(lam_im * dt)
    lb_im = mag * jnp.sin(lam_im * dt)
    den = lam_re * lam_re + lam_im * lam_im
    f_re = ((lb_re - 1.0) * lam_re + lb_im * lam_im) / den
    f_im = (lb_im * lam_re - (lb_re - 1.0) * lam_im) / den
    lam_ref[0:1, :] = lb_re
    lam_ref[1:2, :] = lb_im
    br, bi = bre_ref[...], bim_ref[...]
    grp_r = lax.broadcasted_iota(jnp.int32, (GROUP_W, SSM_S), 0) // SSM_CH
    grp_c = lax.broadcasted_iota(jnp.int32, (GROUP_W, SSM_S), 1) // SSM_P
    m = grp_r == grp_c
    bb_ref[:, :SSM_S] = jnp.where(m, f_re * br - f_im * bi, 0.0).astype(BF16)
    bb_ref[:, SSM_S:] = jnp.where(m, f_re * bi + f_im * br, 0.0).astype(BF16)
    grp_r2 = lax.broadcasted_iota(jnp.int32, (SSM_S, GROUP_W), 0) // SSM_P
    grp_c2 = lax.broadcasted_iota(jnp.int32, (SSM_S, GROUP_W), 1) // SSM_CH
    m2 = grp_r2 == grp_c2
    cc_ref[:SSM_S, :] = jnp.where(m2, cre_ref[...], 0.0).astype(BF16)
    cc_ref[SSM_S:, :] = jnp.where(m2, -cim_ref[...], 0.0).astype(BF16)


def _s5_prep(a_re, a_im, log_dt, b_re, b_im, c_re, c_im):
    depth = a_re.shape[0]
    flat = lambda p: p.reshape(depth, 1, SSM_S)
    b_exp = lambda b: jnp.tile(jnp.transpose(b, (0, 3, 1, 2)).reshape(depth, SSM_CH, SSM_S),
                               (1, SSM_GROUPS, 1))
    c_exp = lambda c: jnp.tile(jnp.transpose(c, (0, 1, 3, 2)).reshape(depth, SSM_S, SSM_CH),
                               (1, 1, SSM_GROUPS))
    lyr = lambda shape: pl.BlockSpec((None,) + shape, lambda l: (l, 0, 0))
    return pl.pallas_call(
        _s5_prep_kernel,
        grid=(depth,),
        in_specs=[lyr((1, SSM_S))] * 3 + [lyr((GROUP_W, SSM_S))] * 2 + [lyr((SSM_S, GROUP_W))] * 2,
        out_specs=[lyr((2, SSM_S)), lyr((GROUP_W, 2 * SSM_S)), lyr((2 * SSM_S, GROUP_W))],
        out_shape=[jax.ShapeDtypeStruct((depth, 2, SSM_S), F32),
                   jax.ShapeDtypeStruct((depth, GROUP_W, 2 * SSM_S), BF16),
                   jax.ShapeDtypeStruct((depth, 2 * SSM_S, GROUP_W), BF16)],
        compiler_params=_cparams(("arbitrary",)),
    )(flat(a_re), flat(a_im), flat(log_dt), b_exp(b_re), b_exp(b_im), c_exp(c_re), c_exp(c_im))


def _s5_output(st, u, cc, d, glu_w, glu_b):
    y = jnp.dot(st.astype(BF16), cc, preferred_element_type=F32) + d * u
    y = jax.nn.gelu(y)
    return y * jax.nn.sigmoid(jnp.dot(y.astype(BF16), glu_w, preferred_element_type=F32) + glu_b)


_S5_PARAMS = ("lam", "bb", "cc", "ssm_d", "glu_w", "glu_b")
_S5_SHAPES = ((2, SSM_S), (GROUP_W, 2 * SSM_S), (2 * SSM_S, GROUP_W), (1, GROUP_W),
              (GROUP_W, GROUP_W), (1, GROUP_W))


S5_SLABS = 2 * SSM_S // 128
S5_PITCH = S5_TCHUNK + 8


def _s5_kernel(u_ref, lam_ref, bb_ref, cc_ref, d_ref, gw_ref, gb_ref, y_ref, fin_ref,
               bu_ref, st_ref, *, batch, tsteps):
    half = S5_SLABS // 2

    @pl.when(pl.program_id(0) == 0)
    def _():
        st_ref[...] = jnp.zeros_like(st_ref)

    u = u_ref[...].reshape(batch * tsteps, GROUP_W)
    bu = jnp.dot(u.astype(BF16), bb_ref[...], preferred_element_type=F32)
    for b in range(batch):
        for s in range(S5_SLABS):
            bu_ref[s, b * S5_PITCH:b * S5_PITCH + tsteps, :] = (
                bu[b * tsteps:(b + 1) * tsteps, s * 128:(s + 1) * 128])
    lam = lam_ref[...]
    lr = [jnp.broadcast_to(lam[0:1, s * 128:(s + 1) * 128], (batch, 128)) for s in range(half)]
    li = [jnp.broadcast_to(lam[1:2, s * 128:(s + 1) * 128], (batch, 128)) for s in range(half)]

    def step(t, carry):
        rows = pl.ds(t, batch, stride=S5_PITCH)
        new = [None] * S5_SLABS
        for s in range(half):
            s_re, s_im = carry[s], carry[half + s]
            new[s] = lr[s] * s_re - li[s] * s_im + bu_ref[s, rows, :]
            new[half + s] = lr[s] * s_im + li[s] * s_re + bu_ref[half + s, rows, :]
            bu_ref[s, rows, :] = new[s]
            bu_ref[half + s, rows, :] = new[half + s]
        return tuple(new)

    st0 = st_ref[...]
    fin = lax.fori_loop(0, tsteps, step,
                        tuple(st0[:, s * 128:(s + 1) * 128] for s in range(S5_SLABS)), unroll=4)
    for s in range(S5_SLABS):
        st_ref[:, s * 128:(s + 1) * 128] = fin[s]
        fin_ref[:, s * 128:(s + 1) * 128] = fin[s]
    st = jnp.concatenate(
        [jnp.concatenate([bu_ref[s, b * S5_PITCH:b * S5_PITCH + tsteps, :] for s in range(S5_SLABS)],
                         axis=1) for b in range(batch)], axis=0)
    y = _s5_output(st, u, cc_ref[...], d_ref[...], gw_ref[...], gb_ref[...])
    y_ref[...] = y.reshape(batch, tsteps, GROUP_W)


def _s5(u, p, l, batch, seq):
    tsteps = min(S5_TCHUNK, seq)
    blk = pl.BlockSpec((batch, tsteps, GROUP_W), lambda i: (0, i, 0))
    return pl.pallas_call(
        functools.partial(_s5_kernel, batch=batch, tsteps=tsteps),
        grid=(seq // tsteps,),
        in_specs=[blk] + [_layer(s, l) for s in _S5_SHAPES],
        out_specs=[blk, _full((batch, 2 * SSM_S))],
        out_shape=[jax.ShapeDtypeStruct((batch, seq, GROUP_W), F32),
                   jax.ShapeDtypeStruct((batch, 2 * SSM_S), F32)],
        scratch_shapes=[pltpu.VMEM((S5_SLABS, batch * S5_PITCH, 128), F32),
                        pltpu.VMEM((batch, 2 * SSM_S), F32)],
        compiler_params=_cparams(("arbitrary",)),
    )(u, *[p[n] for n in _S5_PARAMS])


def _wkv_inputs(zd, zprev, p, bd_ones):
    zs = zd + p["mu"] * (zprev - zd)
    r = zs[:, 0:GROUP_W]
    k = zs[:, GROUP_W:2 * GROUP_W]
    v = zs[:, 2 * GROUP_W:3 * GROUP_W]
    lora = zs[:, 3 * GROUP_W:]
    w = -_softplus(-(p["w0"] + _dot(jnp.tanh(lora), p["w2"]))) - 0.5
    logd = -jnp.exp(w)
    a = jax.nn.sigmoid(p["a0"] + _dot(lora, p["a2"]))
    g = _dot(jax.nn.sigmoid(lora), p["g2"])
    kk = k * p["k_k"]
    nrm = jnp.sqrt(_dot_split(kk * kk, bd_ones))
    kk = kk / jnp.maximum(nrm, 1e-12)
    k2 = k * (1.0 + (a - 1.0) * p["k_a"])
    return r, logd, k2, v, kk, a, g


def _wkv_output(o, r, k2, v, g, p, bd_ones):
    inv_n = 1.0 / WKV_N
    m = _dot_split(o, bd_ones) * inv_n
    var = _dot_split(jnp.square(o - m), bd_ones) * inv_n
    on = (o - m) * lax.rsqrt(var + WKV_LN_EPS) * p["ln_g"] + p["ln_b"]
    bonus = _dot_split(r * k2 * p["r_k"], bd_ones) * v
    return (on + bonus) * g


_WKV_PARAMS = ("mu", "w0", "w2", "a0", "a2", "g2", "k_k", "k_a", "r_k", "ln_g", "ln_b")
_WKV_PARAM_SHAPES = {"mu": (1, D_TM), "w2": (LORA_PAD, GROUP_W), "a2": (LORA_PAD, GROUP_W),
                     "g2": (LORA_PAD, GROUP_W)}


def _wkv_param_specs(l):
    return [_layer(_WKV_PARAM_SHAPES.get(n, (1, GROUP_W)), l) for n in _WKV_PARAMS]


def _bd_mask(n):
    hr = lax.broadcasted_iota(jnp.int32, (n, n), 0) // (n // WKV_HEADS)
    hc = lax.broadcasted_iota(jnp.int32, (n, n), 1) // (n // WKV_HEADS)
    return hr == hc


def _expand(xp, lo_bf16):
    xb = xp.astype(BF16)
    zero = jnp.zeros_like(lo_bf16)
    hi_bf16 = 1 - lo_bf16
    t0, t1 = xb[:, :128], xb[:, 128:]
    return jnp.concatenate([jnp.concatenate([t0 * lo_bf16, zero], axis=1),
                            jnp.concatenate([t0 * hi_bf16, zero], axis=1),
                            jnp.concatenate([zero, t1 * lo_bf16], axis=1),
                            jnp.concatenate([zero, t1 * hi_bf16], axis=1)], axis=0)


def _interleave(*stage_gens):
    results = [None] * len(stage_gens)
    live = list(range(len(stage_gens)))
    while live:
        for i in list(live):
            try:
                next(stage_gens[i])
            except StopIteration as stop:
                results[i] = stop.value
                live.remove(i)
    return results


def _wkv_chunk_stages(r, logd, k2, v, kk, a, states, tri_ones, bd256):
    c = WKV_CHUNK
    n = len(states)
    rows = [slice(i * c, (i + 1) * c) for i in range(n)]
    t_i = lax.broadcasted_iota(jnp.int32, (c, GROUP_W), 0)
    s_i = lax.broadcasted_iota(jnp.int32, (c, GROUP_W), 1) % c
    strict = s_i < t_i
    incl = s_i <= t_i
    lo_bf16 = jnp.where(lax.broadcasted_iota(jnp.int32, (c, 128), 1) < WKV_N, 1.0, 0.0).astype(BF16)
    ex = lambda xp: _expand(xp, lo_bf16)

    cum = [jnp.dot(tri_ones, logd[rw], preferred_element_type=F32, precision=lax.Precision.HIGHEST)
           for rw in rows]
    yield
    g_last = [cm[c - 1:c] for cm in cum]
    a_t, r_t, b_t, k_t, b_h, k_h, v_c = [], [], [], [], [], [], []
    for i, rw in enumerate(rows):
        e_neg = jnp.exp(-cum[i])
        e_end = jnp.exp(g_last[i] - cum[i])
        bvec = kk[rw] * a[rw]
        a_t.append(-kk[rw] * jnp.exp(cum[i] - logd[rw]))
        r_t.append(r[rw] * jnp.exp(cum[i]))
        b_t.append(bvec * e_neg)
        k_t.append(k2[rw] * e_neg)
        b_h.append(bvec * e_end)
        k_h.append(k2[rw] * e_end)
        v_c.append(v[rw])
    yield

    ar = [jnp.concatenate([a_t[i], r_t[i]], axis=0) for i in range(n)]
    p_b = [_dot_nt(ar[i], ex(b_t[i])) for i in range(n)]
    yield
    p_k = [_dot_nt(ar[i], ex(k_t[i])) for i in range(n)]
    yield
    l_p = [jnp.where(strict, p[:c], 0.0) for p in p_b]
    aak = [jnp.where(strict, p[:c], 0.0) for p in p_k]
    rb = [jnp.where(incl, p[c:], 0.0) for p in p_b]
    rk = [jnp.where(incl, p[c:], 0.0) for p in p_k]

    def off_diag(m):
        return (t_i // (2 * m) == s_i // (2 * m)) & (t_i % (2 * m) >= m) & (s_i % (2 * m) < m)

    eye_p = jnp.where(s_i == t_i, 1.0, 0.0)
    t_p = [eye_p + jnp.where(off_diag(1), l_p[i], 0.0) for i in range(n)]
    m = 2
    while m < c:
        off = off_diag(m)
        x = [_dot(t_p[i], ex(jnp.where(off, l_p[i], 0.0))) for i in range(n)]
        yield
        t_p = [t_p[i] + _dot(x[i], ex(t_p[i])) for i in range(n)]
        yield
        m *= 2

    v_bd = [ex(v_c[i]) for i in range(n)]
    a2 = [_dot(t_p[i], ex(a_t[i])) for i in range(n)]
    av = [_dot(aak[i], v_bd[i]) for i in range(n)]
    yield
    w0 = [_dot(t_p[i], ex(av[i])) for i in range(n)]
    from_s = [_dot_nt(jnp.concatenate([a2[i], r_t[i]], axis=0), states[i]) for i in range(n)]
    yield
    w = [from_s[i][:c] + w0[i] for i in range(n)]
    o = [from_s[i][c:] + _dot(jnp.concatenate([rb[i], rk[i]], axis=1),
                              jnp.concatenate([ex(w[i]), v_bd[i]], axis=0)) for i in range(n)]
    yield
    upd = [_dot_tn(jnp.concatenate([w[i], v_c[i]], axis=0), jnp.concatenate([b_h[i], k_h[i]], axis=0))
           for i in range(n)]
    s_new = [states[i] * jnp.exp(g_last[i]) + jnp.where(bd256, upd[i], 0.0) for i in range(n)]
    return jnp.concatenate(o, axis=0), s_new


FF_CHUNK = 256


def _ffn_stages(x, ys, wo_ref, n2, wgu_ref, wd_ref, acc_ref):
    for i, y in enumerate(ys):
        x = x + jnp.dot(y.astype(BF16), wo_ref[i * GROUP_W:(i + 1) * GROUP_W, :],
                        preferred_element_type=F32)
    acc_ref[...] = x
    yield
    h = _rms(x, n2).astype(BF16)
    for c in range(D_FF // FF_CHUNK):
        gate = jnp.dot(h, wgu_ref[:, c * FF_CHUNK:(c + 1) * FF_CHUNK], preferred_element_type=F32)
        up = jnp.dot(h, wgu_ref[:, D_FF + c * FF_CHUNK:D_FF + (c + 1) * FF_CHUNK],
                     preferred_element_type=F32)
        act = (gate * jax.nn.sigmoid(gate) * up).astype(BF16)
        acc_ref[...] += jnp.dot(act, wd_ref[c * FF_CHUNK:(c + 1) * FF_CHUNK, :], preferred_element_type=F32)
        yield


def _rwkv_ffn_kernel(zd_ref, x_ref, ya_ref, yb_ref, yc_ref, wo_ref, n2_ref, gf_ref, wgu_hbm, wd_hbm,
                     *rest, nb, nt, l, final):
    prm = {n: ref[...] for n, ref in zip(_WKV_PARAMS, rest)}
    o_ref, sfin_ref, prev_ref, s_ref, yd_ref, acc_ref, wgu_ref, wd_ref, sem = rest[len(_WKV_PARAMS):]
    c = WKV_CHUNK
    j = pl.program_id(0)

    def weight_copies():
        return (pltpu.make_async_copy(wgu_hbm.at[l], wgu_ref, sem.at[0]),
                pltpu.make_async_copy(wd_hbm.at[l], wd_ref, sem.at[1]))

    def rwkv_stages():
        bd256 = _bd_mask(GROUP_W)
        bd_ones = jnp.where(bd256, 1.0, 0.0).astype(BF16)
        tri_ones = jnp.where(lax.broadcasted_iota(jnp.int32, (c, c), 1)
                             <= lax.broadcasted_iota(jnp.int32, (c, c), 0), 1.0, 0.0)
        zd3 = zd_ref[...]
        zd = zd3.reshape(nb * c, D_TM)
        first = lax.broadcasted_iota(jnp.int32, (nb, c, D_TM), 1) == 0
        carried = jnp.broadcast_to(prev_ref[:, 7:8, :], (nb, c, D_TM))
        zprev = jnp.where(first, carried, pltpu.roll(zd, 1, 0).reshape(nb, c, D_TM)).reshape(nb * c, D_TM)
        r, logd, k2, v, kk, a, g = _wkv_inputs(zd, zprev, prm, bd_ones)
        yield
        o, s_new = yield from _wkv_chunk_stages(r, logd, k2, v, kk, a, [s_ref[b] for b in range(nb)],
                                                tri_ones, bd256)
        yield
        yd_ref[...] = _wkv_output(o, r, k2, v, g, prm, bd_ones)
        for b in range(nb):
            s_ref[b] = s_new[b]
            sfin_ref[b] = s_new[b]
        prev_ref[...] = zd3[:, c - 8:, :]

    def ffn_stages():
        flat = lambda ref: ref[...].reshape(nb * c, ref.shape[-1])
        ys = (flat(ya_ref), flat(yb_ref), flat(yc_ref), yd_ref[...])
        yield from _ffn_stages(flat(x_ref), ys, wo_ref, n2_ref[...], wgu_ref, wd_ref, acc_ref)
        x = acc_ref[...]
        if final:
            x = _rms(x, gf_ref[...])
        o_ref[...] = x.reshape(nb, c, D_MODEL)

    @pl.when(j == 0)
    def _():
        prev_ref[...] = jnp.zeros_like(prev_ref)
        s_ref[...] = jnp.zeros_like(s_ref)
        for cp in weight_copies():
            cp.start()
        _interleave(rwkv_stages())
        for cp in weight_copies():
            cp.wait()

    @pl.when((j > 0) & (j < nt))
    def _():
        _interleave(ffn_stages(), rwkv_stages())

    @pl.when(j == nt)
    def _():
        _interleave(ffn_stages())


def _rwkv_ffn(zd, x, ya, yb, yc, p, l, batch, seq, final):
    c = WKV_CHUNK
    nt = seq // c
    cur = lambda wd: pl.BlockSpec((batch, c, wd), lambda j: (0, jnp.minimum(j, nt - 1), 0))
    prv = lambda wd: pl.BlockSpec((batch, c, wd), lambda j: (0, jnp.maximum(j - 1, 0), 0))
    return pl.pallas_call(
        functools.partial(_rwkv_ffn_kernel, nb=batch, nt=nt, l=l, final=final),
        grid=(nt + 1,),
        in_specs=[cur(D_TM), prv(D_MODEL), prv(GROUP_W), prv(GROUP_W), prv(GROUP_W),
                  _layer((D_MODEL, D_MODEL), l), _layer((1, D_MODEL), l), _full((1, D_MODEL)),
                  pl.BlockSpec(memory_space=pl.ANY), pl.BlockSpec(memory_space=pl.ANY)]
        + _wkv_param_specs(l),
        out_specs=[prv(D_MODEL), _full((batch, GROUP_W, GROUP_W))],
        out_shape=[jax.ShapeDtypeStruct((batch, seq, D_MODEL), F32),
                   jax.ShapeDtypeStruct((batch, GROUP_W, GROUP_W), F32)],
        scratch_shapes=[pltpu.VMEM((batch, 8, D_TM), F32),
                        pltpu.VMEM((batch, GROUP_W, GROUP_W), F32),
                        pltpu.VMEM((batch * c, GROUP_W), F32),
                        pltpu.VMEM((batch * c, D_MODEL), F32),
                        pltpu.VMEM((D_MODEL, 2 * D_FF), BF16),
                        pltpu.VMEM((D_FF, D_MODEL), BF16),
                        pltpu.SemaphoreType.DMA((2,))],
        compiler_params=_cparams(("arbitrary",)),
    )(zd, x, ya, yb, yc, p["w_out"], p["n2"], p["gf"], p["w_gu"], p["w_down"],
      *[p[n] for n in _WKV_PARAMS])


def _sample_mix_kernel(za_ref, zb_ref, zc_ref, zd_ref, shift_ref, s_ref, ssm_ref, conv_ref,
                       lng_ref, lnb_ref, w00_ref, b0_ref, lam_ref, bb_ref, cc_ref, d_ref, gw_ref,
                       gb_ref, cw_ref, cb_ref, *rest):
    prm = {n: ref[...] for n, ref in zip(_WKV_PARAMS, rest)}
    ya_ref, yb_ref, yc_ref, yd_ref, vn_ref, snew_ref, ssmnew_ref, convnew_ref = rest[len(_WKV_PARAMS):]

    za = za_ref[...]
    vn = _gm_norm(za[:, GROUP_W:], lng_ref[...], lnb_ref[...])
    vn_ref[...] = vn
    ya_ref[...] = jax.nn.gelu(za[:, :GROUP_W]) * (w00_ref[...] * vn + b0_ref[...])

    u = zb_ref[...]
    bu = jnp.dot(u.astype(BF16), bb_ref[...], preferred_element_type=F32)
    lam = lam_ref[...]
    lr, li = lam[0:1], lam[1:2]
    st = ssm_ref[...]
    s_re, s_im = st[:, :SSM_S], st[:, SSM_S:]
    st_new = jnp.concatenate([lr * s_re - li * s_im + bu[:, :SSM_S],
                              lr * s_im + li * s_re + bu[:, SSM_S:]], axis=1)
    ssmnew_ref[...] = st_new
    yb_ref[...] = _s5_output(st_new, u, cc_ref[...], d_ref[...], gw_ref[...], gb_ref[...])

    zc = zc_ref[...]
    z = zc[:, 2 * GROUP_W:] * zc[:, :GROUP_W]
    cw = cw_ref[...]
    buf = conv_ref[...]
    y = cb_ref[...] + cw[0:1] * buf[:, :GROUP_W] + cw[1:2] * buf[:, GROUP_W:] + cw[2:3] * z
    yc_ref[...] = zc[:, GROUP_W:2 * GROUP_W] * y
    convnew_ref[:, :GROUP_W] = buf[:, GROUP_W:]
    convnew_ref[:, GROUP_W:] = z

    bd256 = _bd_mask(GROUP_W)
    bd_ones = jnp.where(bd256, 1.0, 0.0).astype(BF16)
    zd = zd_ref[...]
    r, logd, k2, v, kk, a, g = _wkv_inputs(zd, shift_ref[...], prm, bd_ones)
    bt = zd.shape[0]
    s = s_ref[...]
    eye4 = (lax.broadcasted_iota(jnp.int32, (WKV_N, GROUP_W), 0)
            == lax.broadcasted_iota(jnp.int32, (WKV_N, GROUP_W), 1) % WKV_N)

    def head_sum(x3):
        return _dot_split(x3.reshape(bt * WKV_N, GROUP_W), bd_ones).reshape(bt, WKV_N, GROUP_W)

    sa = head_sum(s * (-kk)[:, None, :])
    vcol = head_sum(jnp.where(eye4[None], v[:, None, :], 0.0))
    s_new = (s * jnp.exp(logd)[:, None, :] + sa * (kk * a)[:, None, :] + vcol * k2[:, None, :])
    snew_ref[...] = s_new
    o_rep = head_sum(s_new * r[:, None, :])
    o = jnp.sum(jnp.where(eye4[None], o_rep, 0.0), axis=1)
    yd_ref[...] = _wkv_output(o, r, k2, v, g, prm, bd_ones)


def _sample_mix(za, zb, zc, zd, shift, s_t, ssm, conv, p, l, bt):
    rows = za.shape[0]
    row_blk = lambda wd: pl.BlockSpec((bt, wd), lambda i: (i, 0))
    st_blk = lambda wd: pl.BlockSpec((None, bt, wd), lambda i: (l, i, 0))
    s_blk = pl.BlockSpec((None, bt, WKV_N, GROUP_W), lambda i: (l, i, 0, 0))
    vec = _layer((1, GROUP_W), l)
    s_arg = 5
    return pl.pallas_call(
        _sample_mix_kernel,
        grid=(rows // bt,),
        in_specs=[row_blk(2 * GROUP_W), row_blk(GROUP_W), row_blk(3 * GROUP_W), row_blk(D_TM),
                  st_blk(D_TM), s_blk, st_blk(2 * SSM_S), st_blk(2 * GROUP_W),
                  vec, vec, vec, vec]
        + [_layer(s, l) for s in _S5_SHAPES]
        + [_layer((3, GROUP_W), l), vec] + _wkv_param_specs(l),
        out_specs=[row_blk(GROUP_W)] * 5 + [s_blk, row_blk(2 * SSM_S), row_blk(2 * GROUP_W)],
        out_shape=[jax.ShapeDtypeStruct((rows, GROUP_W), F32)] * 5
        + [jax.ShapeDtypeStruct(s_t.shape, F32),
           jax.ShapeDtypeStruct((rows, 2 * SSM_S), F32),
           jax.ShapeDtypeStruct((rows, 2 * GROUP_W), F32)],
        input_output_aliases={s_arg: 5},
        compiler_params=_cparams(("parallel",)),
    )(za, zb, zc, zd, shift, s_t, ssm, conv, p["lng"], p["lnb"], p["w00"], p["b0"],
      *[p[n] for n in _S5_PARAMS], p["cw"], p["cb"], *[p[n] for n in _WKV_PARAMS])


def _out_ffn_kernel(x_ref, ya_ref, yb_ref, yc_ref, yd_ref, wo_ref, g2_ref, wg_ref, wu_ref, wd_ref,
                    gf_ref, o_ref, h_ref, *, final, ff_split):
    c = pl.program_id(1)

    @pl.when(c == 0)
    def _():
        x = x_ref[...]
        for i, y_ref in enumerate((ya_ref, yb_ref, yc_ref, yd_ref)):
            x = x + jnp.dot(y_ref[...].astype(BF16), wo_ref[i * GROUP_W:(i + 1) * GROUP_W, :],
                            preferred_element_type=F32)
        o_ref[...] = x
        h_ref[...] = _rms(x, g2_ref[...]).astype(BF16)

    h = h_ref[...]
    gate = jnp.dot(h, wg_ref[...], preferred_element_type=F32)
    up = jnp.dot(h, wu_ref[...], preferred_element_type=F32)
    act = (gate * jax.nn.sigmoid(gate) * up).astype(BF16)
    o_ref[...] += jnp.dot(act, wd_ref[...], preferred_element_type=F32)

    if final:
        @pl.when(c == ff_split - 1)
        def _():
            o_ref[...] = _rms(o_ref[...], gf_ref[...])


def _out_ffn(x, ys, p, l, tm, final, ff_split):
    rows = x.shape[0]
    fc = D_FF // ff_split
    row_blk = lambda wd_: pl.BlockSpec((tm, wd_), lambda i, c: (i, 0))
    return pl.pallas_call(
        functools.partial(_out_ffn_kernel, final=final, ff_split=ff_split),
        grid=(rows // tm, ff_split),
        in_specs=[row_blk(D_MODEL)] + [row_blk(GROUP_W)] * 4
        + [_layer((D_MODEL, D_MODEL), l), _layer((1, D_MODEL), l),
           pl.BlockSpec((None, D_MODEL, fc), lambda i, c: (l, 0, c)),
           pl.BlockSpec((None, D_MODEL, fc), lambda i, c: (l, 0, ff_split + c)),
           pl.BlockSpec((None, fc, D_MODEL), lambda i, c: (l, c, 0)),
           _full((1, D_MODEL))],
        out_specs=row_blk(D_MODEL),
        out_shape=jax.ShapeDtypeStruct((rows, D_MODEL), F32),
        scratch_shapes=[pltpu.VMEM((tm, D_MODEL), BF16)],
        compiler_params=_cparams(("parallel", "arbitrary")),
    )(x, *ys, p["w_out"], p["n2"], p["w_gu"], p["w_gu"], p["w_down"], p["gf"])


def _pad_lora(w, start):
    return jnp.pad(w.astype(BF16), ((0, 0), (start, LORA_PAD - start - w.shape[1]), (0, 0)))


def kernel(x_prompt, x_sample, state_wkv, state_shift, state_ssm_re, state_ssm_im, state_conv,
           norm1_g, w_in, gm_ln_g, gm_ln_b, gm_ws, gm_bs,
           ssm_a_re, ssm_a_im, ssm_log_dt, ssm_b_re, ssm_b_im, ssm_c_re, ssm_c_im, ssm_d,
           ssm_glu_w, ssm_glu_b, conv_w, conv_b,
           tm_mu, tm_w0, tm_w2, tm_a0, tm_a2, tm_g2, tm_k_k, tm_k_a, tm_r_k, tm_ln_g, tm_ln_b,
           w_out, norm2_g, ffn_w_gu, ffn_w_down, norm_f_g):
    depth = w_in.shape[0]
    bp, seq, _ = x_prompt.shape
    bs = x_sample.shape[0]
    assert x_sample.shape[1] == 1 and seq % GM_CHUNK == 0 and seq % WKV_CHUNK == 0
    head_d = GROUP_W // GM_HEADS
    rowv = lambda p: p.reshape(depth, 1, -1)

    lam, bb, cc = _s5_prep(ssm_a_re, ssm_a_im, ssm_log_dt, ssm_b_re, ssm_b_im, ssm_c_re, ssm_c_im)
    p = {
        "n1": rowv(norm1_g), "n2": rowv(norm2_g), "w_in": w_in.astype(BF16),
        "lng": rowv(gm_ln_g), "lnb": rowv(gm_ln_b),
        "wcat": jnp.transpose(gm_ws, (0, 2, 1, 3)).reshape(depth, GM_CHUNK, GM_HEADS * GM_CHUNK),
        "bias": jnp.repeat(jnp.transpose(gm_bs, (0, 2, 1)), head_d, axis=2),
        "w00": rowv(jnp.repeat(gm_ws[:, :, 0, 0], head_d, axis=1)),
        "b0": rowv(jnp.repeat(gm_bs[:, :, 0], head_d, axis=1)),
        "lam": lam, "bb": bb, "cc": cc, "ssm_d": rowv(ssm_d),
        "glu_w": ssm_glu_w.astype(BF16), "glu_b": rowv(ssm_glu_b),
        "cw": conv_w, "cb": rowv(conv_b),
        "mu": rowv(tm_mu), "w0": rowv(tm_w0), "w2": _pad_lora(tm_w2, 0),
        "a0": rowv(tm_a0), "a2": _pad_lora(tm_a2, 32), "g2": _pad_lora(tm_g2, 64),
        "k_k": rowv(tm_k_k), "k_a": rowv(tm_k_a), "r_k": rowv(tm_r_k),
        "ln_g": rowv(tm_ln_g), "ln_b": rowv(tm_ln_b),
        "w_out": w_out.astype(BF16), "w_gu": ffn_w_gu.astype(BF16), "w_down": ffn_w_down.astype(BF16),
        "gf": norm_f_g.reshape(1, D_MODEL),
    }

    xp = x_prompt.reshape(bp * seq, D_MODEL)
    xs = x_sample.reshape(bs, D_MODEL)
    wkv_s_buf = jnp.transpose(state_wkv, (0, 1, 3, 2, 4)).reshape(depth, bs, WKV_N, GROUP_W)
    ssm_s_in = jnp.concatenate([state_ssm_re.reshape(depth, bs, SSM_S),
                                state_ssm_im.reshape(depth, bs, SSM_S)], axis=-1)
    conv_s_in = state_conv.reshape(depth, bs, 2 * GROUP_W)

    outs = {k: [] for k in ("wkv_p", "sh_p", "sh_s", "ssm_p", "ssm_s", "cv_p", "cv_s", "chv")}
    tm_p = ROW_TILE if seq % ROW_TILE == 0 else GM_CHUNK
    bt_s = SAMPLE_TILE if bs % SAMPLE_TILE == 0 else bs
    for l in range(depth):
        final = l == depth - 1

        ya, zb, yc, zd, tail = _inproj_mix(xp, p, l, bp, seq, tm_p)
        per_batch = lambda y: y.reshape(bp, seq, y.shape[-1])
        yb, ssm_fin = _s5(per_batch(zb), p, l, bp, seq)
        xp3, wkv_fin = _rwkv_ffn(per_batch(zd), per_batch(xp), per_batch(ya), yb, per_batch(yc),
                                 p, l, bp, seq, final)
        xp = xp3.reshape(bp * seq, D_MODEL)
        outs["wkv_p"].append(wkv_fin)
        outs["sh_p"].append(zd.reshape(bp, seq, D_TM)[:, -1])
        outs["ssm_p"].append(ssm_fin)
        outs["cv_p"].append(tail[:, 6:8])

        za, zb, zc, zd = _inproj(xs, p["n1"], p["w_in"], l)
        ya, yb, yc, yd, vn, wkv_s_buf, ssm_new, conv_new = _sample_mix(
            za, zb, zc, zd, state_shift, wkv_s_buf, ssm_s_in, conv_s_in, p, l, bt_s)
        xs = _out_ffn(xs, (ya, yb, yc, yd), p, l, bs, final, FF_SPLIT_SAMPLE)
        outs["sh_s"].append(zd)
        outs["ssm_s"].append(ssm_new)
        outs["cv_s"].append(conv_new.reshape(bs, 2, GROUP_W))
        outs["chv"].append(vn.reshape(bs, 1, GROUP_W))

    def wkv_blocks(s_bd):
        s5d = s_bd.reshape(depth, -1, WKV_HEADS, WKV_N, WKV_HEADS, WKV_N)
        return jnp.stack([s5d[:, :, h, :, h, :] for h in range(WKV_HEADS)], axis=2)

    wkv_p = wkv_blocks(jnp.stack(outs["wkv_p"]))
    wkv_s = jnp.transpose(wkv_s_buf.reshape(depth, bs, WKV_N, WKV_HEADS, WKV_N), (0, 1, 3, 2, 4))
    ssm_p = jnp.stack(outs["ssm_p"])
    ssm_s = jnp.stack(outs["ssm_s"])
    split = lambda s, i: s[..., i * SSM_S:(i + 1) * SSM_S].reshape(depth, -1, SSM_GROUPS, SSM_P)
    return (xp.reshape(bp, seq, D_MODEL), xs.reshape(bs, 1, D_MODEL),
            wkv_p, wkv_s,
            jnp.stack(outs["sh_p"]), jnp.stack(outs["sh_s"]),
            split(ssm_p, 0), split(ssm_s, 0), split(ssm_p, 1), split(ssm_s, 1),
            jnp.stack(outs["cv_p"]), jnp.stack(outs["cv_s"]),
            jnp.stack(outs["chv"]))
```

```python
import functools

import jax
import jax.numpy as jnp
from jax import lax
from jax.experimental import pallas as pl
from jax.experimental.pallas import tpu as pltpu

F32 = jnp.float32
BF16 = jnp.bfloat16

D_MODEL = 1024
GROUP_W = 256
GM_CHUNK = 128
GM_HEADS = 4
SSM_CH = 16
SSM_GROUPS = 16
SSM_P = 64
SSM_S = SSM_GROUPS * SSM_P
WKV_N = 64
WKV_HEADS = 4
LORA_PAD = 128
D_TM = 3 * GROUP_W + LORA_PAD
IN_COLS = 6 * GROUP_W + D_TM
D_FF = 2816
FF_SPLIT_PROMPT = 2
FF_SPLIT_SAMPLE = 11
NORM_EPS = 1e-6
GM_LN_EPS = 1e-5
WKV_LN_EPS = 64e-5

WKV_CHUNK = 64
S5_TCHUNK = 128
ROW_TILE = 512
SAMPLE_TILE = 32
VMEM_LIMIT = 56 * 1024 * 1024


def _cparams(sem):
    return pltpu.CompilerParams(dimension_semantics=sem, vmem_limit_bytes=VMEM_LIMIT)


def _full(shape):
    n = len(shape)
    return pl.BlockSpec(shape, lambda *_: (0,) * n)


def _layer(shape, l):
    n = len(shape)
    return pl.BlockSpec((None,) + tuple(shape), lambda *_: (l,) + (0,) * n)


def _dot(a, b):
    return jnp.dot(a.astype(BF16), b.astype(BF16), preferred_element_type=F32)


def _dot_nt(a, b):
    return lax.dot_general(a.astype(BF16), b.astype(BF16), (((1,), (1,)), ((), ())),
                           preferred_element_type=F32)


def _dot_tn(a, b):
    return lax.dot_general(a.astype(BF16), b.astype(BF16), (((0,), (0,)), ((), ())),
                           preferred_element_type=F32)


def _dot_split(x, ones_bf16):
    hi = x.astype(BF16)
    lo = (x - hi.astype(F32)).astype(BF16)
    return (jnp.dot(hi, ones_bf16, preferred_element_type=F32)
            + jnp.dot(lo, ones_bf16, preferred_element_type=F32))


def _rms(x, g):
    return x * lax.rsqrt(jnp.mean(x * x, axis=-1, keepdims=True) + NORM_EPS) * g


def _softplus(y):
    return jnp.maximum(y, 0.0) + jnp.log1p(jnp.exp(-jnp.abs(y)))


def _gm_norm(zav, ln_g, ln_b):
    vf = jax.nn.gelu(zav)
    mu = jnp.mean(vf, axis=-1, keepdims=True)
    var = jnp.mean(jnp.square(vf - mu), axis=-1, keepdims=True)
    return (vf - mu) * lax.rsqrt(var + GM_LN_EPS) * ln_g + ln_b


def _inproj_kernel(x_ref, g_ref, w_ref, za_ref, zb_ref, zc_ref, zd_ref):
    h = _rms(x_ref[...], g_ref[...])
    z = jnp.dot(h.astype(BF16), w_ref[...], preferred_element_type=F32)
    za_ref[...] = z[:, 0:2 * GROUP_W]
    zb_ref[...] = z[:, 2 * GROUP_W:3 * GROUP_W]
    zc_ref[...] = z[:, 3 * GROUP_W:6 * GROUP_W]
    zd_ref[...] = z[:, 6 * GROUP_W:]


def _inproj(x, g, w, l):
    rows = x.shape[0]
    widths = (2 * GROUP_W, GROUP_W, 3 * GROUP_W, D_TM)
    return pl.pallas_call(
        _inproj_kernel,
        grid=(1,),
        in_specs=[_full((rows, D_MODEL)), _layer((1, D_MODEL), l), _layer((D_MODEL, IN_COLS), l)],
        out_specs=[_full((rows, wd)) for wd in widths],
        out_shape=[jax.ShapeDtypeStruct((rows, wd), F32) for wd in widths],
        compiler_params=_cparams(("arbitrary",)),
    )(x, g, w)


def _inproj_mix_kernel(x_ref, g_ref, w_ref, lng_ref, lnb_ref, wcat_ref, bias_ref, cw_ref, cb_ref,
                       ya_ref, zb_ref, yc_ref, zd_ref, tail_ref, prev_ref, *, tile):
    @pl.when(pl.program_id(1) == 0)
    def _():
        prev_ref[...] = jnp.zeros_like(prev_ref)

    h = _rms(x_ref[...], g_ref[...])
    z = jnp.dot(h.astype(BF16), w_ref[...], preferred_element_type=F32)
    zb_ref[...] = z[:, 2 * GROUP_W:3 * GROUP_W]
    zd_ref[...] = z[:, 6 * GROUP_W:]

    u = jax.nn.gelu(z[:, :GROUP_W])
    vn = _gm_norm(z[:, GROUP_W:2 * GROUP_W], lng_ref[...], lnb_ref[...])
    kc = GM_HEADS * GM_CHUNK
    t_i = lax.broadcasted_iota(jnp.int32, (GM_CHUNK, kc), 0)
    s_i = lax.broadcasted_iota(jnp.int32, (GM_CHUNK, kc), 1) % GM_CHUNK
    wm = jnp.where(s_i <= t_i, wcat_ref[...], 0.0).astype(BF16)
    r_h = lax.broadcasted_iota(jnp.int32, (kc, GROUP_W), 0) // GM_CHUNK
    c_h = lax.broadcasted_iota(jnp.int32, (kc, GROUP_W), 1) // (GROUP_W // GM_HEADS)
    head_mask = r_h == c_h
    for c in range(tile // GM_CHUNK):
        rows = slice(c * GM_CHUNK, (c + 1) * GM_CHUNK)
        vc = vn[rows].astype(BF16)
        rhs = jnp.where(head_mask, jnp.concatenate([vc] * GM_HEADS, axis=0), jnp.zeros((), BF16))
        s = jnp.dot(wm, rhs, preferred_element_type=F32) + bias_ref[...]
        ya_ref[rows, :] = u[rows] * s

    zz = z[:, 5 * GROUP_W:6 * GROUP_W] * z[:, 3 * GROUP_W:4 * GROUP_W]
    row = lax.broadcasted_iota(jnp.int32, zz.shape, 0)
    prev = prev_ref[...]
    z1 = jnp.where(row == 0, prev[7:8], pltpu.roll(zz, 1, 0))
    z2 = jnp.where(row == 0, prev[6:7], jnp.where(row == 1, prev[7:8], pltpu.roll(zz, 2, 0)))
    cw = cw_ref[...]
    y = cb_ref[...] + cw[0:1] * z2 + cw[1:2] * z1 + cw[2:3] * zz
    yc_ref[...] = z[:, 4 * GROUP_W:5 * GROUP_W] * y
    prev_ref[...] = zz[tile - 8:]
    tail_ref[...] = zz[tile - 8:]


def _inproj_mix(x, p, l, batch, seq, tile):
    nt = seq // tile
    rows = batch * seq
    row_blk = lambda wd: pl.BlockSpec((tile, wd), lambda b, j: (b * nt + j, 0))
    widths = (GROUP_W, GROUP_W, GROUP_W, D_TM)
    return pl.pallas_call(
        functools.partial(_inproj_mix_kernel, tile=tile),
        grid=(batch, nt),
        in_specs=[row_blk(D_MODEL), _layer((1, D_MODEL), l), _layer((D_MODEL, IN_COLS), l),
                  _layer((1, GROUP_W), l), _layer((1, GROUP_W), l),
                  _layer((GM_CHUNK, GM_HEADS * GM_CHUNK), l), _layer((GM_CHUNK, GROUP_W), l),
                  _layer((3, GROUP_W), l), _layer((1, GROUP_W), l)],
        out_specs=[row_blk(wd) for wd in widths]
        + [pl.BlockSpec((None, 8, GROUP_W), lambda b, j: (b, 0, 0))],
        out_shape=[jax.ShapeDtypeStruct((rows, wd), F32) for wd in widths]
        + [jax.ShapeDtypeStruct((batch, 8, GROUP_W), F32)],
        scratch_shapes=[pltpu.VMEM((8, GROUP_W), F32)],
        compiler_params=_cparams(("parallel", "arbitrary")),
    )(x, p["n1"], p["w_in"], p["lng"], p["lnb"], p["wcat"], p["bias"], p["cw"], p["cb"])


def _s5_prep_kernel(are_ref, aim_ref, ldt_ref, bre_ref, bim_ref, cre_ref, cim_ref,
                    lam_ref, bb_ref, cc_ref):
    lam_re = jnp.minimum(are_ref[...], -1e-4)
    lam_im = aim_ref[...]
    dt = jnp.exp(ldt_ref[...])
    mag = jnp.exp(lam_re * dt)
    lb_re = mag * jnp.cos(lam_im * dt)
    lb_im = mag * jnp.sin(lam_im * dt)
    den = lam_re * lam_re + lam_im * lam_im
    f_re = ((lb_re - 1.0) * lam_re + lb_im * lam_im) / den
    f_im = (lb_im * lam_re - (lb_re - 1.0) * lam_im) / den
    lam_ref[0:1, :] = lb_re
    lam_ref[1:2, :] = lb_im
    br, bi = bre_ref[...], bim_ref[...]
    grp_r = lax.broadcasted_iota(jnp.int32, (GROUP_W, SSM_S), 0) // SSM_CH
    grp_c = lax.broadcasted_iota(jnp.int32, (GROUP_W, SSM_S), 1) // SSM_P
    m = grp_r == grp_c
    bb_ref[:, :SSM_S] = jnp.where(m, f_re * br - f_im * bi, 0.0).astype(BF16)
    bb_ref[:, SSM_S:] = jnp.where(m, f_re * bi + f_im * br, 0.0).astype(BF16)
    grp_r2 = lax.broadcasted_iota(jnp.int32, (SSM_S, GROUP_W), 0) // SSM_P
    grp_c2 = lax.broadcasted_iota(jnp.int32, (SSM_S, GROUP_W), 1) // SSM_CH
    m2 = grp_r2 == grp_c2
    cc_ref[:SSM_S, :] = jnp.where(m2, cre_ref[...], 0.0).astype(BF16)
    cc_ref[SSM_S:, :] = jnp.where(m2, -cim_ref[...], 0.0).astype(BF16)


def _s5_prep(a_re, a_im, log_dt, b_re, b_im, c_re, c_im):
    depth = a_re.shape[0]
    flat = lambda p: p.reshape(depth, 1, SSM_S)
    b_exp = lambda b: jnp.tile(jnp.transpose(b, (0, 3, 1, 2)).reshape(depth, SSM_CH, SSM_S),
                               (1, SSM_GROUPS, 1))
    c_exp = lambda c: jnp.tile(jnp.transpose(c, (0, 1, 3, 2)).reshape(depth, SSM_S, SSM_CH),
                               (1, 1, SSM_GROUPS))
    lyr = lambda shape: pl.BlockSpec((None,) + shape, lambda l: (l, 0, 0))
    return pl.pallas_call(
        _s5_prep_kernel,
        grid=(depth,),
        in_specs=[lyr((1, SSM_S))] * 3 + [lyr((GROUP_W, SSM_S))] * 2 + [lyr((SSM_S, GROUP_W))] * 2,
        out_specs=[lyr((2, SSM_S)), lyr((GROUP_W, 2 * SSM_S)), lyr((2 * SSM_S, GROUP_W))],
        out_shape=[jax.ShapeDtypeStruct((depth, 2, SSM_S), F32),
                   jax.ShapeDtypeStruct((depth, GROUP_W, 2 * SSM_S), BF16),
                   jax.ShapeDtypeStruct((depth, 2 * SSM_S, GROUP_W), BF16)],
        compiler_params=_cparams(("arbitrary",)),
    )(flat(a_re), flat(a_im), flat(log_dt), b_exp(b_re), b_exp(b_im), c_exp(c_re), c_exp(c_im))


def _s5_output(st, u, cc, d, glu_w, glu_b):
    y = jnp.dot(st.astype(BF16), cc, preferred_element_type=F32) + d * u
    y = jax.nn.gelu(y)
    return y * jax.nn.sigmoid(jnp.dot(y.astype(BF16), glu_w, preferred_element_type=F32) + glu_b)


_S5_PARAMS = ("lam", "bb", "cc", "ssm_d", "glu_w", "glu_b")
_S5_SHAPES = ((2, SSM_S), (GROUP_W, 2 * SSM_S), (2 * SSM_S, GROUP_W), (1, GROUP_W),
              (GROUP_W, GROUP_W), (1, GROUP_W))


S5_SLABS = 2 * SSM_S // 128
S5_PITCH = S5_TCHUNK + 8


def _s5_kernel(u_ref, lam_ref, bb_ref, cc_ref, d_ref, gw_ref, gb_ref, y_ref, fin_ref,
               bu_ref, st_ref, *, batch, tsteps):
    half = S5_SLABS // 2

    @pl.when(pl.program_id(0) == 0)
    def _():
        st_ref[...] = jnp.zeros_like(st_ref)

    u = u_ref[...].reshape(batch * tsteps, GROUP_W)
    bu = jnp.dot(u.astype(BF16), bb_ref[...], preferred_element_type=F32)
    for b in range(batch):
        for s in range(S5_SLABS):
            bu_ref[s, b * S5_PITCH:b * S5_PITCH + tsteps, :] = (
                bu[b * tsteps:(b + 1) * tsteps, s * 128:(s + 1) * 128])
    lam = lam_ref[...]
    lr = [jnp.broadcast_to(lam[0:1, s * 128:(s + 1) * 128], (batch, 128)) for s in range(half)]
    li = [jnp.broadcast_to(lam[1:2, s * 128:(s + 1) * 128], (batch, 128)) for s in range(half)]

    def step(t, carry):
        rows = pl.ds(t, batch, stride=S5_PITCH)
        new = [None] * S5_SLABS
        for s in range(half):
            s_re, s_im = carry[s], carry[half + s]
            new[s] = lr[s] * s_re - li[s] * s_im + bu_ref[s, rows, :]
            new[half + s] = lr[s] * s_im + li[s] * s_re + bu_ref[half + s, rows, :]
            bu_ref[s, rows, :] = new[s]
            bu_ref[half + s, rows, :] = new[half + s]
        return tuple(new)

    st0 = st_ref[...]
    fin = lax.fori_loop(0, tsteps, step,
                        tuple(st0[:, s * 128:(s + 1) * 128] for s in range(S5_SLABS)), unroll=4)
    for s in range(S5_SLABS):
        st_ref[:, s * 128:(s + 1) * 128] = fin[s]
        fin_ref[:, s * 128:(s + 1) * 128] = fin[s]
    st = jnp.concatenate(
        [jnp.concatenate([bu_ref[s, b * S5_PITCH:b * S5_PITCH + tsteps, :] for s in range(S5_SLABS)],
                         axis=1) for b in range(batch)], axis=0)
    y = _s5_output(st, u, cc_ref[...], d_ref[...], gw_ref[...], gb_ref[...])
    y_ref[...] = y.reshape(batch, tsteps, GROUP_W)


def _s5(u, p, l, batch, seq):
    tsteps = min(S5_TCHUNK, seq)
    blk = pl.BlockSpec((batch, tsteps, GROUP_W), lambda i: (0, i, 0))
    return pl.pallas_call(
        functools.partial(_s5_kernel, batch=batch, tsteps=tsteps),
        grid=(seq // tsteps,),
        in_specs=[blk] + [_layer(s, l) for s in _S5_SHAPES],
        out_specs=[blk, _full((batch, 2 * SSM_S))],
        out_shape=[jax.ShapeDtypeStruct((batch, seq, GROUP_W), F32),
                   jax.ShapeDtypeStruct((batch, 2 * SSM_S), F32)],
        scratch_shapes=[pltpu.VMEM((S5_SLABS, batch * S5_PITCH, 128), F32),
                        pltpu.VMEM((batch, 2 * SSM_S), F32)],
        compiler_params=_cparams(("arbitrary",)),
    )(u, *[p[n] for n in _S5_PARAMS])


def _wkv_inputs(zd, zprev, p, bd_ones):
    zs = zd + p["mu"] * (zprev - zd)
    r = zs[:, 0:GROUP_W]
    k = zs[:, GROUP_W:2 * GROUP_W]
    v = zs[:, 2 * GROUP_W:3 * GROUP_W]
    lora = zs[:, 3 * GROUP_W:]
    w = -_softplus(-(p["w0"] + _dot(jnp.tanh(lora), p["w2"]))) - 0.5
    logd = -jnp.exp(w)
    a = jax.nn.sigmoid(p["a0"] + _dot(lora, p["a2"]))
    g = _dot(jax.nn.sigmoid(lora), p["g2"])
    kk = k * p["k_k"]
    nrm = jnp.sqrt(_dot_split(kk * kk, bd_ones))
    kk = kk / jnp.maximum(nrm, 1e-12)
    k2 = k * (1.0 + (a - 1.0) * p["k_a"])
    return r, logd, k2, v, kk, a, g


def _wkv_output(o, r, k2, v, g, p, bd_ones):
    inv_n = 1.0 / WKV_N
    m = _dot_split(o, bd_ones) * inv_n
    var = _dot_split(jnp.square(o - m), bd_ones) * inv_n
    on = (o - m) * lax.rsqrt(var + WKV_LN_EPS) * p["ln_g"] + p["ln_b"]
    bonus = _dot_split(r * k2 * p["r_k"], bd_ones) * v
    return (on + bonus) * g


_WKV_PARAMS = ("mu", "w0", "w2", "a0", "a2", "g2", "k_k", "k_a", "r_k", "ln_g", "ln_b")
_WKV_PARAM_SHAPES = {"mu": (1, D_TM), "w2": (LORA_PAD, GROUP_W), "a2": (LORA_PAD, GROUP_W),
                     "g2": (LORA_PAD, GROUP_W)}


def _wkv_param_specs(l):
    return [_layer(_WKV_PARAM_SHAPES.get(n, (1, GROUP_W)), l) for n in _WKV_PARAMS]


def _bd_mask(n):
    hr = lax.broadcasted_iota(jnp.int32, (n, n), 0) // (n // WKV_HEADS)
    hc = lax.broadcasted_iota(jnp.int32, (n, n), 1) // (n // WKV_HEADS)
    return hr == hc


def _expand(xp, lo_bf16):
    xb = xp.astype(BF16)
    zero = jnp.zeros_like(lo_bf16)
    hi_bf16 = 1 - lo_bf16
    t0, t1 = xb[:, :128], xb[:, 128:]
    return jnp.concatenate([jnp.concatenate([t0 * lo_bf16, zero], axis=1),
                            jnp.concatenate([t0 * hi_bf16, zero], axis=1),
                            jnp.concatenate([zero, t1 * lo_bf16], axis=1),
                            jnp.concatenate([zero, t1 * hi_bf16], axis=1)], axis=0)


def _interleave(*staged):
    gens = [g for g, _ in staged]
    total = [n for _, n in staged]
    done = [0] * len(gens)
    live = set(range(len(gens)))
    while live:
        i = min(live, key=lambda k: (done[k] / total[k], k))
        try:
            next(gens[i])
            done[i] += 1
        except StopIteration:
            live.remove(i)


def _wkv_chunk_stages(r, logd, k2, v, kk, a, states, bd256):
    c = WKV_CHUNK
    n = len(states)
    rows = [slice(i * c, (i + 1) * c) for i in range(n)]
    t_i = lax.broadcasted_iota(jnp.int32, (c, GROUP_W), 0)
    s_i = lax.broadcasted_iota(jnp.int32, (c, GROUP_W), 1) % c
    strict = s_i < t_i
    incl = s_i <= t_i
    lo_bf16 = jnp.where(lax.broadcasted_iota(jnp.int32, (c, 128), 1) < WKV_N, 1.0, 0.0).astype(BF16)
    ex = lambda xp: _expand(xp, lo_bf16)

    pos = lax.broadcasted_iota(jnp.int32, logd.shape, 0) % c
    cum_all = logd
    step = 1
    while step < c:
        cum_all = cum_all + jnp.where(pos >= step, pltpu.roll(cum_all, step, 0), 0.0)
        step *= 2
    cum = [cum_all[rw] for rw in rows]
    yield
    g_last = [cm[c - 1:c] for cm in cum]
    a_t, r_t, b_t, k_t, b_h, k_h, v_c = [], [], [], [], [], [], []
    for i, rw in enumerate(rows):
        e_neg = jnp.exp(-cum[i])
        e_end = jnp.exp(g_last[i] - cum[i])
        bvec = kk[rw] * a[rw]
        a_t.append(-kk[rw] * jnp.exp(cum[i] - logd[rw]))
        r_t.append(r[rw] * jnp.exp(cum[i]))
        b_t.append(bvec * e_neg)
        k_t.append(k2[rw] * e_neg)
        b_h.append(bvec * e_end)
        k_h.append(k2[rw] * e_end)
        v_c.append(v[rw])
        if i % 2 == 1:
            yield

    ar = [jnp.concatenate([a_t[i], r_t[i]], axis=0) for i in range(n)]
    p_b = [_dot_nt(ar[i], ex(b_t[i])) for i in range(n)]
    yield
    p_k = [_dot_nt(ar[i], ex(k_t[i])) for i in range(n)]
    yield
    l_p = [jnp.where(strict, p[:c], 0.0) for p in p_b]
    aak = [jnp.where(strict, p[:c], 0.0) for p in p_k]
    rb = [jnp.where(incl, p[c:], 0.0) for p in p_b]
    rk = [jnp.where(incl, p[c:], 0.0) for p in p_k]

    def off_diag(m):
        return (t_i // (2 * m) == s_i // (2 * m)) & (t_i % (2 * m) >= m) & (s_i % (2 * m) < m)

    eye_p = jnp.where(s_i == t_i, 1.0, 0.0)
    t_p = [eye_p + jnp.where(off_diag(1), l_p[i], 0.0) for i in range(n)]
    m = 2
    while m < c:
        off = off_diag(m)
        x = [_dot(t_p[i], ex(jnp.where(off, l_p[i], 0.0))) for i in range(n)]
        yield
        t_p = [t_p[i] + _dot(x[i], ex(t_p[i])) for i in range(n)]
        yield
        m *= 2

    v_bd = [ex(v_c[i]) for i in range(n)]
    a2 = [_dot(t_p[i], ex(a_t[i])) for i in range(n)]
    av = [_dot(aak[i], v_bd[i]) for i in range(n)]
    yield
    w0 = [_dot(t_p[i], ex(av[i])) for i in range(n)]
    from_s = [_dot_nt(jnp.concatenate([a2[i], r_t[i]], axis=0), states[i]) for i in range(n)]
    yield
    w = [from_s[i][:c] + w0[i] for i in range(n)]
    o = [from_s[i][c:] + _dot(jnp.concatenate([rb[i], rk[i]], axis=1),
                              jnp.concatenate([ex(w[i]), v_bd[i]], axis=0)) for i in range(n)]
    yield
    upd = [_dot_tn(jnp.concatenate([w[i], v_c[i]], axis=0), jnp.concatenate([b_h[i], k_h[i]], axis=0))
           for i in range(n)]
    s_new = [states[i] * jnp.exp(g_last[i]) + jnp.where(bd256, upd[i], 0.0) for i in range(n)]
    return jnp.concatenate(o, axis=0), s_new


FF_CHUNK = 256


def _ffn_stages(x, ys, wo_ref, n2, wgu_ref, wd_ref, acc_ref):
    for i, y in enumerate(ys):
        x = x + jnp.dot(y.astype(BF16), wo_ref[i * GROUP_W:(i + 1) * GROUP_W, :],
                        preferred_element_type=F32)
    acc_ref[...] = x
    yield
    h = _rms(x, n2).astype(BF16)
    for c in range(D_FF // FF_CHUNK):
        gate = jnp.dot(h, wgu_ref[:, c * FF_CHUNK:(c + 1) * FF_CHUNK], preferred_element_type=F32)
        yield
        up = jnp.dot(h, wgu_ref[:, D_FF + c * FF_CHUNK:D_FF + (c + 1) * FF_CHUNK],
                     preferred_element_type=F32)
        yield
        act = (gate * jax.nn.sigmoid(gate) * up).astype(BF16)
        acc_ref[...] += jnp.dot(act, wd_ref[c * FF_CHUNK:(c + 1) * FF_CHUNK, :], preferred_element_type=F32)
        yield


def _rwkv_ffn_kernel(zd_ref, x_ref, ya_ref, yb_ref, yc_ref, wo_ref, n2_ref, gf_ref, wgu_hbm, wd_hbm,
                     *rest, nb, nt, l, final):
    prm = {n: ref[...] for n, ref in zip(_WKV_PARAMS, rest)}
    o_ref, sfin_ref, prev_ref, s_ref, yd_ref, acc_ref, wgu_ref, wd_ref, sem = rest[len(_WKV_PARAMS):]
    c = WKV_CHUNK
    j = pl.program_id(0)
    n_rwkv = 2 * (WKV_CHUNK.bit_length() - 2) + nb // 2 + 11
    n_ffn = 3 * (D_FF // FF_CHUNK) + 2

    def weight_copies():
        return (pltpu.make_async_copy(wgu_hbm.at[l], wgu_ref, sem.at[0]),
                pltpu.make_async_copy(wd_hbm.at[l], wd_ref, sem.at[1]))

    def rwkv_stages():
        bd256 = _bd_mask(GROUP_W)
        bd_ones = jnp.where(bd256, 1.0, 0.0).astype(BF16)
        zd3 = zd_ref[...]
        zd = zd3.reshape(nb * c, D_TM)
        first = lax.broadcasted_iota(jnp.int32, (nb, c, D_TM), 1) == 0
        carried = jnp.broadcast_to(prev_ref[:, 7:8, :], (nb, c, D_TM))
        zprev = jnp.where(first, carried, pltpu.roll(zd, 1, 0).reshape(nb, c, D_TM)).reshape(nb * c, D_TM)
        halves = (slice(0, nb * c // 2), slice(nb * c // 2, nb * c))
        parts = []
        for rows in halves:
            parts.append(_wkv_inputs(zd[rows], zprev[rows], prm, bd_ones))
            yield
        r, logd, k2, v, kk, a, g = (jnp.concatenate(xs, axis=0) for xs in zip(*parts))
        o, s_new = yield from _wkv_chunk_stages(r, logd, k2, v, kk, a, [s_ref[b] for b in range(nb)],
                                                bd256)
        for rows in halves:
            yield
            yd_ref[rows, :] = _wkv_output(o[rows], r[rows], k2[rows], v[rows], g[rows], prm, bd_ones)
        for b in range(nb):
            s_ref[b] = s_new[b]
            sfin_ref[b] = s_new[b]
        prev_ref[...] = zd3[:, c - 8:, :]

    def ffn_stages():
        flat = lambda ref: ref[...].reshape(nb * c, ref.shape[-1])
        ys = (flat(ya_ref), flat(yb_ref), flat(yc_ref), yd_ref[...])
        yield from _ffn_stages(flat(x_ref), ys, wo_ref, n2_ref[...], wgu_ref, wd_ref, acc_ref)
        x = acc_ref[...]
        if final:
            x = _rms(x, gf_ref[...])
        o_ref[...] = x.reshape(nb, c, D_MODEL)

    @pl.when(j == 0)
    def _():
        prev_ref[...] = jnp.zeros_like(prev_ref)
        s_ref[...] = jnp.zeros_like(s_ref)
        for cp in weight_copies():
            cp.start()
        _interleave((rwkv_stages(), n_rwkv))
        for cp in weight_copies():
            cp.wait()

    @pl.when((j > 0) & (j < nt))
    def _():
        _interleave((ffn_stages(), n_ffn), (rwkv_stages(), n_rwkv))

    @pl.when(j == nt)
    def _():
        _interleave((ffn_stages(), n_ffn))


def _rwkv_ffn(zd, x, ya, yb, yc, p, l, batch, seq, final):
    c = WKV_CHUNK
    nt = seq // c
    cur = lambda wd: pl.BlockSpec((batch, c, wd), lambda j: (0, jnp.minimum(j, nt - 1), 0))
    prv = lambda wd: pl.BlockSpec((batch, c, wd), lambda j: (0, jnp.maximum(j - 1, 0), 0))
    return pl.pallas_call(
        functools.partial(_rwkv_ffn_kernel, nb=batch, nt=nt, l=l, final=final),
        grid=(nt + 1,),
        in_specs=[cur(D_TM), prv(D_MODEL), prv(GROUP_W), prv(GROUP_W), prv(GROUP_W),
                  _layer((D_MODEL, D_MODEL), l), _layer((1, D_MODEL), l), _full((1, D_MODEL)),
                  pl.BlockSpec(memory_space=pl.ANY), pl.BlockSpec(memory_space=pl.ANY)]
        + _wkv_param_specs(l),
        out_specs=[prv(D_MODEL), _full((batch, GROUP_W, GROUP_W))],
        out_shape=[jax.ShapeDtypeStruct((batch, seq, D_MODEL), F32),
                   jax.ShapeDtypeStruct((batch, GROUP_W, GROUP_W), F32)],
        scratch_shapes=[pltpu.VMEM((batch, 8, D_TM), F32),
                        pltpu.VMEM((batch, GROUP_W, GROUP_W), F32),
                        pltpu.VMEM((batch * c, GROUP_W), F32),
                        pltpu.VMEM((batch * c, D_MODEL), F32),
                        pltpu.VMEM((D_MODEL, 2 * D_FF), BF16),
                        pltpu.VMEM((D_FF, D_MODEL), BF16),
                        pltpu.SemaphoreType.DMA((2,))],
        compiler_params=_cparams(("arbitrary",)),
    )(zd, x, ya, yb, yc, p["w_out"], p["n2"], p["gf"], p["w_gu"], p["w_down"],
      *[p[n] for n in _WKV_PARAMS])


def _sample_mix_kernel(za_ref, zb_ref, zc_ref, zd_ref, shift_ref, s_ref, ssm_ref, conv_ref,
                       lng_ref, lnb_ref, w00_ref, b0_ref, lam_ref, bb_ref, cc_ref, d_ref, gw_ref,
                       gb_ref, cw_ref, cb_ref, *rest):
    prm = {n: ref[...] for n, ref in zip(_WKV_PARAMS, rest)}
    ya_ref, yb_ref, yc_ref, yd_ref, vn_ref, snew_ref, ssmnew_ref, convnew_ref = rest[len(_WKV_PARAMS):]

    za = za_ref[...]
    vn = _gm_norm(za[:, GROUP_W:], lng_ref[...], lnb_ref[...])
    vn_ref[...] = vn
    ya_ref[...] = jax.nn.gelu(za[:, :GROUP_W]) * (w00_ref[...] * vn + b0_ref[...])

    u = zb_ref[...]
    bu = jnp.dot(u.astype(BF16), bb_ref[...], preferred_element_type=F32)
    lam = lam_ref[...]
    lr, li = lam[0:1], lam[1:2]
    st = ssm_ref[...]
    s_re, s_im = st[:, :SSM_S], st[:, SSM_S:]
    st_new = jnp.concatenate([lr * s_re - li * s_im + bu[:, :SSM_S],
                              lr * s_im + li * s_re + bu[:, SSM_S:]], axis=1)
    ssmnew_ref[...] = st_new
    yb_ref[...] = _s5_output(st_new, u, cc_ref[...], d_ref[...], gw_ref[...], gb_ref[...])

    zc = zc_ref[...]
    z = zc[:, 2 * GROUP_W:] * zc[:, :GROUP_W]
    cw = cw_ref[...]
    buf = conv_ref[...]
    y = cb_ref[...] + cw[0:1] * buf[:, :GROUP_W] + cw[1:2] * buf[:, GROUP_W:] + cw[2:3] * z
    yc_ref[...] = zc[:, GROUP_W:2 * GROUP_W] * y
    convnew_ref[:, :GROUP_W] = buf[:, GROUP_W:]
    convnew_ref[:, GROUP_W:] = z

    bd256 = _bd_mask(GROUP_W)
    bd_ones = jnp.where(bd256, 1.0, 0.0).astype(BF16)
    zd = zd_ref[...]
    r, logd, k2, v, kk, a, g = _wkv_inputs(zd, shift_ref[...], prm, bd_ones)
    bt = zd.shape[0]
    s = s_ref[...]
    eye4 = (lax.broadcasted_iota(jnp.int32, (WKV_N, GROUP_W), 0)
            == lax.broadcasted_iota(jnp.int32, (WKV_N, GROUP_W), 1) % WKV_N)

    def head_sum(x3):
        return _dot_split(x3.reshape(bt * WKV_N, GROUP_W), bd_ones).reshape(bt, WKV_N, GROUP_W)

    sa = head_sum(s * (-kk)[:, None, :])
    vcol = head_sum(jnp.where(eye4[None], v[:, None, :], 0.0))
    s_new = (s * jnp.exp(logd)[:, None, :] + sa * (kk * a)[:, None, :] + vcol * k2[:, None, :])
    snew_ref[...] = s_new
    o_rep = head_sum(s_new * r[:, None, :])
    o = jnp.sum(jnp.where(eye4[None], o_rep, 0.0), axis=1)
    yd_ref[...] = _wkv_output(o, r, k2, v, g, prm, bd_ones)


def _sample_mix(za, zb, zc, zd, shift, s_t, ssm, conv, p, l, bt):
    rows = za.shape[0]
    row_blk = lambda wd: pl.BlockSpec((bt, wd), lambda i: (i, 0))
    st_blk = lambda wd: pl.BlockSpec((None, bt, wd), lambda i: (l, i, 0))
    s_blk = pl.BlockSpec((None, bt, WKV_N, GROUP_W), lambda i: (l, i, 0, 0))
    vec = _layer((1, GROUP_W), l)
    s_arg = 5
    return pl.pallas_call(
        _sample_mix_kernel,
        grid=(rows // bt,),
        in_specs=[row_blk(2 * GROUP_W), row_blk(GROUP_W), row_blk(3 * GROUP_W), row_blk(D_TM),
                  st_blk(D_TM), s_blk, st_blk(2 * SSM_S), st_blk(2 * GROUP_W),
                  vec, vec, vec, vec]
        + [_layer(s, l) for s in _S5_SHAPES]
        + [_layer((3, GROUP_W), l), vec] + _wkv_param_specs(l),
        out_specs=[row_blk(GROUP_W)] * 5 + [s_blk, row_blk(2 * SSM_S), row_blk(2 * GROUP_W)],
        out_shape=[jax.ShapeDtypeStruct((rows, GROUP_W), F32)] * 5
        + [jax.ShapeDtypeStruct(s_t.shape, F32),
           jax.ShapeDtypeStruct((rows, 2 * SSM_S), F32),
           jax.ShapeDtypeStruct((rows, 2 * GROUP_W), F32)],
        input_output_aliases={s_arg: 5},
        compiler_params=_cparams(("parallel",)),
    )(za, zb, zc, zd, shift, s_t, ssm, conv, p["lng"], p["lnb"], p["w00"], p["b0"],
      *[p[n] for n in _S5_PARAMS], p["cw"], p["cb"], *[p[n] for n in _WKV_PARAMS])


def _out_ffn_kernel(x_ref, ya_ref, yb_ref, yc_ref, yd_ref, wo_ref, g2_ref, wg_ref, wu_ref, wd_ref,
                    gf_ref, o_ref, h_ref, *, final, ff_split):
    c = pl.program_id(1)

    @pl.when(c == 0)
    def _():
        x = x_ref[...]
        for i, y_ref in enumerate((ya_ref, yb_ref, yc_ref, yd_ref)):
            x = x + jnp.dot(y_ref[...].astype(BF16), wo_ref[i * GROUP_W:(i + 1) * GROUP_W, :],
                            preferred_element_type=F32)
        o_ref[...] = x
        h_ref[...] = _rms(x, g2_ref[...]).astype(BF16)

    h = h_ref[...]
    gate = jnp.dot(h, wg_ref[...], preferred_element_type=F32)
    up = jnp.dot(h, wu_ref[...], preferred_element_type=F32)
    act = (gate * jax.nn.sigmoid(gate) * up).astype(BF16)
    o_ref[...] += jnp.dot(act, wd_ref[...], preferred_element_type=F32)

    if final:
        @pl.when(c == ff_split - 1)
        def _():
            o_ref[...] = _rms(o_ref[...], gf_ref[...])


def _out_ffn(x, ys, p, l, tm, final, ff_split):
    rows = x.shape[0]
    fc = D_FF // ff_split
    row_blk = lambda wd_: pl.BlockSpec((tm, wd_), lambda i, c: (i, 0))
    return pl.pallas_call(
        functools.partial(_out_ffn_kernel, final=final, ff_split=ff_split),
        grid=(rows // tm, ff_split),
        in_specs=[row_blk(D_MODEL)] + [row_blk(GROUP_W)] * 4
        + [_layer((D_MODEL, D_MODEL), l), _layer((1, D_MODEL), l),
           pl.BlockSpec((None, D_MODEL, fc), lambda i, c: (l, 0, c)),
           pl.BlockSpec((None, D_MODEL, fc), lambda i, c: (l, 0, ff_split + c)),
           pl.BlockSpec((None, fc, D_MODEL), lambda i, c: (l, c, 0)),
           _full((1, D_MODEL))],
        out_specs=row_blk(D_MODEL),
        out_shape=jax.ShapeDtypeStruct((rows, D_MODEL), F32),
        scratch_shapes=[pltpu.VMEM((tm, D_MODEL), BF16)],
        compiler_params=_cparams(("parallel", "arbitrary")),
    )(x, *ys, p["w_out"], p["n2"], p["w_gu"], p["w_gu"], p["w_down"], p["gf"])


def _pad_lora(w, start):
    return jnp.pad(w.astype(BF16), ((0, 0), (start, LORA_PAD - start - w.shape[1]), (0, 0)))


def kernel(x_prompt, x_sample, state_wkv, state_shift, state_ssm_re, state_ssm_im, state_conv,
           norm1_g, w_in, gm_ln_g, gm_ln_b, gm_ws, gm_bs,
           ssm_a_re, ssm_a_im, ssm_log_dt, ssm_b_re, ssm_b_im, ssm_c_re, ssm_c_im, ssm_d,
           ssm_glu_w, ssm_glu_b, conv_w, conv_b,
           tm_mu, tm_w0, tm_w2, tm_a0, tm_a2, tm_g2, tm_k_k, tm_k_a, tm_r_k, tm_ln_g, tm_ln_b,
           w_out, norm2_g, ffn_w_gu, ffn_w_down, norm_f_g):
    depth = w_in.shape[0]
    bp, seq, _ = x_prompt.shape
    bs = x_sample.shape[0]
    assert x_sample.shape[1] == 1 and seq % GM_CHUNK == 0 and seq % WKV_CHUNK == 0
    head_d = GROUP_W // GM_HEADS
    rowv = lambda p: p.reshape(depth, 1, -1)

    lam, bb, cc = _s5_prep(ssm_a_re, ssm_a_im, ssm_log_dt, ssm_b_re, ssm_b_im, ssm_c_re, ssm_c_im)
    p = {
        "n1": rowv(norm1_g), "n2": rowv(norm2_g), "w_in": w_in.astype(BF16),
        "lng": rowv(gm_ln_g), "lnb": rowv(gm_ln_b),
        "wcat": jnp.transpose(gm_ws, (0, 2, 1, 3)).reshape(depth, GM_CHUNK, GM_HEADS * GM_CHUNK),
        "bias": jnp.repeat(jnp.transpose(gm_bs, (0, 2, 1)), head_d, axis=2),
        "w00": rowv(jnp.repeat(gm_ws[:, :, 0, 0], head_d, axis=1)),
        "b0": rowv(jnp.repeat(gm_bs[:, :, 0], head_d, axis=1)),
        "lam": lam, "bb": bb, "cc": cc, "ssm_d": rowv(ssm_d),
        "glu_w": ssm_glu_w.astype(BF16), "glu_b": rowv(ssm_glu_b),
        "cw": conv_w, "cb": rowv(conv_b),
        "mu": rowv(tm_mu), "w0": rowv(tm_w0), "w2": _pad_lora(tm_w2, 0),
        "a0": rowv(tm_a0), "a2": _pad_lora(tm_a2, 32), "g2": _pad_lora(tm_g2, 64),
        "k_k": rowv(tm_k_k), "k_a": rowv(tm_k_a), "r_k": rowv(tm_r_k),
        "ln_g": rowv(tm_ln_g), "ln_b": rowv(tm_ln_b),
        "w_out": w_out.astype(BF16), "w_gu": ffn_w_gu.astype(BF16), "w_down": ffn_w_down.astype(BF16),
        "gf": norm_f_g.reshape(1, D_MODEL),
    }

    xp = x_prompt.reshape(bp * seq, D_MODEL)
    xs = x_sample.reshape(bs, D_MODEL)
    wkv_s_buf = jnp.transpose(state_wkv, (0, 1, 3, 2, 4)).reshape(depth, bs, WKV_N, GROUP_W)
    ssm_s_in = jnp.concatenate([state_ssm_re.reshape(depth, bs, SSM_S),
                                state_ssm_im.reshape(depth, bs, SSM_S)], axis=-1)
    conv_s_in = state_conv.reshape(depth, bs, 2 * GROUP_W)

    outs = {k: [] for k in ("wkv_p", "sh_p", "sh_s", "ssm_p", "ssm_s", "cv_p", "cv_s", "chv")}
    tm_p = ROW_TILE if seq % ROW_TILE == 0 else GM_CHUNK
    bt_s = SAMPLE_TILE if bs % SAMPLE_TILE == 0 else bs
    for l in range(depth):
        final = l == depth - 1

        ya, zb, yc, zd, tail = _inproj_mix(xp, p, l, bp, seq, tm_p)
        per_batch = lambda y: y.reshape(bp, seq, y.shape[-1])
        yb, ssm_fin = _s5(per_batch(zb), p, l, bp, seq)
        xp3, wkv_fin = _rwkv_ffn(per_batch(zd), per_batch(xp), per_batch(ya), yb, per_batch(yc),
                                 p, l, bp, seq, final)
        xp = xp3.reshape(bp * seq, D_MODEL)
        outs["wkv_p"].append(wkv_fin)
        outs["sh_p"].append(zd.reshape(bp, seq, D_TM)[:, -1])
        outs["ssm_p"].append(ssm_fin)
        outs["cv_p"].append(tail[:, 6:8])

        za, zb, zc, zd = _inproj(xs, p["n1"], p["w_in"], l)
        ya, yb, yc, yd, vn, wkv_s_buf, ssm_new, conv_new = _sample_mix(
            za, zb, zc, zd, state_shift, wkv_s_buf, ssm_s_in, conv_s_in, p, l, bt_s)
        xs = _out_ffn(xs, (ya, yb, yc, yd), p, l, bs, final, FF_SPLIT_SAMPLE)
        outs["sh_s"].append(zd)
        outs["ssm_s"].append(ssm_new)
        outs["cv_s"].append(conv_new.reshape(bs, 2, GROUP_W))
        outs["chv"].append(vn.reshape(bs, 1, GROUP_W))

    def wkv_blocks(s_bd):
        s5d = s_bd.reshape(depth, -1, WKV_HEADS, WKV_N, WKV_HEADS, WKV_N)
        return jnp.stack([s5d[:, :, h, :, h, :] for h in range(WKV_HEADS)], axis=2)

    wkv_p = wkv_blocks(jnp.stack(outs["wkv_p"]))
    wkv_s = jnp.transpose(wkv_s_buf.reshape(depth, bs, WKV_N, WKV_HEADS, WKV_N), (0, 1, 3, 2, 4))
    ssm_p = jnp.stack(outs["ssm_p"])
    ssm_s = jnp.stack(outs["ssm_s"])
    split = lambda s, i: s[..., i * SSM_S:(i + 1) * SSM_S].reshape(depth, -1, SSM_GROUPS, SSM_P)
    return (xp.reshape(bp, seq, D_MODEL), xs.reshape(bs, 1, D_MODEL),
            wkv_p, wkv_s,
            jnp.stack(outs["sh_p"]), jnp.stack(outs["sh_s"]),
            split(ssm_p, 0), split(ssm_s, 0), split(ssm_p, 1), split(ssm_s, 1),
            jnp.stack(outs["cv_p"]), jnp.stack(outs["cv_s"]),
            jnp.stack(outs["chv"]))
```

```python
import functools

import jax
import jax.numpy as jnp
from jax import lax
from jax.experimental import pallas as pl
from jax.experimental.pallas import tpu as pltpu

F32 = jnp.float32
BF16 = jnp.bfloat16

D_MODEL = 1024
GROUP_W = 256
GM_CHUNK = 128
GM_HEADS = 4
SSM_CH = 16
SSM_GROUPS = 16
SSM_P = 64
SSM_S = SSM_GROUPS * SSM_P
WKV_N = 64
WKV_HEADS = 4
LORA_PAD = 128
D_TM = 3 * GROUP_W + LORA_PAD
IN_COLS = 6 * GROUP_W + D_TM
D_FF = 2816
FF_SPLIT_SAMPLE = 11
NORM_EPS = 1e-6
GM_LN_EPS = 1e-5
WKV_LN_EPS = 64e-5

WKV_CHUNK = 64
S5_TCHUNK = 128
ROW_TILE = 512
SAMPLE_TILE = 32
VMEM_LIMIT = 56 * 1024 * 1024


def _cparams(sem):
    return pltpu.CompilerParams(dimension_semantics=sem, vmem_limit_bytes=VMEM_LIMIT)


def _full(shape):
    n = len(shape)
    return pl.BlockSpec(shape, lambda *_: (0,) * n)


def _layer(shape, l):
    n = len(shape)
    return pl.BlockSpec((None,) + tuple(shape), lambda *_: (l,) + (0,) * n)


_VEC256 = ("lng", "lnb", "w00", "b0", "cb", "ssm_d", "glu_b", "w0", "a0", "k_k", "k_a", "r_k", "ln_g", "ln_b")
_VEC1024 = ("n1", "n2")
_MAT_SHAPES = {
    "w_in": (D_MODEL, IN_COLS), "wcat": (GM_CHUNK, GM_HEADS * GM_CHUNK), "bias": (GM_CHUNK, GROUP_W),
    "cw": (3, GROUP_W), "lam": (2, SSM_S), "bb": (GROUP_W, 2 * SSM_S), "cc": (2 * SSM_S, GROUP_W),
    "glu_w": (GROUP_W, GROUP_W), "mu": (1, D_TM), "w2": (LORA_PAD, GROUP_W), "a2": (LORA_PAD, GROUP_W),
    "g2": (LORA_PAD, GROUP_W), "w_out": (D_MODEL, D_MODEL),
}


def _pspec(name, l):
    for names, width in ((_VEC256, GROUP_W), (_VEC1024, D_MODEL)):
        if name in names:
            row = names.index(name)
            return pl.BlockSpec((None, None, 1, width), lambda *_: (l, row, 0, 0))
    return _layer(_MAT_SHAPES[name], l)


def _pspecs(names, l):
    return [_pspec(n, l) for n in names]


def _dot(a, b):
    return jnp.dot(a.astype(BF16), b.astype(BF16), preferred_element_type=F32)


def _dot_nt(a, b):
    return lax.dot_general(a.astype(BF16), b.astype(BF16), (((1,), (1,)), ((), ())),
                           preferred_element_type=F32)


def _dot_tn(a, b):
    return lax.dot_general(a.astype(BF16), b.astype(BF16), (((0,), (0,)), ((), ())),
                           preferred_element_type=F32)


def _dot_split(x, ones_bf16):
    hi = x.astype(BF16)
    lo = (x - hi.astype(F32)).astype(BF16)
    return (jnp.dot(hi, ones_bf16, preferred_element_type=F32)
            + jnp.dot(lo, ones_bf16, preferred_element_type=F32))


def _rms(x, g):
    return x * lax.rsqrt(jnp.mean(x * x, axis=-1, keepdims=True) + NORM_EPS) * g


def _softplus(y):
    return jnp.maximum(y, 0.0) + jnp.log1p(jnp.exp(-jnp.abs(y)))


def _gm_norm(zav, ln_g, ln_b):
    vf = jax.nn.gelu(zav)
    mu = jnp.mean(vf, axis=-1, keepdims=True)
    var = jnp.mean(jnp.square(vf - mu), axis=-1, keepdims=True)
    return (vf - mu) * lax.rsqrt(var + GM_LN_EPS) * ln_g + ln_b


def _inproj_kernel(x_ref, g_ref, w_ref, za_ref, zb_ref, zc_ref, zd_ref):
    h = _rms(x_ref[...], g_ref[...])
    z = jnp.dot(h.astype(BF16), w_ref[...], preferred_element_type=F32)
    za_ref[...] = z[:, 0:2 * GROUP_W]
    zb_ref[...] = z[:, 2 * GROUP_W:3 * GROUP_W]
    zc_ref[...] = z[:, 3 * GROUP_W:6 * GROUP_W]
    zd_ref[...] = z[:, 6 * GROUP_W:]


def _inproj(x, p, l):
    rows = x.shape[0]
    widths = (2 * GROUP_W, GROUP_W, 3 * GROUP_W, D_TM)
    return pl.pallas_call(
        _inproj_kernel,
        grid=(1,),
        in_specs=[_full((rows, D_MODEL))] + _pspecs(("n1", "w_in"), l),
        out_specs=[_full((rows, wd)) for wd in widths],
        out_shape=[jax.ShapeDtypeStruct((rows, wd), F32) for wd in widths],
        compiler_params=_cparams(("arbitrary",)),
    )(x, p["n1"], p["w_in"])


def _inproj_mix_kernel(x_ref, g_ref, w_ref, lng_ref, lnb_ref, wcat_ref, bias_ref, cw_ref, cb_ref,
                       ya_ref, zb_ref, yc_ref, zd_ref, tail_ref, prev_ref, *, tile):
    @pl.when(pl.program_id(1) == 0)
    def _():
        prev_ref[...] = jnp.zeros_like(prev_ref)

    h = _rms(x_ref[...], g_ref[...])
    z = jnp.dot(h.astype(BF16), w_ref[...], preferred_element_type=F32)
    zb_ref[...] = z[:, 2 * GROUP_W:3 * GROUP_W]
    zd_ref[...] = z[:, 6 * GROUP_W:]

    u = jax.nn.gelu(z[:, :GROUP_W])
    vn = _gm_norm(z[:, GROUP_W:2 * GROUP_W], lng_ref[...], lnb_ref[...])
    kc = GM_HEADS * GM_CHUNK
    t_i = lax.broadcasted_iota(jnp.int32, (GM_CHUNK, kc), 0)
    s_i = lax.broadcasted_iota(jnp.int32, (GM_CHUNK, kc), 1) % GM_CHUNK
    wm = jnp.where(s_i <= t_i, wcat_ref[...], 0.0).astype(BF16)
    r_h = lax.broadcasted_iota(jnp.int32, (kc, GROUP_W), 0) // GM_CHUNK
    c_h = lax.broadcasted_iota(jnp.int32, (kc, GROUP_W), 1) // (GROUP_W // GM_HEADS)
    head_mask = r_h == c_h
    for c in range(tile // GM_CHUNK):
        rows = slice(c * GM_CHUNK, (c + 1) * GM_CHUNK)
        vc = vn[rows].astype(BF16)
        rhs = jnp.where(head_mask, jnp.concatenate([vc] * GM_HEADS, axis=0), jnp.zeros((), BF16))
        s = jnp.dot(wm, rhs, preferred_element_type=F32) + bias_ref[...]
        ya_ref[rows, :] = u[rows] * s

    zz = z[:, 5 * GROUP_W:6 * GROUP_W] * z[:, 3 * GROUP_W:4 * GROUP_W]
    row = lax.broadcasted_iota(jnp.int32, zz.shape, 0)
    prev = prev_ref[...]
    z1 = jnp.where(row == 0, prev[7:8], pltpu.roll(zz, 1, 0))
    z2 = jnp.where(row == 0, prev[6:7], jnp.where(row == 1, prev[7:8], pltpu.roll(zz, 2, 0)))
    cw = cw_ref[...]
    y = cb_ref[...] + cw[0:1] * z2 + cw[1:2] * z1 + cw[2:3] * zz
    yc_ref[...] = z[:, 4 * GROUP_W:5 * GROUP_W] * y
    prev_ref[...] = zz[tile - 8:]
    tail_ref[...] = zz[tile - 8:]


def _inproj_mix(x, p, l, batch, seq, tile):
    nt = seq // tile
    row_blk = lambda wd: pl.BlockSpec((None, tile, wd), lambda b, j: (b, j, 0))
    widths = (GROUP_W, GROUP_W, GROUP_W, D_TM)
    names = ("n1", "w_in", "lng", "lnb", "wcat", "bias", "cw", "cb")
    return pl.pallas_call(
        functools.partial(_inproj_mix_kernel, tile=tile),
        grid=(batch, nt),
        in_specs=[row_blk(D_MODEL)] + _pspecs(names, l),
        out_specs=[row_blk(wd) for wd in widths]
        + [pl.BlockSpec((None, 8, GROUP_W), lambda b, j: (b, 0, 0))],
        out_shape=[jax.ShapeDtypeStruct((batch, seq, wd), F32) for wd in widths]
        + [jax.ShapeDtypeStruct((batch, 8, GROUP_W), F32)],
        scratch_shapes=[pltpu.VMEM((8, GROUP_W), F32)],
        compiler_params=_cparams(("parallel", "arbitrary")),
    )(x, *[p[n] for n in names])


def _s5_prep_kernel(are_ref, aim_ref, ldt_ref, bre_ref, bim_ref, cre_ref, cim_ref,
                    lam_ref, bb_ref, cc_ref):
    lam_re = jnp.minimum(are_ref[...], -1e-4)
    lam_im = aim_ref[...]
    dt = jnp.exp(ldt_ref[...])
    mag = jnp.exp(lam_re * dt)
    lb_re = mag * jnp.cos(lam_im * dt)
    lb_im = mag * jnp.sin(lam_im * dt)
    den = lam_re * lam_re + lam_im * lam_im
    f_re = ((lb_re - 1.0) * lam_re + lb_im * lam_im) / den
    f_im = (lb_im * lam_re - (lb_re - 1.0) * lam_im) / den
    lam_ref[0:1, :] = lb_re
    lam_ref[1:2, :] = lb_im
    br, bi = bre_ref[...], bim_ref[...]
    grp_r = lax.broadcasted_iota(jnp.int32, (GROUP_W, SSM_S), 0) // SSM_CH
    grp_c = lax.broadcasted_iota(jnp.int32, (GROUP_W, SSM_S), 1) // SSM_P
    m = grp_r == grp_c
    bb_ref[:, :SSM_S] = jnp.where(m, f_re * br - f_im * bi, 0.0).astype(BF16)
    bb_ref[:, SSM_S:] = jnp.where(m, f_re * bi + f_im * br, 0.0).astype(BF16)
    grp_r2 = lax.broadcasted_iota(jnp.int32, (SSM_S, GROUP_W), 0) // SSM_P
    grp_c2 = lax.broadcasted_iota(jnp.int32, (SSM_S, GROUP_W), 1) // SSM_CH
    m2 = grp_r2 == grp_c2
    cc_ref[:SSM_S, :] = jnp.where(m2, cre_ref[...], 0.0).astype(BF16)
    cc_ref[SSM_S:, :] = jnp.where(m2, -cim_ref[...], 0.0).astype(BF16)


def _s5_prep(a_re, a_im, log_dt, b_re, b_im, c_re, c_im):
    depth = a_re.shape[0]
    flat = lambda p: p.reshape(depth, 1, SSM_S)
    b_exp = lambda b: jnp.tile(jnp.transpose(b, (0, 3, 1, 2)).reshape(depth, SSM_CH, SSM_S),
                               (1, SSM_GROUPS, 1))
    c_exp = lambda c: jnp.tile(jnp.transpose(c, (0, 1, 3, 2)).reshape(depth, SSM_S, SSM_CH),
                               (1, 1, SSM_GROUPS))
    lyr = lambda shape: pl.BlockSpec((None,) + shape, lambda l: (l, 0, 0))
    return pl.pallas_call(
        _s5_prep_kernel,
        grid=(depth,),
        in_specs=[lyr((1, SSM_S))] * 3 + [lyr((GROUP_W, SSM_S))] * 2 + [lyr((SSM_S, GROUP_W))] * 2,
        out_specs=[lyr((2, SSM_S)), lyr((GROUP_W, 2 * SSM_S)), lyr((2 * SSM_S, GROUP_W))],
        out_shape=[jax.ShapeDtypeStruct((depth, 2, SSM_S), F32),
                   jax.ShapeDtypeStruct((depth, GROUP_W, 2 * SSM_S), BF16),
                   jax.ShapeDtypeStruct((depth, 2 * SSM_S, GROUP_W), BF16)],
        compiler_params=_cparams(("arbitrary",)),
    )(flat(a_re), flat(a_im), flat(log_dt), b_exp(b_re), b_exp(b_im), c_exp(c_re), c_exp(c_im))


def _s5_output(st, u, cc, d, glu_w, glu_b):
    y = jnp.dot(st.astype(BF16), cc, preferred_element_type=F32) + d * u
    y = jax.nn.gelu(y)
    return y * jax.nn.sigmoid(jnp.dot(y.astype(BF16), glu_w, preferred_element_type=F32) + glu_b)


_S5_PARAMS = ("lam", "bb", "cc", "ssm_d", "glu_w", "glu_b")


S5_SLABS = 2 * SSM_S // 128
S5_PITCH = S5_TCHUNK + 8


def _s5_kernel(u_ref, lam_ref, bb_ref, cc_ref, d_ref, gw_ref, gb_ref, y_ref, fin_ref,
               bu_ref, st_ref, *, batch, tsteps):
    half = S5_SLABS // 2

    @pl.when(pl.program_id(0) == 0)
    def _():
        st_ref[...] = jnp.zeros_like(st_ref)

    u = u_ref[...].reshape(batch * tsteps, GROUP_W)
    bu = jnp.dot(u.astype(BF16), bb_ref[...], preferred_element_type=F32)
    for b in range(batch):
        for s in range(S5_SLABS):
            bu_ref[s, b * S5_PITCH:b * S5_PITCH + tsteps, :] = (
                bu[b * tsteps:(b + 1) * tsteps, s * 128:(s + 1) * 128])
    lam = lam_ref[...]
    lr = [jnp.broadcast_to(lam[0:1, s * 128:(s + 1) * 128], (batch, 128)) for s in range(half)]
    li = [jnp.broadcast_to(lam[1:2, s * 128:(s + 1) * 128], (batch, 128)) for s in range(half)]

    def step(t, carry):
        rows = pl.ds(t, batch, stride=S5_PITCH)
        new = [None] * S5_SLABS
        for s in range(half):
            s_re, s_im = carry[s], carry[half + s]
            new[s] = lr[s] * s_re - li[s] * s_im + bu_ref[s, rows, :]
            new[half + s] = lr[s] * s_im + li[s] * s_re + bu_ref[half + s, rows, :]
            bu_ref[s, rows, :] = new[s]
            bu_ref[half + s, rows, :] = new[half + s]
        return tuple(new)

    st0 = st_ref[...]
    fin = lax.fori_loop(0, tsteps, step,
                        tuple(st0[:, s * 128:(s + 1) * 128] for s in range(S5_SLABS)), unroll=4)
    for s in range(S5_SLABS):
        st_ref[:, s * 128:(s + 1) * 128] = fin[s]
        fin_ref[:, s * 128:(s + 1) * 128] = fin[s]
    st = jnp.concatenate(
        [jnp.concatenate([bu_ref[s, b * S5_PITCH:b * S5_PITCH + tsteps, :] for s in range(S5_SLABS)],
                         axis=1) for b in range(batch)], axis=0)
    y = _s5_output(st, u, cc_ref[...], d_ref[...], gw_ref[...], gb_ref[...])
    y_ref[...] = y.reshape(batch, tsteps, GROUP_W)


def _s5(u, p, l, batch, seq):
    tsteps = min(S5_TCHUNK, seq)
    blk = pl.BlockSpec((batch, tsteps, GROUP_W), lambda i: (0, i, 0))
    return pl.pallas_call(
        functools.partial(_s5_kernel, batch=batch, tsteps=tsteps),
        grid=(seq // tsteps,),
        in_specs=[blk] + _pspecs(_S5_PARAMS, l),
        out_specs=[blk, _full((batch, 2 * SSM_S))],
        out_shape=[jax.ShapeDtypeStruct((batch, seq, GROUP_W), F32),
                   jax.ShapeDtypeStruct((batch, 2 * SSM_S), F32)],
        scratch_shapes=[pltpu.VMEM((S5_SLABS, batch * S5_PITCH, 128), F32),
                        pltpu.VMEM((batch, 2 * SSM_S), F32)],
        compiler_params=_cparams(("arbitrary",)),
    )(u, *[p[n] for n in _S5_PARAMS])


def _wkv_inputs(zd, zprev, p, bd_ones):
    zs = zd + p["mu"] * (zprev - zd)
    r = zs[:, 0:GROUP_W]
    k = zs[:, GROUP_W:2 * GROUP_W]
    v = zs[:, 2 * GROUP_W:3 * GROUP_W]
    lora = zs[:, 3 * GROUP_W:]
    w = -_softplus(-(p["w0"] + _dot(jnp.tanh(lora), p["w2"]))) - 0.5
    logd = -jnp.exp(w)
    a = jax.nn.sigmoid(p["a0"] + _dot(lora, p["a2"]))
    g = _dot(jax.nn.sigmoid(lora), p["g2"])
    kk = k * p["k_k"]
    nrm = jnp.sqrt(_dot_split(kk * kk, bd_ones))
    kk = kk / jnp.maximum(nrm, 1e-12)
    k2 = k * (1.0 + (a - 1.0) * p["k_a"])
    return r, logd, k2, v, kk, a, g


def _wkv_output(o, r, k2, v, g, p, bd_ones):
    inv_n = 1.0 / WKV_N
    m = _dot_split(o, bd_ones) * inv_n
    var = _dot_split(jnp.square(o - m), bd_ones) * inv_n
    on = (o - m) * lax.rsqrt(var + WKV_LN_EPS) * p["ln_g"] + p["ln_b"]
    bonus = _dot_split(r * k2 * p["r_k"], bd_ones) * v
    return (on + bonus) * g


_WKV_PARAMS = ("mu", "w0", "w2", "a0", "a2", "g2", "k_k", "k_a", "r_k", "ln_g", "ln_b")


def _bd_mask(n):
    hr = lax.broadcasted_iota(jnp.int32, (n, n), 0) // (n // WKV_HEADS)
    hc = lax.broadcasted_iota(jnp.int32, (n, n), 1) // (n // WKV_HEADS)
    return hr == hc


def _expand(xp, lo_bf16):
    xb = xp.astype(BF16)
    zero = jnp.zeros_like(lo_bf16)
    hi_bf16 = 1 - lo_bf16
    t0, t1 = xb[:, :128], xb[:, 128:]
    return jnp.concatenate([jnp.concatenate([t0 * lo_bf16, zero], axis=1),
                            jnp.concatenate([t0 * hi_bf16, zero], axis=1),
                            jnp.concatenate([zero, t1 * lo_bf16], axis=1),
                            jnp.concatenate([zero, t1 * hi_bf16], axis=1)], axis=0)


def _interleave(*staged):
    gens = [g for g, _ in staged]
    total = [n for _, n in staged]
    done = [0] * len(gens)
    live = set(range(len(gens)))
    while live:
        i = min(live, key=lambda k: (done[k] / total[k], k))
        try:
            next(gens[i])
            done[i] += 1
        except StopIteration:
            live.remove(i)


def _wkv_chunk_stages(r, logd, k2, v, kk, a, states, bd256):
    c = WKV_CHUNK
    n = len(states)
    rows = [slice(i * c, (i + 1) * c) for i in range(n)]
    t_i = lax.broadcasted_iota(jnp.int32, (c, GROUP_W), 0)
    s_i = lax.broadcasted_iota(jnp.int32, (c, GROUP_W), 1) % c
    strict = s_i < t_i
    incl = s_i <= t_i
    lo_bf16 = jnp.where(lax.broadcasted_iota(jnp.int32, (c, 128), 1) < WKV_N, 1.0, 0.0).astype(BF16)
    ex = lambda xp: _expand(xp, lo_bf16)

    pos = lax.broadcasted_iota(jnp.int32, logd.shape, 0) % c
    cum_all = logd
    step = 1
    while step < c:
        cum_all = cum_all + jnp.where(pos >= step, pltpu.roll(cum_all, step, 0), 0.0)
        step *= 2
    cum = [cum_all[rw] for rw in rows]
    yield
    g_last = [cm[c - 1:c] for cm in cum]
    a_t, r_t, b_t, k_t, b_h, k_h, v_c = [], [], [], [], [], [], []
    for i, rw in enumerate(rows):
        e_neg = jnp.exp(-cum[i])
        e_end = jnp.exp(g_last[i] - cum[i])
        bvec = kk[rw] * a[rw]
        a_t.append(-kk[rw] * jnp.exp(cum[i] - logd[rw]))
        r_t.append(r[rw] * jnp.exp(cum[i]))
        b_t.append(bvec * e_neg)
        k_t.append(k2[rw] * e_neg)
        b_h.append(bvec * e_end)
        k_h.append(k2[rw] * e_end)
        v_c.append(v[rw])
        if i % 2 == 1:
            yield

    ar = [jnp.concatenate([a_t[i], r_t[i]], axis=0) for i in range(n)]
    p_b = [_dot_nt(ar[i], ex(b_t[i])) for i in range(n)]
    yield
    p_k = [_dot_nt(ar[i], ex(k_t[i])) for i in range(n)]
    yield
    l_p = [jnp.where(strict, p[:c], 0.0) for p in p_b]
    aak = [jnp.where(strict, p[:c], 0.0) for p in p_k]
    rb = [jnp.where(incl, p[c:], 0.0) for p in p_b]
    rk = [jnp.where(incl, p[c:], 0.0) for p in p_k]

    def off_diag(m):
        return (t_i // (2 * m) == s_i // (2 * m)) & (t_i % (2 * m) >= m) & (s_i % (2 * m) < m)

    eye_p = jnp.where(s_i == t_i, 1.0, 0.0)
    t_p = [eye_p + jnp.where(off_diag(1), l_p[i], 0.0) for i in range(n)]
    m = 2
    while m < c:
        off = off_diag(m)
        x = [_dot(t_p[i], ex(jnp.where(off, l_p[i], 0.0))) for i in range(n)]
        yield
        t_p = [t_p[i] + _dot(x[i], ex(t_p[i])) for i in range(n)]
        yield
        m *= 2

    v_bd = [ex(v_c[i]) for i in range(n)]
    a2 = [_dot(t_p[i], ex(a_t[i])) for i in range(n)]
    av = [_dot(aak[i], v_bd[i]) for i in range(n)]
    yield
    w0 = [_dot(t_p[i], ex(av[i])) for i in range(n)]
    from_s = [_dot_nt(jnp.concatenate([a2[i], r_t[i]], axis=0), states[i]) for i in range(n)]
    yield
    w = [from_s[i][:c] + w0[i] for i in range(n)]
    o = [from_s[i][c:] + _dot(jnp.concatenate([rb[i], rk[i]], axis=1),
                              jnp.concatenate([ex(w[i]), v_bd[i]], axis=0)) for i in range(n)]
    yield
    upd = [_dot_tn(jnp.concatenate([w[i], v_c[i]], axis=0), jnp.concatenate([b_h[i], k_h[i]], axis=0))
           for i in range(n)]
    s_new = [states[i] * jnp.exp(g_last[i]) + jnp.where(bd256, upd[i], 0.0) for i in range(n)]
    return jnp.concatenate(o, axis=0), s_new


FF_CHUNK = 256


def _ffn_stages(x, ys, wo_ref, n2, wgu_ref, wd_ref, acc_ref):
    for i, y in enumerate(ys):
        x = x + jnp.dot(y.astype(BF16), wo_ref[i * GROUP_W:(i + 1) * GROUP_W, :],
                        preferred_element_type=F32)
    acc_ref[...] = x
    yield
    h = _rms(x, n2).astype(BF16)
    for c in range(D_FF // FF_CHUNK):
        gate = jnp.dot(h, wgu_ref[:, c * FF_CHUNK:(c + 1) * FF_CHUNK], preferred_element_type=F32)
        yield
        up = jnp.dot(h, wgu_ref[:, D_FF + c * FF_CHUNK:D_FF + (c + 1) * FF_CHUNK],
                     preferred_element_type=F32)
        yield
        act = (gate * jax.nn.sigmoid(gate) * up).astype(BF16)
        acc_ref[...] += jnp.dot(act, wd_ref[c * FF_CHUNK:(c + 1) * FF_CHUNK, :], preferred_element_type=F32)
        yield


def _rwkv_ffn_kernel(zd_ref, x_ref, ya_ref, yb_ref, yc_ref, wo_ref, n2_ref, gf_ref, wgu_hbm, wd_hbm,
                     *rest, nb, nt, l, final):
    prm = {n: ref[...] for n, ref in zip(_WKV_PARAMS, rest)}
    o_ref, sfin_ref, prev_ref, s_ref, yd_ref, acc_ref, wgu_ref, wd_ref, sem = rest[len(_WKV_PARAMS):]
    c = WKV_CHUNK
    j = pl.program_id(0)
    n_rwkv = 2 * (WKV_CHUNK.bit_length() - 2) + nb // 2 + 11
    n_ffn = 3 * (D_FF // FF_CHUNK) + 2

    def weight_copies():
        return (pltpu.make_async_copy(wgu_hbm.at[l], wgu_ref, sem.at[0]),
                pltpu.make_async_copy(wd_hbm.at[l], wd_ref, sem.at[1]))

    def rwkv_stages():
        bd256 = _bd_mask(GROUP_W)
        bd_ones = jnp.where(bd256, 1.0, 0.0).astype(BF16)
        zd3 = zd_ref[...]
        zd = zd3.reshape(nb * c, D_TM)
        first = lax.broadcasted_iota(jnp.int32, (nb, c, D_TM), 1) == 0
        carried = jnp.broadcast_to(prev_ref[:, 7:8, :], (nb, c, D_TM))
        zprev = jnp.where(first, carried, pltpu.roll(zd, 1, 0).reshape(nb, c, D_TM)).reshape(nb * c, D_TM)
        halves = (slice(0, nb * c // 2), slice(nb * c // 2, nb * c))
        parts = []
        for rows in halves:
            parts.append(_wkv_inputs(zd[rows], zprev[rows], prm, bd_ones))
            yield
        r, logd, k2, v, kk, a, g = (jnp.concatenate(xs, axis=0) for xs in zip(*parts))
        o, s_new = yield from _wkv_chunk_stages(r, logd, k2, v, kk, a, [s_ref[b] for b in range(nb)],
                                                bd256)
        for rows in halves:
            yield
            yd_ref[rows, :] = _wkv_output(o[rows], r[rows], k2[rows], v[rows], g[rows], prm, bd_ones)
        for b in range(nb):
            s_ref[b] = s_new[b]
            sfin_ref[b] = s_new[b]
        prev_ref[...] = zd3[:, c - 8:, :]

    def ffn_stages():
        flat = lambda ref: ref[...].reshape(nb * c, ref.shape[-1])
        ys = (flat(ya_ref), flat(yb_ref), flat(yc_ref), yd_ref[...])
        yield from _ffn_stages(flat(x_ref), ys, wo_ref, n2_ref[...], wgu_ref, wd_ref, acc_ref)
        x = acc_ref[...]
        if final:
            x = _rms(x, gf_ref[...])
        o_ref[...] = x.reshape(nb, c, D_MODEL)

    @pl.when(j == 0)
    def _():
        prev_ref[...] = jnp.zeros_like(prev_ref)
        s_ref[...] = jnp.zeros_like(s_ref)
        for cp in weight_copies():
            cp.start()
        _interleave((rwkv_stages(), n_rwkv))
        for cp in weight_copies():
            cp.wait()

    @pl.when((j > 0) & (j < nt))
    def _():
        _interleave((ffn_stages(), n_ffn), (rwkv_stages(), n_rwkv))

    @pl.when(j == nt)
    def _():
        _interleave((ffn_stages(), n_ffn))


def _rwkv_ffn(zd, x, ya, yb, yc, p, l, batch, seq, final):
    c = WKV_CHUNK
    nt = seq // c
    cur = lambda wd: pl.BlockSpec((batch, c, wd), lambda j: (0, jnp.minimum(j, nt - 1), 0))
    prv = lambda wd: pl.BlockSpec((batch, c, wd), lambda j: (0, jnp.maximum(j - 1, 0), 0))
    return pl.pallas_call(
        functools.partial(_rwkv_ffn_kernel, nb=batch, nt=nt, l=l, final=final),
        grid=(nt + 1,),
        in_specs=[cur(D_TM), prv(D_MODEL), prv(GROUP_W), prv(GROUP_W), prv(GROUP_W),
                  _pspec("w_out", l), _pspec("n2", l), _full((1, D_MODEL)),
                  pl.BlockSpec(memory_space=pl.ANY), pl.BlockSpec(memory_space=pl.ANY)]
        + _pspecs(_WKV_PARAMS, l),
        out_specs=[prv(D_MODEL), _full((batch, GROUP_W, GROUP_W))],
        out_shape=[jax.ShapeDtypeStruct((batch, seq, D_MODEL), F32),
                   jax.ShapeDtypeStruct((batch, GROUP_W, GROUP_W), F32)],
        scratch_shapes=[pltpu.VMEM((batch, 8, D_TM), F32),
                        pltpu.VMEM((batch, GROUP_W, GROUP_W), F32),
                        pltpu.VMEM((batch * c, GROUP_W), F32),
                        pltpu.VMEM((batch * c, D_MODEL), F32),
                        pltpu.VMEM((D_MODEL, 2 * D_FF), BF16),
                        pltpu.VMEM((D_FF, D_MODEL), BF16),
                        pltpu.SemaphoreType.DMA((2,))],
        compiler_params=_cparams(("arbitrary",)),
    )(zd, x, ya, yb, yc, p["w_out"], p["n2"], p["gf"], p["w_gu"], p["w_down"],
      *[p[n] for n in _WKV_PARAMS])


def _sample_mix_kernel(za_ref, zb_ref, zc_ref, zd_ref, shift_ref, s_ref, ssm_ref, conv_ref,
                       lng_ref, lnb_ref, w00_ref, b0_ref, lam_ref, bb_ref, cc_ref, d_ref, gw_ref,
                       gb_ref, cw_ref, cb_ref, *rest):
    prm = {n: ref[...] for n, ref in zip(_WKV_PARAMS, rest)}
    ya_ref, yb_ref, yc_ref, yd_ref, vn_ref, snew_ref, ssmnew_ref, convnew_ref = rest[len(_WKV_PARAMS):]

    za = za_ref[...]
    vn = _gm_norm(za[:, GROUP_W:], lng_ref[...], lnb_ref[...])
    vn_ref[...] = vn
    ya_ref[...] = jax.nn.gelu(za[:, :GROUP_W]) * (w00_ref[...] * vn + b0_ref[...])

    u = zb_ref[...]
    bu = jnp.dot(u.astype(BF16), bb_ref[...], preferred_element_type=F32)
    lam = lam_ref[...]
    lr, li = lam[0:1], lam[1:2]
    st = ssm_ref[...]
    s_re, s_im = st[:, :SSM_S], st[:, SSM_S:]
    st_new = jnp.concatenate([lr * s_re - li * s_im + bu[:, :SSM_S],
                              lr * s_im + li * s_re + bu[:, SSM_S:]], axis=1)
    ssmnew_ref[...] = st_new
    yb_ref[...] = _s5_output(st_new, u, cc_ref[...], d_ref[...], gw_ref[...], gb_ref[...])

    zc = zc_ref[...]
    z = zc[:, 2 * GROUP_W:] * zc[:, :GROUP_W]
    cw = cw_ref[...]
    buf = conv_ref[...]
    y = cb_ref[...] + cw[0:1] * buf[:, :GROUP_W] + cw[1:2] * buf[:, GROUP_W:] + cw[2:3] * z
    yc_ref[...] = zc[:, GROUP_W:2 * GROUP_W] * y
    convnew_ref[:, :GROUP_W] = buf[:, GROUP_W:]
    convnew_ref[:, GROUP_W:] = z

    bd256 = _bd_mask(GROUP_W)
    bd_ones = jnp.where(bd256, 1.0, 0.0).astype(BF16)
    zd = zd_ref[...]
    r, logd, k2, v, kk, a, g = _wkv_inputs(zd, shift_ref[...], prm, bd_ones)
    bt = zd.shape[0]
    s4 = s_ref[...]
    s = jnp.concatenate([s4[:, h] for h in range(WKV_HEADS)], axis=-1)
    eye4 = (lax.broadcasted_iota(jnp.int32, (WKV_N, GROUP_W), 0)
            == lax.broadcasted_iota(jnp.int32, (WKV_N, GROUP_W), 1) % WKV_N)

    def head_sum(x3):
        return _dot_split(x3.reshape(bt * WKV_N, GROUP_W), bd_ones).reshape(bt, WKV_N, GROUP_W)

    sa = head_sum(s * (-kk)[:, None, :])
    vcol = head_sum(jnp.where(eye4[None], v[:, None, :], 0.0))
    s_new = (s * jnp.exp(logd)[:, None, :] + sa * (kk * a)[:, None, :] + vcol * k2[:, None, :])
    for h in range(WKV_HEADS):
        snew_ref[:, h] = s_new[:, :, h * WKV_N:(h + 1) * WKV_N]
    o_rep = head_sum(s_new * r[:, None, :])
    o = jnp.sum(jnp.where(eye4[None], o_rep, 0.0), axis=1)
    yd_ref[...] = _wkv_output(o, r, k2, v, g, prm, bd_ones)


def _sample_mix(za, zb, zc, zd, shift, wkv, ssm, conv, p, l, bt):
    rows = za.shape[0]
    row_blk = lambda wd: pl.BlockSpec((bt, wd), lambda i: (i, 0))
    st_blk = lambda wd: pl.BlockSpec((None, bt, wd), lambda i: (l, i, 0))
    s_dims = (WKV_HEADS, WKV_N, WKV_N)
    names = ("lng", "lnb", "w00", "b0") + _S5_PARAMS + ("cw", "cb") + _WKV_PARAMS
    return pl.pallas_call(
        _sample_mix_kernel,
        grid=(rows // bt,),
        in_specs=[row_blk(2 * GROUP_W), row_blk(GROUP_W), row_blk(3 * GROUP_W), row_blk(D_TM),
                  st_blk(D_TM), pl.BlockSpec((None, bt) + s_dims, lambda i: (l, i, 0, 0, 0)),
                  st_blk(2 * SSM_S), st_blk(2 * GROUP_W)] + _pspecs(names, l),
        out_specs=[row_blk(GROUP_W)] * 5
        + [pl.BlockSpec((bt,) + s_dims, lambda i: (i, 0, 0, 0)), row_blk(2 * SSM_S), row_blk(2 * GROUP_W)],
        out_shape=[jax.ShapeDtypeStruct((rows, GROUP_W), F32)] * 5
        + [jax.ShapeDtypeStruct((rows,) + s_dims, F32),
           jax.ShapeDtypeStruct((rows, 2 * SSM_S), F32),
           jax.ShapeDtypeStruct((rows, 2 * GROUP_W), F32)],
        compiler_params=_cparams(("parallel",)),
    )(za, zb, zc, zd, shift, wkv, ssm, conv, *[p[n] for n in names])


def _out_ffn_kernel(x_ref, ya_ref, yb_ref, yc_ref, yd_ref, wo_ref, g2_ref, wg_ref, wu_ref, wd_ref,
                    gf_ref, o_ref, h_ref, *, final, ff_split):
    c = pl.program_id(1)

    @pl.when(c == 0)
    def _():
        x = x_ref[...]
        for i, y_ref in enumerate((ya_ref, yb_ref, yc_ref, yd_ref)):
            x = x + jnp.dot(y_ref[...].astype(BF16), wo_ref[i * GROUP_W:(i + 1) * GROUP_W, :],
                            preferred_element_type=F32)
        o_ref[...] = x
        h_ref[...] = _rms(x, g2_ref[...]).astype(BF16)

    h = h_ref[...]
    gate = jnp.dot(h, wg_ref[...], preferred_element_type=F32)
    up = jnp.dot(h, wu_ref[...], preferred_element_type=F32)
    act = (gate * jax.nn.sigmoid(gate) * up).astype(BF16)
    o_ref[...] += jnp.dot(act, wd_ref[...], preferred_element_type=F32)

    if final:
        @pl.when(c == ff_split - 1)
        def _():
            o_ref[...] = _rms(o_ref[...], gf_ref[...])


def _out_ffn(x, ys, p, l, tm, final, ff_split):
    rows = x.shape[0]
    fc = D_FF // ff_split
    row_blk = lambda wd_: pl.BlockSpec((tm, wd_), lambda i, c: (i, 0))
    return pl.pallas_call(
        functools.partial(_out_ffn_kernel, final=final, ff_split=ff_split),
        grid=(rows // tm, ff_split),
        in_specs=[row_blk(D_MODEL)] + [row_blk(GROUP_W)] * 4
        + [_pspec("w_out", l), _pspec("n2", l),
           pl.BlockSpec((None, D_MODEL, fc), lambda i, c: (l, 0, c)),
           pl.BlockSpec((None, D_MODEL, fc), lambda i, c: (l, 0, ff_split + c)),
           pl.BlockSpec((None, fc, D_MODEL), lambda i, c: (l, c, 0)),
           _full((1, D_MODEL))],
        out_specs=row_blk(D_MODEL),
        out_shape=jax.ShapeDtypeStruct((rows, D_MODEL), F32),
        scratch_shapes=[pltpu.VMEM((tm, D_MODEL), BF16)],
        compiler_params=_cparams(("parallel", "arbitrary")),
    )(x, *ys, p["w_out"], p["n2"], p["w_gu"], p["w_gu"], p["w_down"], p["gf"])


def _pad_lora(w, start):
    return jnp.pad(w.astype(BF16), ((0, 0), (start, LORA_PAD - start - w.shape[1]), (0, 0)))


def kernel(x_prompt, x_sample, state_wkv, state_shift, state_ssm_re, state_ssm_im, state_conv,
           norm1_g, w_in, gm_ln_g, gm_ln_b, gm_ws, gm_bs,
           ssm_a_re, ssm_a_im, ssm_log_dt, ssm_b_re, ssm_b_im, ssm_c_re, ssm_c_im, ssm_d,
           ssm_glu_w, ssm_glu_b, conv_w, conv_b,
           tm_mu, tm_w0, tm_w2, tm_a0, tm_a2, tm_g2, tm_k_k, tm_k_a, tm_r_k, tm_ln_g, tm_ln_b,
           w_out, norm2_g, ffn_w_gu, ffn_w_down, norm_f_g):
    depth = w_in.shape[0]
    bp, seq, _ = x_prompt.shape
    bs = x_sample.shape[0]
    assert x_sample.shape[1] == 1 and seq % GM_CHUNK == 0 and seq % WKV_CHUNK == 0
    head_d = GROUP_W // GM_HEADS

    lam, bb, cc = _s5_prep(ssm_a_re, ssm_a_im, ssm_log_dt, ssm_b_re, ssm_b_im, ssm_c_re, ssm_c_im)
    vec256 = {
        "lng": gm_ln_g, "lnb": gm_ln_b,
        "w00": jnp.repeat(gm_ws[:, :, 0, 0], head_d, axis=1), "b0": jnp.repeat(gm_bs[:, :, 0], head_d, axis=1),
        "cb": conv_b, "ssm_d": ssm_d.reshape(depth, GROUP_W), "glu_b": ssm_glu_b,
        "w0": tm_w0, "a0": tm_a0, "k_k": tm_k_k, "k_a": tm_k_a, "r_k": tm_r_k.reshape(depth, GROUP_W),
        "ln_g": tm_ln_g, "ln_b": tm_ln_b,
    }
    vec1024 = {"n1": norm1_g, "n2": norm2_g}
    p = {
        "w_in": w_in.astype(BF16),
        "wcat": jnp.transpose(gm_ws, (0, 2, 1, 3)).reshape(depth, GM_CHUNK, GM_HEADS * GM_CHUNK),
        "bias": jnp.repeat(jnp.transpose(gm_bs, (0, 2, 1)), head_d, axis=2),
        "lam": lam, "bb": bb, "cc": cc, "glu_w": ssm_glu_w.astype(BF16), "cw": conv_w,
        "mu": tm_mu.reshape(depth, 1, D_TM),
        "w2": _pad_lora(tm_w2, 0), "a2": _pad_lora(tm_a2, 32), "g2": _pad_lora(tm_g2, 64),
        "w_out": w_out.astype(BF16), "w_gu": ffn_w_gu.astype(BF16), "w_down": ffn_w_down.astype(BF16),
        "gf": norm_f_g.reshape(1, D_MODEL),
    }
    for names, vecs in ((_VEC256, vec256), (_VEC1024, vec1024)):
        stacked = jnp.stack([vecs[n] for n in names], axis=1)[:, :, None, :]
        p.update({n: stacked for n in names})

    xs = x_sample.reshape(bs, D_MODEL)
    xp = x_prompt
    ssm_s_in = jnp.concatenate([state_ssm_re.reshape(depth, bs, SSM_S),
                                state_ssm_im.reshape(depth, bs, SSM_S)], axis=-1)
    conv_s_in = state_conv.reshape(depth, bs, 2 * GROUP_W)

    outs = {k: [] for k in ("wkv_p", "wkv_s", "sh_p", "sh_s", "ssm_p", "ssm_s", "cv_p", "cv_s", "chv")}
    tm_p = ROW_TILE if seq % ROW_TILE == 0 else GM_CHUNK
    bt_s = SAMPLE_TILE if bs % SAMPLE_TILE == 0 else bs
    for l in range(depth):
        final = l == depth - 1

        ya, zb, yc, zd, tail = _inproj_mix(xp, p, l, bp, seq, tm_p)
        yb, ssm_fin = _s5(zb, p, l, bp, seq)
        xp, wkv_fin = _rwkv_ffn(zd, xp, ya, yb, yc, p, l, bp, seq, final)
        outs["wkv_p"].append(wkv_fin)
        outs["sh_p"].append(zd[:, -1])
        outs["ssm_p"].append(ssm_fin)
        outs["cv_p"].append(tail[:, 6:8])

        za, zb, zc, zd = _inproj(xs, p, l)
        ya, yb, yc, yd, vn, wkv_new, ssm_new, conv_new = _sample_mix(
            za, zb, zc, zd, state_shift, state_wkv, ssm_s_in, conv_s_in, p, l, bt_s)
        xs = _out_ffn(xs, (ya, yb, yc, yd), p, l, bs, final, FF_SPLIT_SAMPLE)
        outs["wkv_s"].append(wkv_new)
        outs["sh_s"].append(zd)
        outs["ssm_s"].append(ssm_new)
        outs["cv_s"].append(conv_new.reshape(bs, 2, GROUP_W))
        outs["chv"].append(vn.reshape(bs, 1, GROUP_W))

    def wkv_blocks(s_bd):
        s5d = s_bd.reshape(depth, -1, WKV_HEADS, WKV_N, WKV_HEADS, WKV_N)
        return jnp.stack([s5d[:, :, h, :, h, :] for h in range(WKV_HEADS)], axis=2)

    wkv_p = wkv_blocks(jnp.stack(outs["wkv_p"]))
    ssm_p = jnp.stack(outs["ssm_p"])
    ssm_s = jnp.stack(outs["ssm_s"])
    split = lambda s, i: s[..., i * SSM_S:(i + 1) * SSM_S].reshape(depth, -1, SSM_GROUPS, SSM_P)
    return (xp, xs.reshape(bs, 1, D_MODEL),
            wkv_p, jnp.stack(outs["wkv_s"]),
            jnp.stack(outs["sh_p"]), jnp.stack(outs["sh_s"]),
            split(ssm_p, 0), split(ssm_s, 0), split(ssm_p, 1), split(ssm_s, 1),
            jnp.stack(outs["cv_p"]), jnp.stack(outs["cv_s"]),
            jnp.stack(outs["chv"]))
```

```python
import functools

import jax
import jax.numpy as jnp
from jax import lax
from jax.experimental import pallas as pl
from jax.experimental.pallas import tpu as pltpu

F32 = jnp.float32
BF16 = jnp.bfloat16

D_MODEL = 1024
GROUP_W = 256
GM_CHUNK = 128
GM_HEADS = 4
SSM_CH = 16
SSM_GROUPS = 16
SSM_P = 64
SSM_S = SSM_GROUPS * SSM_P
WKV_N = 64
WKV_HEADS = 4
LORA_PAD = 128
D_TM = 3 * GROUP_W + LORA_PAD
IN_COLS = 6 * GROUP_W + D_TM
D_FF = 2816
FF_SPLIT_SAMPLE = 11
NORM_EPS = 1e-6
GM_LN_EPS = 1e-5
WKV_LN_EPS = 64e-5

WKV_CHUNK = 64
S5_TCHUNK = 128
ROW_TILE = 512
SAMPLE_TILE = 32
VMEM_LIMIT = 56 * 1024 * 1024


def _cparams(sem):
    return pltpu.CompilerParams(dimension_semantics=sem, vmem_limit_bytes=VMEM_LIMIT)


def _full(shape):
    n = len(shape)
    return pl.BlockSpec(shape, lambda *_: (0,) * n)


def _layer(shape, l):
    n = len(shape)
    return pl.BlockSpec((None,) + tuple(shape), lambda *_: (l,) + (0,) * n)


_VEC256 = ("lng", "lnb", "w00", "b0", "cb", "ssm_d", "glu_b", "w0", "a0", "k_k", "k_a", "r_k", "ln_g", "ln_b")
_VEC1024 = ("n1", "n2")
_MAT_SHAPES = {
    "w_in": (D_MODEL, IN_COLS), "wcat": (GM_CHUNK, GM_HEADS * GM_CHUNK), "bias": (GM_CHUNK, GROUP_W),
    "cw": (3, GROUP_W), "lam": (2, SSM_S), "bb": (GROUP_W, 2 * SSM_S), "cc": (2 * SSM_S, GROUP_W),
    "bb_tiles": (2 * SSM_S // GROUP_W, GROUP_W, GROUP_W),
    "glu_w": (GROUP_W, GROUP_W), "mu": (1, D_TM), "w2": (LORA_PAD, GROUP_W), "a2": (LORA_PAD, GROUP_W),
    "g2": (LORA_PAD, GROUP_W), "w_out": (D_MODEL, D_MODEL),
}


def _pspec(name, l):
    for names, width in ((_VEC256, GROUP_W), (_VEC1024, D_MODEL)):
        if name in names:
            row = names.index(name)
            return pl.BlockSpec((None, None, 1, width), lambda *_: (l, row, 0, 0))
    return _layer(_MAT_SHAPES[name], l)


def _pspecs(names, l):
    return [_pspec(n, l) for n in names]


def _dot(a, b):
    return jnp.dot(a.astype(BF16), b.astype(BF16), preferred_element_type=F32)


def _dot_nt(a, b):
    return lax.dot_general(a.astype(BF16), b.astype(BF16), (((1,), (1,)), ((), ())),
                           preferred_element_type=F32)


def _dot_tn(a, b):
    return lax.dot_general(a.astype(BF16), b.astype(BF16), (((0,), (0,)), ((), ())),
                           preferred_element_type=F32)


def _dot_split(x, ones_bf16):
    hi = x.astype(BF16)
    lo = (x - hi.astype(F32)).astype(BF16)
    return (jnp.dot(hi, ones_bf16, preferred_element_type=F32)
            + jnp.dot(lo, ones_bf16, preferred_element_type=F32))


def _rms(x, g):
    return x * lax.rsqrt(jnp.mean(x * x, axis=-1, keepdims=True) + NORM_EPS) * g


def _softplus(y):
    return jnp.maximum(y, 0.0) + jnp.log1p(jnp.exp(-jnp.abs(y)))


def _gm_norm(zav, ln_g, ln_b):
    vf = jax.nn.gelu(zav)
    mu = jnp.mean(vf, axis=-1, keepdims=True)
    var = jnp.mean(jnp.square(vf - mu), axis=-1, keepdims=True)
    return (vf - mu) * lax.rsqrt(var + GM_LN_EPS) * ln_g + ln_b


def _inproj_kernel(x_ref, g_ref, w_ref, za_ref, zb_ref, zc_ref, zd_ref):
    h = _rms(x_ref[...], g_ref[...])
    z = jnp.dot(h.astype(BF16), w_ref[...], preferred_element_type=F32)
    za_ref[...] = z[:, 0:2 * GROUP_W]
    zb_ref[...] = z[:, 2 * GROUP_W:3 * GROUP_W]
    zc_ref[...] = z[:, 3 * GROUP_W:6 * GROUP_W]
    zd_ref[...] = z[:, 6 * GROUP_W:]


def _inproj(x, p, l):
    rows = x.shape[0]
    widths = (2 * GROUP_W, GROUP_W, 3 * GROUP_W, D_TM)
    return pl.pallas_call(
        _inproj_kernel,
        grid=(1,),
        in_specs=[_full((rows, D_MODEL))] + _pspecs(("n1", "w_in"), l),
        out_specs=[_full((rows, wd)) for wd in widths],
        out_shape=[jax.ShapeDtypeStruct((rows, wd), F32) for wd in widths],
        compiler_params=_cparams(("arbitrary",)),
    )(x, p["n1"], p["w_in"])


def _inproj_mix_kernel(x_ref, g_ref, w_ref, lng_ref, lnb_ref, wcat_ref, bias_ref, cw_ref, cb_ref,
                       ya_ref, zb_ref, yc_ref, zd_ref, tail_ref, prev_ref, *, tile):
    @pl.when(pl.program_id(1) == 0)
    def _():
        prev_ref[...] = jnp.zeros_like(prev_ref)

    h = _rms(x_ref[...], g_ref[...])
    z = jnp.dot(h.astype(BF16), w_ref[...], preferred_element_type=F32)
    zb_ref[...] = z[:, 2 * GROUP_W:3 * GROUP_W]
    zd_ref[...] = z[:, 6 * GROUP_W:]

    u = jax.nn.gelu(z[:, :GROUP_W])
    vn = _gm_norm(z[:, GROUP_W:2 * GROUP_W], lng_ref[...], lnb_ref[...])
    kc = GM_HEADS * GM_CHUNK
    t_i = lax.broadcasted_iota(jnp.int32, (GM_CHUNK, kc), 0)
    s_i = lax.broadcasted_iota(jnp.int32, (GM_CHUNK, kc), 1) % GM_CHUNK
    wm = jnp.where(s_i <= t_i, wcat_ref[...], 0.0).astype(BF16)
    r_h = lax.broadcasted_iota(jnp.int32, (kc, GROUP_W), 0) // GM_CHUNK
    c_h = lax.broadcasted_iota(jnp.int32, (kc, GROUP_W), 1) // (GROUP_W // GM_HEADS)
    head_mask = r_h == c_h
    for c in range(tile // GM_CHUNK):
        rows = slice(c * GM_CHUNK, (c + 1) * GM_CHUNK)
        vc = vn[rows].astype(BF16)
        rhs = jnp.where(head_mask, jnp.concatenate([vc] * GM_HEADS, axis=0), jnp.zeros((), BF16))
        s = jnp.dot(wm, rhs, preferred_element_type=F32) + bias_ref[...]
        ya_ref[rows, :] = u[rows] * s

    zz = z[:, 5 * GROUP_W:6 * GROUP_W] * z[:, 3 * GROUP_W:4 * GROUP_W]
    row = lax.broadcasted_iota(jnp.int32, zz.shape, 0)
    prev = prev_ref[...]
    z1 = jnp.where(row == 0, prev[7:8], pltpu.roll(zz, 1, 0))
    z2 = jnp.where(row == 0, prev[6:7], jnp.where(row == 1, prev[7:8], pltpu.roll(zz, 2, 0)))
    cw = cw_ref[...]
    y = cb_ref[...] + cw[0:1] * z2 + cw[1:2] * z1 + cw[2:3] * zz
    yc_ref[...] = z[:, 4 * GROUP_W:5 * GROUP_W] * y
    prev_ref[...] = zz[tile - 8:]
    tail_ref[...] = zz[tile - 8:]


def _inproj_mix(x, p, l, batch, seq, tile):
    nt = seq // tile
    row_blk = lambda wd: pl.BlockSpec((None, tile, wd), lambda b, j: (b, j, 0))
    widths = (GROUP_W, GROUP_W, GROUP_W, D_TM)
    names = ("n1", "w_in", "lng", "lnb", "wcat", "bias", "cw", "cb")
    return pl.pallas_call(
        functools.partial(_inproj_mix_kernel, tile=tile),
        grid=(batch, nt),
        in_specs=[row_blk(D_MODEL)] + _pspecs(names, l),
        out_specs=[row_blk(wd) for wd in widths]
        + [pl.BlockSpec((None, 8, GROUP_W), lambda b, j: (b, 0, 0))],
        out_shape=[jax.ShapeDtypeStruct((batch, seq, wd), F32) for wd in widths]
        + [jax.ShapeDtypeStruct((batch, 8, GROUP_W), F32)],
        scratch_shapes=[pltpu.VMEM((8, GROUP_W), F32)],
        compiler_params=_cparams(("parallel", "arbitrary")),
    )(x, *[p[n] for n in names])


def _s5_prep_kernel(are_ref, aim_ref, ldt_ref, bre_ref, bim_ref, cre_ref, cim_ref,
                    lam_ref, bb_ref, cc_ref):
    lam_re = jnp.minimum(are_ref[...], -1e-4)
    lam_im = aim_ref[...]
    dt = jnp.exp(ldt_ref[...])
    mag = jnp.exp(lam_re * dt)
    lb_re = mag * jnp.cos(lam_im * dt)
    lb_im = mag * jnp.sin(lam_im * dt)
    den = lam_re * lam_re + lam_im * lam_im
    f_re = ((lb_re - 1.0) * lam_re + lb_im * lam_im) / den
    f_im = (lb_im * lam_re - (lb_re - 1.0) * lam_im) / den
    lam_ref[0:1, :] = lb_re
    lam_ref[1:2, :] = lb_im
    br, bi = bre_ref[...], bim_ref[...]
    grp_r = lax.broadcasted_iota(jnp.int32, (GROUP_W, SSM_S), 0) // SSM_CH
    grp_c = lax.broadcasted_iota(jnp.int32, (GROUP_W, SSM_S), 1) // SSM_P
    m = grp_r == grp_c
    bb_ref[:, :SSM_S] = jnp.where(m, f_re * br - f_im * bi, 0.0).astype(BF16)
    bb_ref[:, SSM_S:] = jnp.where(m, f_re * bi + f_im * br, 0.0).astype(BF16)
    grp_r2 = lax.broadcasted_iota(jnp.int32, (SSM_S, GROUP_W), 0) // SSM_P
    grp_c2 = lax.broadcasted_iota(jnp.int32, (SSM_S, GROUP_W), 1) // SSM_CH
    m2 = grp_r2 == grp_c2
    cc_ref[:SSM_S, :] = jnp.where(m2, cre_ref[...], 0.0).astype(BF16)
    cc_ref[SSM_S:, :] = jnp.where(m2, -cim_ref[...], 0.0).astype(BF16)


def _s5_prep(a_re, a_im, log_dt, b_re, b_im, c_re, c_im):
    depth = a_re.shape[0]
    flat = lambda p: p.reshape(depth, 1, SSM_S)
    b_exp = lambda b: jnp.tile(jnp.transpose(b, (0, 3, 1, 2)).reshape(depth, SSM_CH, SSM_S),
                               (1, SSM_GROUPS, 1))
    c_exp = lambda c: jnp.tile(jnp.transpose(c, (0, 1, 3, 2)).reshape(depth, SSM_S, SSM_CH),
                               (1, 1, SSM_GROUPS))
    lyr = lambda shape: pl.BlockSpec((None,) + shape, lambda l: (l, 0, 0))
    return pl.pallas_call(
        _s5_prep_kernel,
        grid=(depth,),
        in_specs=[lyr((1, SSM_S))] * 3 + [lyr((GROUP_W, SSM_S))] * 2 + [lyr((SSM_S, GROUP_W))] * 2,
        out_specs=[lyr((2, SSM_S)), lyr((GROUP_W, 2 * SSM_S)), lyr((2 * SSM_S, GROUP_W))],
        out_shape=[jax.ShapeDtypeStruct((depth, 2, SSM_S), F32),
                   jax.ShapeDtypeStruct((depth, GROUP_W, 2 * SSM_S), BF16),
                   jax.ShapeDtypeStruct((depth, 2 * SSM_S, GROUP_W), BF16)],
        compiler_params=_cparams(("arbitrary",)),
    )(flat(a_re), flat(a_im), flat(log_dt), b_exp(b_re), b_exp(b_im), c_exp(c_re), c_exp(c_im))


def _s5_output(st, u, cc, d, glu_w, glu_b):
    y = jnp.dot(st.astype(BF16), cc, preferred_element_type=F32) + d * u
    y = jax.nn.gelu(y)
    return y * jax.nn.sigmoid(jnp.dot(y.astype(BF16), glu_w, preferred_element_type=F32) + glu_b)


_S5_PARAMS = ("lam", "bb", "cc", "ssm_d", "glu_w", "glu_b")


S5_SLABS = 2 * SSM_S // 128
S5_PITCH = S5_TCHUNK + 8


S5_PARTS = 8


def _s5_kernel(u_ref, unext_ref, lam_ref, bbt_ref, cc_ref, d_ref, gw_ref, gb_ref, y_ref, fin_ref,
               bu_ref, st_ref, *, batch, tsteps):
    half = S5_SLABS // 2
    per = S5_SLABS // S5_PARTS
    tper = tsteps // S5_PARTS
    g = pl.program_id(0)
    cur = lax.rem(g, 2)
    nxt = 1 - cur
    rows_of = lambda b: slice(b * S5_PITCH, b * S5_PITCH + tsteps)

    def fill(slot, u_bf16, part):
        bu = jnp.dot(u_bf16, bbt_ref[part], preferred_element_type=F32)
        for k in range(per):
            for b in range(batch):
                bu_ref[slot, part * per + k, rows_of(b), :] = bu[b * tsteps:(b + 1) * tsteps,
                                                                   k * 128:(k + 1) * 128]

    u = u_ref[...].reshape(batch * tsteps, GROUP_W)

    @pl.when(g == 0)
    def _():
        st_ref[...] = jnp.zeros_like(st_ref)
        for part in range(S5_PARTS):
            fill(0, u.astype(BF16), part)

    u_next = unext_ref[...].reshape(batch * tsteps, GROUP_W).astype(BF16)
    lam = lam_ref[...]
    lr = [jnp.broadcast_to(lam[0:1, s * 128:(s + 1) * 128], (batch, 128)) for s in range(half)]
    li = [jnp.broadcast_to(lam[1:2, s * 128:(s + 1) * 128], (batch, 128)) for s in range(half)]

    def trip(i, carry):
        fill(nxt, u_next, i)
        for tt in range(tper):
            rows = pl.ds(i * tper + tt, batch, stride=S5_PITCH)
            new = [None] * S5_SLABS
            for s in range(half):
                s_re, s_im = carry[s], carry[half + s]
                new[s] = lr[s] * s_re - li[s] * s_im + bu_ref[cur, s, rows, :]
                new[half + s] = lr[s] * s_im + li[s] * s_re + bu_ref[cur, half + s, rows, :]
                bu_ref[cur, s, rows, :] = new[s]
                bu_ref[cur, half + s, rows, :] = new[half + s]
            carry = tuple(new)
        return carry

    st0 = st_ref[...]
    fin = lax.fori_loop(0, S5_PARTS, trip, tuple(st0[:, s * 128:(s + 1) * 128] for s in range(S5_SLABS)))
    for s in range(S5_SLABS):
        st_ref[:, s * 128:(s + 1) * 128] = fin[s]
        fin_ref[:, s * 128:(s + 1) * 128] = fin[s]
    st = jnp.concatenate(
        [jnp.concatenate([bu_ref[cur, s, rows_of(b), :] for s in range(S5_SLABS)], axis=1)
         for b in range(batch)], axis=0)
    y = _s5_output(st, u, cc_ref[...], d_ref[...], gw_ref[...], gb_ref[...])
    y_ref[...] = y.reshape(batch, tsteps, GROUP_W)


def _s5(u, p, l, batch, seq):
    tsteps = min(S5_TCHUNK, seq)
    nt = seq // tsteps
    blk = lambda tile: pl.BlockSpec((batch, tsteps, GROUP_W), lambda i: (0, tile(i), 0))
    names = ("lam", "bb_tiles", "cc", "ssm_d", "glu_w", "glu_b")
    return pl.pallas_call(
        functools.partial(_s5_kernel, batch=batch, tsteps=tsteps),
        grid=(nt,),
        in_specs=[blk(lambda i: i), blk(lambda i: jnp.minimum(i + 1, nt - 1))] + _pspecs(names, l),
        out_specs=[blk(lambda i: i), _full((batch, 2 * SSM_S))],
        out_shape=[jax.ShapeDtypeStruct((batch, seq, GROUP_W), F32),
                   jax.ShapeDtypeStruct((batch, 2 * SSM_S), F32)],
        scratch_shapes=[pltpu.VMEM((2, S5_SLABS, batch * S5_PITCH, 128), F32),
                        pltpu.VMEM((batch, 2 * SSM_S), F32)],
        compiler_params=_cparams(("arbitrary",)),
    )(u, u, *[p[n] for n in names])


def _wkv_inputs(zd, zprev, p, bd_ones):
    zs = zd + p["mu"] * (zprev - zd)
    r = zs[:, 0:GROUP_W]
    k = zs[:, GROUP_W:2 * GROUP_W]
    v = zs[:, 2 * GROUP_W:3 * GROUP_W]
    lora = zs[:, 3 * GROUP_W:]
    w = -_softplus(-(p["w0"] + _dot(jnp.tanh(lora), p["w2"]))) - 0.5
    logd = -jnp.exp(w)
    a = jax.nn.sigmoid(p["a0"] + _dot(lora, p["a2"]))
    g = _dot(jax.nn.sigmoid(lora), p["g2"])
    kk = k * p["k_k"]
    nrm = jnp.sqrt(_dot_split(kk * kk, bd_ones))
    kk = kk / jnp.maximum(nrm, 1e-12)
    k2 = k * (1.0 + (a - 1.0) * p["k_a"])
    return r, logd, k2, v, kk, a, g


def _wkv_output(o, r, k2, v, g, p, bd_ones):
    inv_n = 1.0 / WKV_N
    m = _dot_split(o, bd_ones) * inv_n
    var = _dot_split(jnp.square(o - m), bd_ones) * inv_n
    on = (o - m) * lax.rsqrt(var + WKV_LN_EPS) * p["ln_g"] + p["ln_b"]
    bonus = _dot_split(r * k2 * p["r_k"], bd_ones) * v
    return (on + bonus) * g


_WKV_PARAMS = ("mu", "w0", "w2", "a0", "a2", "g2", "k_k", "k_a", "r_k", "ln_g", "ln_b")


def _bd_mask(n):
    hr = lax.broadcasted_iota(jnp.int32, (n, n), 0) // (n // WKV_HEADS)
    hc = lax.broadcasted_iota(jnp.int32, (n, n), 1) // (n // WKV_HEADS)
    return hr == hc


def _expand(xp, lo_bf16):
    xb = xp.astype(BF16)
    zero = jnp.zeros_like(lo_bf16)
    hi_bf16 = 1 - lo_bf16
    t0, t1 = xb[:, :128], xb[:, 128:]
    return jnp.concatenate([jnp.concatenate([t0 * lo_bf16, zero], axis=1),
                            jnp.concatenate([t0 * hi_bf16, zero], axis=1),
                            jnp.concatenate([zero, t1 * lo_bf16], axis=1),
                            jnp.concatenate([zero, t1 * hi_bf16], axis=1)], axis=0)


def _interleave(*staged):
    gens = [g for g, _ in staged]
    total = [n for _, n in staged]
    done = [0] * len(gens)
    live = set(range(len(gens)))
    while live:
        i = min(live, key=lambda k: (done[k] / total[k], k))
        try:
            next(gens[i])
            done[i] += 1
        except StopIteration:
            live.remove(i)


def _wkv_chunk_stages(r, logd, k2, v, kk, a, states, bd256):
    c = WKV_CHUNK
    n = len(states)
    rows = [slice(i * c, (i + 1) * c) for i in range(n)]
    t_i = lax.broadcasted_iota(jnp.int32, (c, GROUP_W), 0)
    s_i = lax.broadcasted_iota(jnp.int32, (c, GROUP_W), 1) % c
    strict = s_i < t_i
    incl = s_i <= t_i
    lo_bf16 = jnp.where(lax.broadcasted_iota(jnp.int32, (c, 128), 1) < WKV_N, 1.0, 0.0).astype(BF16)
    ex = lambda xp: _expand(xp, lo_bf16)

    pos = lax.broadcasted_iota(jnp.int32, logd.shape, 0) % c
    cum_all = logd
    step = 1
    while step < c:
        cum_all = cum_all + jnp.where(pos >= step, pltpu.roll(cum_all, step, 0), 0.0)
        step *= 2
    cum = [cum_all[rw] for rw in rows]
    yield
    g_last = [cm[c - 1:c] for cm in cum]
    a_t, r_t, b_t, k_t, b_h, k_h, v_c = [], [], [], [], [], [], []
    for i, rw in enumerate(rows):
        e_neg = jnp.exp(-cum[i])
        e_end = jnp.exp(g_last[i] - cum[i])
        bvec = kk[rw] * a[rw]
        a_t.append(-kk[rw] * jnp.exp(cum[i] - logd[rw]))
        r_t.append(r[rw] * jnp.exp(cum[i]))
        b_t.append(bvec * e_neg)
        k_t.append(k2[rw] * e_neg)
        b_h.append(bvec * e_end)
        k_h.append(k2[rw] * e_end)
        v_c.append(v[rw])
        if i % 2 == 1:
            yield

    ar = [jnp.concatenate([a_t[i], r_t[i]], axis=0) for i in range(n)]
    p_b = [_dot_nt(ar[i], ex(b_t[i])) for i in range(n)]
    yield
    p_k = [_dot_nt(ar[i], ex(k_t[i])) for i in range(n)]
    yield
    l_p = [jnp.where(strict, p[:c], 0.0) for p in p_b]
    aak = [jnp.where(strict, p[:c], 0.0) for p in p_k]
    rb = [jnp.where(incl, p[c:], 0.0) for p in p_b]
    rk = [jnp.where(incl, p[c:], 0.0) for p in p_k]

    def off_diag(m):
        return (t_i // (2 * m) == s_i // (2 * m)) & (t_i % (2 * m) >= m) & (s_i % (2 * m) < m)

    eye_p = jnp.where(s_i == t_i, 1.0, 0.0)
    t_p = [eye_p + jnp.where(off_diag(1), l_p[i], 0.0) for i in range(n)]
    m = 2
    while m < c:
        off = off_diag(m)
        x = [_dot(t_p[i], ex(jnp.where(off, l_p[i], 0.0))) for i in range(n)]
        yield
        t_p = [t_p[i] + _dot(x[i], ex(t_p[i])) for i in range(n)]
        yield
        m *= 2

    v_bd = [ex(v_c[i]) for i in range(n)]
    a2 = [_dot(t_p[i], ex(a_t[i])) for i in range(n)]
    av = [_dot(aak[i], v_bd[i]) for i in range(n)]
    yield
    w0 = [_dot(t_p[i], ex(av[i])) for i in range(n)]
    from_s = [_dot_nt(jnp.concatenate([a2[i], r_t[i]], axis=0), states[i]) for i in range(n)]
    yield
    w = [from_s[i][:c] + w0[i] for i in range(n)]
    o = [from_s[i][c:] + _dot(jnp.concatenate([rb[i], rk[i]], axis=1),
                              jnp.concatenate([ex(w[i]), v_bd[i]], axis=0)) for i in range(n)]
    yield
    upd = [_dot_tn(jnp.concatenate([w[i], v_c[i]], axis=0), jnp.concatenate([b_h[i], k_h[i]], axis=0))
           for i in range(n)]
    s_new = [states[i] * jnp.exp(g_last[i]) + jnp.where(bd256, upd[i], 0.0) for i in range(n)]
    return jnp.concatenate(o, axis=0), s_new


FF_CHUNK = 256


def _ffn_stages(x, ys, wo_ref, n2, wgu_ref, wd_ref, acc_ref):
    for i, y in enumerate(ys):
        x = x + jnp.dot(y.astype(BF16), wo_ref[i * GROUP_W:(i + 1) * GROUP_W, :],
                        preferred_element_type=F32)
    acc_ref[...] = x
    yield
    h = _rms(x, n2).astype(BF16)
    for c in range(D_FF // FF_CHUNK):
        gate = jnp.dot(h, wgu_ref[:, c * FF_CHUNK:(c + 1) * FF_CHUNK], preferred_element_type=F32)
        yield
        up = jnp.dot(h, wgu_ref[:, D_FF + c * FF_CHUNK:D_FF + (c + 1) * FF_CHUNK],
                     preferred_element_type=F32)
        yield
        act = (gate * jax.nn.sigmoid(gate) * up).astype(BF16)
        acc_ref[...] += jnp.dot(act, wd_ref[c * FF_CHUNK:(c + 1) * FF_CHUNK, :], preferred_element_type=F32)
        yield


def _rwkv_ffn_kernel(zd_ref, x_ref, ya_ref, yb_ref, yc_ref, wo_ref, n2_ref, gf_ref, wgu_hbm, wd_hbm,
                     *rest, nb, nt, l, final):
    prm = {n: ref[...] for n, ref in zip(_WKV_PARAMS, rest)}
    o_ref, sfin_ref, prev_ref, s_ref, yd_ref, acc_ref, wgu_ref, wd_ref, sem = rest[len(_WKV_PARAMS):]
    c = WKV_CHUNK
    j = pl.program_id(0)
    n_rwkv = 2 * (WKV_CHUNK.bit_length() - 2) + nb // 2 + 11
    n_ffn = 3 * (D_FF // FF_CHUNK) + 2

    def weight_copies():
        return (pltpu.make_async_copy(wgu_hbm.at[l], wgu_ref, sem.at[0]),
                pltpu.make_async_copy(wd_hbm.at[l], wd_ref, sem.at[1]))

    def rwkv_stages():
        bd256 = _bd_mask(GROUP_W)
        bd_ones = jnp.where(bd256, 1.0, 0.0).astype(BF16)
        zd3 = zd_ref[...]
        zd = zd3.reshape(nb * c, D_TM)
        first = lax.broadcasted_iota(jnp.int32, (nb, c, D_TM), 1) == 0
        carried = jnp.broadcast_to(prev_ref[:, 7:8, :], (nb, c, D_TM))
        zprev = jnp.where(first, carried, pltpu.roll(zd, 1, 0).reshape(nb, c, D_TM)).reshape(nb * c, D_TM)
        halves = (slice(0, nb * c // 2), slice(nb * c // 2, nb * c))
        parts = []
        for rows in halves:
            parts.append(_wkv_inputs(zd[rows], zprev[rows], prm, bd_ones))
            yield
        r, logd, k2, v, kk, a, g = (jnp.concatenate(xs, axis=0) for xs in zip(*parts))
        o, s_new = yield from _wkv_chunk_stages(r, logd, k2, v, kk, a, [s_ref[b] for b in range(nb)],
                                                bd256)
        for rows in halves:
            yield
            yd_ref[rows, :] = _wkv_output(o[rows], r[rows], k2[rows], v[rows], g[rows], prm, bd_ones)
        for b in range(nb):
            s_ref[b] = s_new[b]
            sfin_ref[b] = s_new[b]
        prev_ref[...] = zd3[:, c - 8:, :]

    def ffn_stages():
        flat = lambda ref: ref[...].reshape(nb * c, ref.shape[-1])
        ys = (flat(ya_ref), flat(yb_ref), flat(yc_ref), yd_ref[...])
        yield from _ffn_stages(flat(x_ref), ys, wo_ref, n2_ref[...], wgu_ref, wd_ref, acc_ref)
        x = acc_ref[...]
        if final:
            x = _rms(x, gf_ref[...])
        o_ref[...] = x.reshape(nb, c, D_MODEL)

    @pl.when(j == 0)
    def _():
        prev_ref[...] = jnp.zeros_like(prev_ref)
        s_ref[...] = jnp.zeros_like(s_ref)
        for cp in weight_copies():
            cp.start()
        _interleave((rwkv_stages(), n_rwkv))
        for cp in weight_copies():
            cp.wait()

    @pl.when((j > 0) & (j < nt))
    def _():
        _interleave((ffn_stages(), n_ffn), (rwkv_stages(), n_rwkv))

    @pl.when(j == nt)
    def _():
        _interleave((ffn_stages(), n_ffn))


def _rwkv_ffn(zd, x, ya, yb, yc, p, l, batch, seq, final):
    c = WKV_CHUNK
    nt = seq // c
    cur = lambda wd: pl.BlockSpec((batch, c, wd), lambda j: (0, jnp.minimum(j, nt - 1), 0))
    prv = lambda wd: pl.BlockSpec((batch, c, wd), lambda j: (0, jnp.maximum(j - 1, 0), 0))
    return pl.pallas_call(
        functools.partial(_rwkv_ffn_kernel, nb=batch, nt=nt, l=l, final=final),
        grid=(nt + 1,),
        in_specs=[cur(D_TM), prv(D_MODEL), prv(GROUP_W), prv(GROUP_W), prv(GROUP_W),
                  _pspec("w_out", l), _pspec("n2", l), _full((1, D_MODEL)),
                  pl.BlockSpec(memory_space=pl.ANY), pl.BlockSpec(memory_space=pl.ANY)]
        + _pspecs(_WKV_PARAMS, l),
        out_specs=[prv(D_MODEL), _full((batch, GROUP_W, GROUP_W))],
        out_shape=[jax.ShapeDtypeStruct((batch, seq, D_MODEL), F32),
                   jax.ShapeDtypeStruct((batch, GROUP_W, GROUP_W), F32)],
        scratch_shapes=[pltpu.VMEM((batch, 8, D_TM), F32),
                        pltpu.VMEM((batch, GROUP_W, GROUP_W), F32),
                        pltpu.VMEM((batch * c, GROUP_W), F32),
                        pltpu.VMEM((batch * c, D_MODEL), F32),
                        pltpu.VMEM((D_MODEL, 2 * D_FF), BF16),
                        pltpu.VMEM((D_FF, D_MODEL), BF16),
                        pltpu.SemaphoreType.DMA((2,))],
        compiler_params=_cparams(("arbitrary",)),
    )(zd, x, ya, yb, yc, p["w_out"], p["n2"], p["gf"], p["w_gu"], p["w_down"],
      *[p[n] for n in _WKV_PARAMS])


def _sample_mix_kernel(za_ref, zb_ref, zc_ref, zd_ref, shift_ref, s_ref, ssm_ref, conv_ref,
                       lng_ref, lnb_ref, w00_ref, b0_ref, lam_ref, bb_ref, cc_ref, d_ref, gw_ref,
                       gb_ref, cw_ref, cb_ref, *rest):
    prm = {n: ref[...] for n, ref in zip(_WKV_PARAMS, rest)}
    ya_ref, yb_ref, yc_ref, yd_ref, vn_ref, snew_ref, ssmnew_ref, convnew_ref = rest[len(_WKV_PARAMS):]

    za = za_ref[...]
    vn = _gm_norm(za[:, GROUP_W:], lng_ref[...], lnb_ref[...])
    vn_ref[...] = vn
    ya_ref[...] = jax.nn.gelu(za[:, :GROUP_W]) * (w00_ref[...] * vn + b0_ref[...])

    u = zb_ref[...]
    bu = jnp.dot(u.astype(BF16), bb_ref[...], preferred_element_type=F32)
    lam = lam_ref[...]
    lr, li = lam[0:1], lam[1:2]
    st = ssm_ref[...]
    s_re, s_im = st[:, :SSM_S], st[:, SSM_S:]
    st_new = jnp.concatenate([lr * s_re - li * s_im + bu[:, :SSM_S],
                              lr * s_im + li * s_re + bu[:, SSM_S:]], axis=1)
    ssmnew_ref[...] = st_new
    yb_ref[...] = _s5_output(st_new, u, cc_ref[...], d_ref[...], gw_ref[...], gb_ref[...])

    zc = zc_ref[...]
    z = zc[:, 2 * GROUP_W:] * zc[:, :GROUP_W]
    cw = cw_ref[...]
    buf = conv_ref[...]
    y = cb_ref[...] + cw[0:1] * buf[:, :GROUP_W] + cw[1:2] * buf[:, GROUP_W:] + cw[2:3] * z
    yc_ref[...] = zc[:, GROUP_W:2 * GROUP_W] * y
    convnew_ref[:, :GROUP_W] = buf[:, GROUP_W:]
    convnew_ref[:, GROUP_W:] = z

    bd256 = _bd_mask(GROUP_W)
    bd_ones = jnp.where(bd256, 1.0, 0.0).astype(BF16)
    zd = zd_ref[...]
    r, logd, k2, v, kk, a, g = _wkv_inputs(zd, shift_ref[...], prm, bd_ones)
    bt = zd.shape[0]
    s4 = s_ref[...]
    s = jnp.concatenate([s4[:, h] for h in range(WKV_HEADS)], axis=-1)
    eye4 = (lax.broadcasted_iota(jnp.int32, (WKV_N, GROUP_W), 0)
            == lax.broadcasted_iota(jnp.int32, (WKV_N, GROUP_W), 1) % WKV_N)

    def head_sum(x3):
        return _dot_split(x3.reshape(bt * WKV_N, GROUP_W), bd_ones).reshape(bt, WKV_N, GROUP_W)

    sa = head_sum(s * (-kk)[:, None, :])
    vcol = head_sum(jnp.where(eye4[None], v[:, None, :], 0.0))
    s_new = (s * jnp.exp(logd)[:, None, :] + sa * (kk * a)[:, None, :] + vcol * k2[:, None, :])
    for h in range(WKV_HEADS):
        snew_ref[:, h] = s_new[:, :, h * WKV_N:(h + 1) * WKV_N]
    o_rep = head_sum(s_new * r[:, None, :])
    o = jnp.sum(jnp.where(eye4[None], o_rep, 0.0), axis=1)
    yd_ref[...] = _wkv_output(o, r, k2, v, g, prm, bd_ones)


def _sample_mix(za, zb, zc, zd, shift, wkv, ssm, conv, p, l, bt):
    rows = za.shape[0]
    row_blk = lambda wd: pl.BlockSpec((bt, wd), lambda i: (i, 0))
    st_blk = lambda wd: pl.BlockSpec((None, bt, wd), lambda i: (l, i, 0))
    s_dims = (WKV_HEADS, WKV_N, WKV_N)
    names = ("lng", "lnb", "w00", "b0") + _S5_PARAMS + ("cw", "cb") + _WKV_PARAMS
    return pl.pallas_call(
        _sample_mix_kernel,
        grid=(rows // bt,),
        in_specs=[row_blk(2 * GROUP_W), row_blk(GROUP_W), row_blk(3 * GROUP_W), row_blk(D_TM),
                  st_blk(D_TM), pl.BlockSpec((None, bt) + s_dims, lambda i: (l, i, 0, 0, 0)),
                  st_blk(2 * SSM_S), st_blk(2 * GROUP_W)] + _pspecs(names, l),
        out_specs=[row_blk(GROUP_W)] * 5
        + [pl.BlockSpec((bt,) + s_dims, lambda i: (i, 0, 0, 0)), row_blk(2 * SSM_S), row_blk(2 * GROUP_W)],
        out_shape=[jax.ShapeDtypeStruct((rows, GROUP_W), F32)] * 5
        + [jax.ShapeDtypeStruct((rows,) + s_dims, F32),
           jax.ShapeDtypeStruct((rows, 2 * SSM_S), F32),
           jax.ShapeDtypeStruct((rows, 2 * GROUP_W), F32)],
        compiler_params=_cparams(("parallel",)),
    )(za, zb, zc, zd, shift, wkv, ssm, conv, *[p[n] for n in names])


def _out_ffn_kernel(x_ref, ya_ref, yb_ref, yc_ref, yd_ref, wo_ref, g2_ref, wg_ref, wu_ref, wd_ref,
                    gf_ref, o_ref, h_ref, *, final, ff_split):
    c = pl.program_id(1)

    @pl.when(c == 0)
    def _():
        x = x_ref[...]
        for i, y_ref in enumerate((ya_ref, yb_ref, yc_ref, yd_ref)):
            x = x + jnp.dot(y_ref[...].astype(BF16), wo_ref[i * GROUP_W:(i + 1) * GROUP_W, :],
                            preferred_element_type=F32)
        o_ref[...] = x
        h_ref[...] = _rms(x, g2_ref[...]).astype(BF16)

    h = h_ref[...]
    gate = jnp.dot(h, wg_ref[...], preferred_element_type=F32)
    up = jnp.dot(h, wu_ref[...], preferred_element_type=F32)
    act = (gate * jax.nn.sigmoid(gate) * up).astype(BF16)
    o_ref[...] += jnp.dot(act, wd_ref[...], preferred_element_type=F32)

    if final:
        @pl.when(c == ff_split - 1)
        def _():
            o_ref[...] = _rms(o_ref[...], gf_ref[...])


def _out_ffn(x, ys, p, l, tm, final, ff_split):
    rows = x.shape[0]
    fc = D_FF // ff_split
    row_blk = lambda wd_: pl.BlockSpec((tm, wd_), lambda i, c: (i, 0))
    return pl.pallas_call(
        functools.partial(_out_ffn_kernel, final=final, ff_split=ff_split),
        grid=(rows // tm, ff_split),
        in_specs=[row_blk(D_MODEL)] + [row_blk(GROUP_W)] * 4
        + [_pspec("w_out", l), _pspec("n2", l),
           pl.BlockSpec((None, D_MODEL, fc), lambda i, c: (l, 0, c)),
           pl.BlockSpec((None, D_MODEL, fc), lambda i, c: (l, 0, ff_split + c)),
           pl.BlockSpec((None, fc, D_MODEL), lambda i, c: (l, c, 0)),
           _full((1, D_MODEL))],
        out_specs=row_blk(D_MODEL),
        out_shape=jax.ShapeDtypeStruct((rows, D_MODEL), F32),
        scratch_shapes=[pltpu.VMEM((tm, D_MODEL), BF16)],
        compiler_params=_cparams(("parallel", "arbitrary")),
    )(x, *ys, p["w_out"], p["n2"], p["w_gu"], p["w_gu"], p["w_down"], p["gf"])


def _pad_lora(w, start):
    return jnp.pad(w.astype(BF16), ((0, 0), (start, LORA_PAD - start - w.shape[1]), (0, 0)))


def kernel(x_prompt, x_sample, state_wkv, state_shift, state_ssm_re, state_ssm_im, state_conv,
           norm1_g, w_in, gm_ln_g, gm_ln_b, gm_ws, gm_bs,
           ssm_a_re, ssm_a_im, ssm_log_dt, ssm_b_re, ssm_b_im, ssm_c_re, ssm_c_im, ssm_d,
           ssm_glu_w, ssm_glu_b, conv_w, conv_b,
           tm_mu, tm_w0, tm_w2, tm_a0, tm_a2, tm_g2, tm_k_k, tm_k_a, tm_r_k, tm_ln_g, tm_ln_b,
           w_out, norm2_g, ffn_w_gu, ffn_w_down, norm_f_g):
    depth = w_in.shape[0]
    bp, seq, _ = x_prompt.shape
    bs = x_sample.shape[0]
    assert x_sample.shape[1] == 1 and seq % GM_CHUNK == 0 and seq % WKV_CHUNK == 0
    head_d = GROUP_W // GM_HEADS

    lam, bb, cc = _s5_prep(ssm_a_re, ssm_a_im, ssm_log_dt, ssm_b_re, ssm_b_im, ssm_c_re, ssm_c_im)
    vec256 = {
        "lng": gm_ln_g, "lnb": gm_ln_b,
        "w00": jnp.repeat(gm_ws[:, :, 0, 0], head_d, axis=1), "b0": jnp.repeat(gm_bs[:, :, 0], head_d, axis=1),
        "cb": conv_b, "ssm_d": ssm_d.reshape(depth, GROUP_W), "glu_b": ssm_glu_b,
        "w0": tm_w0, "a0": tm_a0, "k_k": tm_k_k, "k_a": tm_k_a, "r_k": tm_r_k.reshape(depth, GROUP_W),
        "ln_g": tm_ln_g, "ln_b": tm_ln_b,
    }
    vec1024 = {"n1": norm1_g, "n2": norm2_g}
    p = {
        "w_in": w_in.astype(BF16),
        "wcat": jnp.transpose(gm_ws, (0, 2, 1, 3)).reshape(depth, GM_CHUNK, GM_HEADS * GM_CHUNK),
        "bias": jnp.repeat(jnp.transpose(gm_bs, (0, 2, 1)), head_d, axis=2),
        "lam": lam, "bb": bb, "cc": cc, "glu_w": ssm_glu_w.astype(BF16), "cw": conv_w,
        "bb_tiles": jnp.transpose(bb.reshape(depth, GROUP_W, S5_PARTS, GROUP_W), (0, 2, 1, 3)),
        "mu": tm_mu.reshape(depth, 1, D_TM),
        "w2": _pad_lora(tm_w2, 0), "a2": _pad_lora(tm_a2, 32), "g2": _pad_lora(tm_g2, 64),
        "w_out": w_out.astype(BF16), "w_gu": ffn_w_gu.astype(BF16), "w_down": ffn_w_down.astype(BF16),
        "gf": norm_f_g.reshape(1, D_MODEL),
    }
    for names, vecs in ((_VEC256, vec256), (_VEC1024, vec1024)):
        stacked = jnp.stack([vecs[n] for n in names], axis=1)[:, :, None, :]
        p.update({n: stacked for n in names})

    xs = x_sample.reshape(bs, D_MODEL)
    xp = x_prompt
    ssm_s_in = jnp.concatenate([state_ssm_re.reshape(depth, bs, SSM_S),
                                state_ssm_im.reshape(depth, bs, SSM_S)], axis=-1)
    conv_s_in = state_conv.reshape(depth, bs, 2 * GROUP_W)

    outs = {k: [] for k in ("wkv_p", "wkv_s", "sh_p", "sh_s", "ssm_p", "ssm_s", "cv_p", "cv_s", "chv")}
    tm_p = ROW_TILE if seq % ROW_TILE == 0 else GM_CHUNK
    bt_s = SAMPLE_TILE if bs % SAMPLE_TILE == 0 else bs
    for l in range(depth):
        final = l == depth - 1

        ya, zb, yc, zd, tail = _inproj_mix(xp, p, l, bp, seq, tm_p)
        yb, ssm_fin = _s5(zb, p, l, bp, seq)
        xp, wkv_fin = _rwkv_ffn(zd, xp, ya, yb, yc, p, l, bp, seq, final)
        outs["wkv_p"].append(wkv_fin)
        outs["sh_p"].append(zd[:, -1])
        outs["ssm_p"].append(ssm_fin)
        outs["cv_p"].append(tail[:, 6:8])

        za, zb, zc, zd = _inproj(xs, p, l)
        ya, yb, yc, yd, vn, wkv_new, ssm_new, conv_new = _sample_mix(
            za, zb, zc, zd, state_shift, state_wkv, ssm_s_in, conv_s_in, p, l, bt_s)
        xs = _out_ffn(xs, (ya, yb, yc, yd), p, l, bs, final, FF_SPLIT_SAMPLE)
        outs["wkv_s"].append(wkv_new)
        outs["sh_s"].append(zd)
        outs["ssm_s"].append(ssm_new)
        outs["cv_s"].append(conv_new.reshape(bs, 2, GROUP_W))
        outs["chv"].append(vn.reshape(bs, 1, GROUP_W))

    def wkv_blocks(s_bd):
        s5d = s_bd.reshape(depth, -1, WKV_HEADS, WKV_N, WKV_HEADS, WKV_N)
        return jnp.stack([s5d[:, :, h, :, h, :] for h in range(WKV_HEADS)], axis=2)

    wkv_p = wkv_blocks(jnp.stack(outs["wkv_p"]))
    ssm_p = jnp.stack(outs["ssm_p"])
    ssm_s = jnp.stack(outs["ssm_s"])
    split = lambda s, i: s[..., i * SSM_S:(i + 1) * SSM_S].reshape(depth, -1, SSM_GROUPS, SSM_P)
    return (xp, xs.reshape(bs, 1, D_MODEL),
            wkv_p, jnp.stack(outs["wkv_s"]),
            jnp.stack(outs["sh_p"]), jnp.stack(outs["sh_s"]),
            split(ssm_p, 0), split(ssm_s, 0), split(ssm_p, 1), split(ssm_s, 1),
            jnp.stack(outs["cv_p"]), jnp.stack(outs["cv_s"]),
            jnp.stack(outs["chv"]))
```

```python
import functools

import jax
import jax.numpy as jnp
from jax import lax
from jax.experimental import pallas as pl
from jax.experimental.pallas import tpu as pltpu

F32 = jnp.float32
BF16 = jnp.bfloat16

D_MODEL = 1024
GROUP_W = 256
GM_CHUNK = 128
GM_HEADS = 4
SSM_CH = 16
SSM_GROUPS = 16
SSM_P = 64
SSM_S = SSM_GROUPS * SSM_P
WKV_N = 64
WKV_HEADS = 4
LORA_PAD = 128
D_TM = 3 * GROUP_W + LORA_PAD
IN_COLS = 6 * GROUP_W + D_TM
D_FF = 2816
FF_SPLIT_SAMPLE = 11
NORM_EPS = 1e-6
GM_LN_EPS = 1e-5
WKV_LN_EPS = 64e-5

WKV_CHUNK = 64
S5_TCHUNK = 128
ROW_TILE = 512
SAMPLE_TILE = 32
VMEM_LIMIT = 56 * 1024 * 1024


def _cparams(sem):
    return pltpu.CompilerParams(dimension_semantics=sem, vmem_limit_bytes=VMEM_LIMIT)


def _full(shape):
    n = len(shape)
    return pl.BlockSpec(shape, lambda *_: (0,) * n)


def _layer(shape, l):
    n = len(shape)
    return pl.BlockSpec((None,) + tuple(shape), lambda *_: (l,) + (0,) * n)


def _dot(a, b):
    return jnp.dot(a.astype(BF16), b.astype(BF16), preferred_element_type=F32)


def _dot_nt(a, b):
    return lax.dot_general(a.astype(BF16), b.astype(BF16), (((1,), (1,)), ((), ())),
                           preferred_element_type=F32)


def _dot_tn(a, b):
    return lax.dot_general(a.astype(BF16), b.astype(BF16), (((0,), (0,)), ((), ())),
                           preferred_element_type=F32)


def _dot_split(x, ones_bf16):
    hi = x.astype(BF16)
    lo = (x - hi.astype(F32)).astype(BF16)
    return (jnp.dot(hi, ones_bf16, preferred_element_type=F32)
            + jnp.dot(lo, ones_bf16, preferred_element_type=F32))


def _rms(x, g):
    return x * lax.rsqrt(jnp.mean(x * x, axis=-1, keepdims=True) + NORM_EPS) * g


def _softplus(y):
    return jnp.maximum(y, 0.0) + jnp.log1p(jnp.exp(-jnp.abs(y)))


def _gm_norm(zav, ln_g, ln_b):
    vf = jax.nn.gelu(zav)
    mu = jnp.mean(vf, axis=-1, keepdims=True)
    var = jnp.mean(jnp.square(vf - mu), axis=-1, keepdims=True)
    return (vf - mu) * lax.rsqrt(var + GM_LN_EPS) * ln_g + ln_b


def _inproj_kernel(x_ref, g_ref, w_ref, za_ref, zb_ref, zc_ref, zd_ref):
    h = _rms(x_ref[...], g_ref[...])
    z = jnp.dot(h.astype(BF16), w_ref[...], preferred_element_type=F32)
    za_ref[...] = z[:, 0:2 * GROUP_W]
    zb_ref[...] = z[:, 2 * GROUP_W:3 * GROUP_W]
    zc_ref[...] = z[:, 3 * GROUP_W:6 * GROUP_W]
    zd_ref[...] = z[:, 6 * GROUP_W:]


def _inproj(x, g, w, l):
    rows = x.shape[0]
    widths = (2 * GROUP_W, GROUP_W, 3 * GROUP_W, D_TM)
    return pl.pallas_call(
        _inproj_kernel,
        grid=(1,),
        in_specs=[_full((rows, D_MODEL)), _layer((1, D_MODEL), l), _layer((D_MODEL, IN_COLS), l)],
        out_specs=[_full((rows, wd)) for wd in widths],
        out_shape=[jax.ShapeDtypeStruct((rows, wd), F32) for wd in widths],
        compiler_params=_cparams(("arbitrary",)),
    )(x, g, w)


def _inproj_mix_kernel(x_ref, g_ref, w_ref, lng_ref, lnb_ref, wcat_ref, bias_ref, cw_ref, cb_ref,
                       ya_ref, zb_ref, yc_ref, zd_ref, tail_ref, prev_ref, *, tile):
    @pl.when(pl.program_id(1) == 0)
    def _():
        prev_ref[...] = jnp.zeros_like(prev_ref)

    h = _rms(x_ref[...], g_ref[...])
    z = jnp.dot(h.astype(BF16), w_ref[...], preferred_element_type=F32)
    zb_ref[...] = z[:, 2 * GROUP_W:3 * GROUP_W]
    zd_ref[...] = z[:, 6 * GROUP_W:]

    u = jax.nn.gelu(z[:, :GROUP_W])
    vn = _gm_norm(z[:, GROUP_W:2 * GROUP_W], lng_ref[...], lnb_ref[...])
    kc = GM_HEADS * GM_CHUNK
    t_i = lax.broadcasted_iota(jnp.int32, (GM_CHUNK, kc), 0)
    s_i = lax.broadcasted_iota(jnp.int32, (GM_CHUNK, kc), 1) % GM_CHUNK
    wm = jnp.where(s_i <= t_i, wcat_ref[...], 0.0).astype(BF16)
    r_h = lax.broadcasted_iota(jnp.int32, (kc, GROUP_W), 0) // GM_CHUNK
    c_h = lax.broadcasted_iota(jnp.int32, (kc, GROUP_W), 1) // (GROUP_W // GM_HEADS)
    head_mask = r_h == c_h
    for c in range(tile // GM_CHUNK):
        rows = slice(c * GM_CHUNK, (c + 1) * GM_CHUNK)
        vc = vn[rows].astype(BF16)
        rhs = jnp.where(head_mask, jnp.concatenate([vc] * GM_HEADS, axis=0), jnp.zeros((), BF16))
        s = jnp.dot(wm, rhs, preferred_element_type=F32) + bias_ref[...]
        ya_ref[rows, :] = u[rows] * s

    zz = z[:, 5 * GROUP_W:6 * GROUP_W] * z[:, 3 * GROUP_W:4 * GROUP_W]
    row = lax.broadcasted_iota(jnp.int32, zz.shape, 0)
    prev = prev_ref[...]
    z1 = jnp.where(row == 0, prev[7:8], pltpu.roll(zz, 1, 0))
    z2 = jnp.where(row == 0, prev[6:7], jnp.where(row == 1, prev[7:8], pltpu.roll(zz, 2, 0)))
    cw = cw_ref[...]
    y = cb_ref[...] + cw[0:1] * z2 + cw[1:2] * z1 + cw[2:3] * zz
    yc_ref[...] = z[:, 4 * GROUP_W:5 * GROUP_W] * y
    prev_ref[...] = zz[tile - 8:]
    tail_ref[...] = zz[tile - 8:]


def _inproj_mix(x, p, l, batch, seq, tile):
    nt = seq // tile
    rows = batch * seq
    row_blk = lambda wd: pl.BlockSpec((tile, wd), lambda b, j: (b * nt + j, 0))
    widths = (GROUP_W, GROUP_W, GROUP_W, D_TM)
    return pl.pallas_call(
        functools.partial(_inproj_mix_kernel, tile=tile),
        grid=(batch, nt),
        in_specs=[row_blk(D_MODEL), _layer((1, D_MODEL), l), _layer((D_MODEL, IN_COLS), l),
                  _layer((1, GROUP_W), l), _layer((1, GROUP_W), l),
                  _layer((GM_CHUNK, GM_HEADS * GM_CHUNK), l), _layer((GM_CHUNK, GROUP_W), l),
                  _layer((3, GROUP_W), l), _layer((1, GROUP_W), l)],
        out_specs=[row_blk(wd) for wd in widths]
        + [pl.BlockSpec((None, 8, GROUP_W), lambda b, j: (b, 0, 0))],
        out_shape=[jax.ShapeDtypeStruct((rows, wd), F32) for wd in widths]
        + [jax.ShapeDtypeStruct((batch, 8, GROUP_W), F32)],
        scratch_shapes=[pltpu.VMEM((8, GROUP_W), F32)],
        compiler_params=_cparams(("parallel", "arbitrary")),
    )(x, p["n1"], p["w_in"], p["lng"], p["lnb"], p["wcat"], p["bias"], p["cw"], p["cb"])


def _s5_prep_kernel(are_ref, aim_ref, ldt_ref, bre_ref, bim_ref, cre_ref, cim_ref,
                    lam_ref, bb_ref, cc_ref):
    lam_re = jnp.minimum(are_ref[...], -1e-4)
    lam_im = aim_ref[...]
    dt = jnp.exp(ldt_ref[...])
    mag = jnp.exp(lam_re * dt)
    lb_re = mag * jnp.cos(lam_im * dt)
    lb_im = mag * jnp.sin(lam_im * dt)
    den = lam_re * lam_re + lam_im * lam_im
    f_re = ((lb_re - 1.0) * lam_re + lb_im * lam_im) / den
    f_im = (lb_im * lam_re - (lb_re - 1.0) * lam_im) / den
    lam_ref[0:1, :] = lb_re
    lam_ref[1:2, :] = lb_im
    br, bi = bre_ref[...], bim_ref[...]
    grp_r = lax.broadcasted_iota(jnp.int32, (GROUP_W, SSM_S), 0) // SSM_CH
    grp_c = lax.broadcasted_iota(jnp.int32, (GROUP_W, SSM_S), 1) // SSM_P
    m = grp_r == grp_c
    bb_ref[:, :SSM_S] = jnp.where(m, f_re * br - f_im * bi, 0.0).astype(BF16)
    bb_ref[:, SSM_S:] = jnp.where(m, f_re * bi + f_im * br, 0.0).astype(BF16)
    grp_r2 = lax.broadcasted_iota(jnp.int32, (SSM_S, GROUP_W), 0) // SSM_P
    grp_c2 = lax.broadcasted_iota(jnp.int32, (SSM_S, GROUP_W), 1) // SSM_CH
    m2 = grp_r2 == grp_c2
    cc_ref[:SSM_S, :] = jnp.where(m2, cre_ref[...], 0.0).astype(BF16)
    cc_ref[SSM_S:, :] = jnp.where(m2, -cim_ref[...], 0.0).astype(BF16)


def _s5_prep(a_re, a_im, log_dt, b_re, b_im, c_re, c_im):
    depth = a_re.shape[0]
    flat = lambda p: p.reshape(depth, 1, SSM_S)
    b_exp = lambda b: jnp.tile(jnp.transpose(b, (0, 3, 1, 2)).reshape(depth, SSM_CH, SSM_S),
                               (1, SSM_GROUPS, 1))
    c_exp = lambda c: jnp.tile(jnp.transpose(c, (0, 1, 3, 2)).reshape(depth, SSM_S, SSM_CH),
                               (1, 1, SSM_GROUPS))
    lyr = lambda shape: pl.BlockSpec((None,) + shape, lambda l: (l, 0, 0))
    return pl.pallas_call(
        _s5_prep_kernel,
        grid=(depth,),
        in_specs=[lyr((1, SSM_S))] * 3 + [lyr((GROUP_W, SSM_S))] * 2 + [lyr((SSM_S, GROUP_W))] * 2,
        out_specs=[lyr((2, SSM_S)), lyr((GROUP_W, 2 * SSM_S)), lyr((2 * SSM_S, GROUP_W))],
        out_shape=[jax.ShapeDtypeStruct((depth, 2, SSM_S), F32),
                   jax.ShapeDtypeStruct((depth, GROUP_W, 2 * SSM_S), BF16),
                   jax.ShapeDtypeStruct((depth, 2 * SSM_S, GROUP_W), BF16)],
        compiler_params=_cparams(("arbitrary",)),
    )(flat(a_re), flat(a_im), flat(log_dt), b_exp(b_re), b_exp(b_im), c_exp(c_re), c_exp(c_im))


def _s5_output(st, u, cc, d, glu_w, glu_b):
    y = jnp.dot(st.astype(BF16), cc, preferred_element_type=F32) + d * u
    y = jax.nn.gelu(y)
    return y * jax.nn.sigmoid(jnp.dot(y.astype(BF16), glu_w, preferred_element_type=F32) + glu_b)


_S5_PARAMS = ("lam", "bb", "cc", "ssm_d", "glu_w", "glu_b")
_S5_SHAPES = ((2, SSM_S), (GROUP_W, 2 * SSM_S), (2 * SSM_S, GROUP_W), (1, GROUP_W),
              (GROUP_W, GROUP_W), (1, GROUP_W))


S5_SLABS = 2 * SSM_S // 128
S5_PITCH = S5_TCHUNK + 8


def _s5_kernel(u_ref, lam_ref, bb_ref, cc_ref, d_ref, gw_ref, gb_ref, y_ref, fin_ref,
               bu_ref, st_ref, *, batch, tsteps):
    half = S5_SLABS // 2

    @pl.when(pl.program_id(0) == 0)
    def _():
        st_ref[...] = jnp.zeros_like(st_ref)

    u = u_ref[...].reshape(batch * tsteps, GROUP_W)
    bu = jnp.dot(u.astype(BF16), bb_ref[...], preferred_element_type=F32)
    for b in range(batch):
        for s in range(S5_SLABS):
            bu_ref[s, b * S5_PITCH:b * S5_PITCH + tsteps, :] = (
                bu[b * tsteps:(b + 1) * tsteps, s * 128:(s + 1) * 128])
    lam = lam_ref[...]
    lr = [jnp.broadcast_to(lam[0:1, s * 128:(s + 1) * 128], (batch, 128)) for s in range(half)]
    li = [jnp.broadcast_to(lam[1:2, s * 128:(s + 1) * 128], (batch, 128)) for s in range(half)]

    def step(t, carry):
        rows = pl.ds(t, batch, stride=S5_PITCH)
        new = [None] * S5_SLABS
        for s in range(half):
            s_re, s_im = carry[s], carry[half + s]
            new[s] = lr[s] * s_re - li[s] * s_im + bu_ref[s, rows, :]
            new[half + s] = lr[s] * s_im + li[s] * s_re + bu_ref[half + s, rows, :]
            bu_ref[s, rows, :] = new[s]
            bu_ref[half + s, rows, :] = new[half + s]
        return tuple(new)

    st0 = st_ref[...]
    fin = lax.fori_loop(0, tsteps, step,
                        tuple(st0[:, s * 128:(s + 1) * 128] for s in range(S5_SLABS)), unroll=4)
    for s in range(S5_SLABS):
        st_ref[:, s * 128:(s + 1) * 128] = fin[s]
        fin_ref[:, s * 128:(s + 1) * 128] = fin[s]
    st = jnp.concatenate(
        [jnp.concatenate([bu_ref[s, b * S5_PITCH:b * S5_PITCH + tsteps, :] for s in range(S5_SLABS)],
                         axis=1) for b in range(batch)], axis=0)
    y = _s5_output(st, u, cc_ref[...], d_ref[...], gw_ref[...], gb_ref[...])
    y_ref[...] = y.reshape(batch, tsteps, GROUP_W)


def _s5(u, p, l, batch, seq):
    tsteps = min(S5_TCHUNK, seq)
    blk = pl.BlockSpec((batch, tsteps, GROUP_W), lambda i: (0, i, 0))
    return pl.pallas_call(
        functools.partial(_s5_kernel, batch=batch, tsteps=tsteps),
        grid=(seq // tsteps,),
        in_specs=[blk] + [_layer(s, l) for s in _S5_SHAPES],
        out_specs=[blk, _full((batch, 2 * SSM_S))],
        out_shape=[jax.ShapeDtypeStruct((batch, seq, GROUP_W), F32),
                   jax.ShapeDtypeStruct((batch, 2 * SSM_S), F32)],
        scratch_shapes=[pltpu.VMEM((S5_SLABS, batch * S5_PITCH, 128), F32),
                        pltpu.VMEM((batch, 2 * SSM_S), F32)],
        compiler_params=_cparams(("arbitrary",)),
    )(u, *[p[n] for n in _S5_PARAMS])


def _wkv_inputs(zd, zprev, p, bd_ones):
    zs = zd + p["mu"] * (zprev - zd)
    r = zs[:, 0:GROUP_W]
    k = zs[:, GROUP_W:2 * GROUP_W]
    v = zs[:, 2 * GROUP_W:3 * GROUP_W]
    lora = zs[:, 3 * GROUP_W:]
    w = -_softplus(-(p["w0"] + _dot(jnp.tanh(lora), p["w2"]))) - 0.5
    logd = -jnp.exp(w)
    a = jax.nn.sigmoid(p["a0"] + _dot(lora, p["a2"]))
    g = _dot(jax.nn.sigmoid(lora), p["g2"])
    kk = k * p["k_k"]
    nrm = jnp.sqrt(_dot_split(kk * kk, bd_ones))
    kk = kk / jnp.maximum(nrm, 1e-12)
    k2 = k * (1.0 + (a - 1.0) * p["k_a"])
    return r, logd, k2, v, kk, a, g


def _wkv_output(o, r, k2, v, g, p, bd_ones):
    inv_n = 1.0 / WKV_N
    m = _dot_split(o, bd_ones) * inv_n
    var = _dot_split(jnp.square(o - m), bd_ones) * inv_n
    on = (o - m) * lax.rsqrt(var + WKV_LN_EPS) * p["ln_g"] + p["ln_b"]
    bonus = _dot_split(r * k2 * p["r_k"], bd_ones) * v
    return (on + bonus) * g


_WKV_PARAMS = ("mu", "w0", "w2", "a0", "a2", "g2", "k_k", "k_a", "r_k", "ln_g", "ln_b")
_WKV_PARAM_SHAPES = {"mu": (1, D_TM), "w2": (LORA_PAD, GROUP_W), "a2": (LORA_PAD, GROUP_W),
                     "g2": (LORA_PAD, GROUP_W)}


def _wkv_param_specs(l):
    return [_layer(_WKV_PARAM_SHAPES.get(n, (1, GROUP_W)), l) for n in _WKV_PARAMS]


def _bd_mask(n):
    hr = lax.broadcasted_iota(jnp.int32, (n, n), 0) // (n // WKV_HEADS)
    hc = lax.broadcasted_iota(jnp.int32, (n, n), 1) // (n // WKV_HEADS)
    return hr == hc


def _expand(xp, lo_bf16):
    xb = xp.astype(BF16)
    zero = jnp.zeros_like(lo_bf16)
    hi_bf16 = 1 - lo_bf16
    t0, t1 = xb[:, :128], xb[:, 128:]
    return jnp.concatenate([jnp.concatenate([t0 * lo_bf16, zero], axis=1),
                            jnp.concatenate([t0 * hi_bf16, zero], axis=1),
                            jnp.concatenate([zero, t1 * lo_bf16], axis=1),
                            jnp.concatenate([zero, t1 * hi_bf16], axis=1)], axis=0)


def _interleave(*stage_gens):
    results = [None] * len(stage_gens)
    live = list(range(len(stage_gens)))
    while live:
        for i in list(live):
            try:
                next(stage_gens[i])
            except StopIteration as stop:
                results[i] = stop.value
                live.remove(i)
    return results


def _wkv_chunk_stages(r, logd, k2, v, kk, a, states, tri_ones, bd256):
    c = WKV_CHUNK
    n = len(states)
    rows = [slice(i * c, (i + 1) * c) for i in range(n)]
    t_i = lax.broadcasted_iota(jnp.int32, (c, GROUP_W), 0)
    s_i = lax.broadcasted_iota(jnp.int32, (c, GROUP_W), 1) % c
    strict = s_i < t_i
    incl = s_i <= t_i
    lo_bf16 = jnp.where(lax.broadcasted_iota(jnp.int32, (c, 128), 1) < WKV_N, 1.0, 0.0).astype(BF16)
    ex = lambda xp: _expand(xp, lo_bf16)

    cum = [jnp.dot(tri_ones, logd[rw], preferred_element_type=F32, precision=lax.Precision.HIGHEST)
           for rw in rows]
    yield
    g_last = [cm[c - 1:c] for cm in cum]
    a_t, r_t, b_t, k_t, b_h, k_h, v_c = [], [], [], [], [], [], []
    for i, rw in enumerate(rows):
        e_neg = jnp.exp(-cum[i])
        e_end = jnp.exp(g_last[i] - cum[i])
        bvec = kk[rw] * a[rw]
        a_t.append(-kk[rw] * jnp.exp(cum[i] - logd[rw]))
        r_t.append(r[rw] * jnp.exp(cum[i]))
        b_t.append(bvec * e_neg)
        k_t.append(k2[rw] * e_neg)
        b_h.append(bvec * e_end)
        k_h.append(k2[rw] * e_end)
        v_c.append(v[rw])
    yield

    ar = [jnp.concatenate([a_t[i], r_t[i]], axis=0) for i in range(n)]
    p_b = [_dot_nt(ar[i], ex(b_t[i])) for i in range(n)]
    yield
    p_k = [_dot_nt(ar[i], ex(k_t[i])) for i in range(n)]
    yield
    l_p = [jnp.where(strict, p[:c], 0.0) for p in p_b]
    aak = [jnp.where(strict, p[:c], 0.0) for p in p_k]
    rb = [jnp.where(incl, p[c:], 0.0) for p in p_b]
    rk = [jnp.where(incl, p[c:], 0.0) for p in p_k]

    def off_diag(m):
        return (t_i // (2 * m) == s_i // (2 * m)) & (t_i % (2 * m) >= m) & (s_i % (2 * m) < m)

    eye_p = jnp.where(s_i == t_i, 1.0, 0.0)
    t_p = [eye_p + jnp.where(off_diag(1), l_p[i], 0.0) for i in range(n)]
    m = 2
    while m < c:
        off = off_diag(m)
        x = [_dot(t_p[i], ex(jnp.where(off, l_p[i], 0.0))) for i in range(n)]
        yield
        t_p = [t_p[i] + _dot(x[i], ex(t_p[i])) for i in range(n)]
        yield
        m *= 2

    v_bd = [ex(v_c[i]) for i in range(n)]
    a2 = [_dot(t_p[i], ex(a_t[i])) for i in range(n)]
    av = [_dot(aak[i], v_bd[i]) for i in range(n)]
    yield
    w0 = [_dot(t_p[i], ex(av[i])) for i in range(n)]
    from_s = [_dot_nt(jnp.concatenate([a2[i], r_t[i]], axis=0), states[i]) for i in range(n)]
    yield
    w = [from_s[i][:c] + w0[i] for i in range(n)]
    o = [from_s[i][c:] + _dot(jnp.concatenate([rb[i], rk[i]], axis=1),
                              jnp.concatenate([ex(w[i]), v_bd[i]], axis=0)) for i in range(n)]
    yield
    upd = [_dot_tn(jnp.concatenate([w[i], v_c[i]], axis=0), jnp.concatenate([b_h[i], k_h[i]], axis=0))
           for i in range(n)]
    s_new = [states[i] * jnp.exp(g_last[i]) + jnp.where(bd256, upd[i], 0.0) for i in range(n)]
    return jnp.concatenate(o, axis=0), s_new


FF_CHUNK = 256


def _ffn_stages(x, ys, wo_ref, n2, wgu_ref, wd_ref, acc_ref):
    for i, y in enumerate(ys):
        x = x + jnp.dot(y.astype(BF16), wo_ref[i * GROUP_W:(i + 1) * GROUP_W, :],
                        preferred_element_type=F32)
    acc_ref[...] = x
    yield
    h = _rms(x, n2).astype(BF16)
    for c in range(D_FF // FF_CHUNK):
        gate = jnp.dot(h, wgu_ref[:, c * FF_CHUNK:(c + 1) * FF_CHUNK], preferred_element_type=F32)
        up = jnp.dot(h, wgu_ref[:, D_FF + c * FF_CHUNK:D_FF + (c + 1) * FF_CHUNK],
                     preferred_element_type=F32)
        act = (gate * jax.nn.sigmoid(gate) * up).astype(BF16)
        acc_ref[...] += jnp.dot(act, wd_ref[c * FF_CHUNK:(c + 1) * FF_CHUNK, :], preferred_element_type=F32)
        yield


def _rwkv_ffn_kernel(zd_ref, x_ref, ya_ref, yb_ref, yc_ref, wo_ref, n2_ref, gf_ref, wgu_hbm, wd_hbm,
                     *rest, nb, nt, l, final):
    prm = {n: ref[...] for n, ref in zip(_WKV_PARAMS, rest)}
    o_ref, sfin_ref, prev_ref, s_ref, yd_ref, acc_ref, wgu_ref, wd_ref, sem = rest[len(_WKV_PARAMS):]
    c = WKV_CHUNK
    j = pl.program_id(0)

    def weight_copies():
        return (pltpu.make_async_copy(wgu_hbm.at[l], wgu_ref, sem.at[0]),
                pltpu.make_async_copy(wd_hbm.at[l], wd_ref, sem.at[1]))

    def rwkv_stages():
        bd256 = _bd_mask(GROUP_W)
        bd_ones = jnp.where(bd256, 1.0, 0.0).astype(BF16)
        tri_ones = jnp.where(lax.broadcasted_iota(jnp.int32, (c, c), 1)
                             <= lax.broadcasted_iota(jnp.int32, (c, c), 0), 1.0, 0.0)
        zd3 = zd_ref[...]
        zd = zd3.reshape(nb * c, D_TM)
        first = lax.broadcasted_iota(jnp.int32, (nb, c, D_TM), 1) == 0
        carried = jnp.broadcast_to(prev_ref[:, 7:8, :], (nb, c, D_TM))
        zprev = jnp.where(first, carried, pltpu.roll(zd, 1, 0).reshape(nb, c, D_TM)).reshape(nb * c, D_TM)
        r, logd, k2, v, kk, a, g = _wkv_inputs(zd, zprev, prm, bd_ones)
        yield
        o, s_new = yield from _wkv_chunk_stages(r, logd, k2, v, kk, a, [s_ref[b] for b in range(nb)],
                                                tri_ones, bd256)
        yield
        yd_ref[...] = _wkv_output(o, r, k2, v, g, prm, bd_ones)
        for b in range(nb):
            s_ref[b] = s_new[b]
            sfin_ref[b] = s_new[b]
        prev_ref[...] = zd3[:, c - 8:, :]

    def ffn_stages():
        flat = lambda ref: ref[...].reshape(nb * c, ref.shape[-1])
        ys = (flat(ya_ref), flat(yb_ref), flat(yc_ref), yd_ref[...])
        yield from _ffn_stages(flat(x_ref), ys, wo_ref, n2_ref[...], wgu_ref, wd_ref, acc_ref)
        x = acc_ref[...]
        if final:
            x = _rms(x, gf_ref[...])
        o_ref[...] = x.reshape(nb, c, D_MODEL)

    @pl.when(j == 0)
    def _():
        prev_ref[...] = jnp.zeros_like(prev_ref)
        s_ref[...] = jnp.zeros_like(s_ref)
        for cp in weight_copies():
            cp.start()
        _interleave(rwkv_stages())
        for cp in weight_copies():
            cp.wait()

    @pl.when((j > 0) & (j < nt))
    def _():
        _interleave(ffn_stages(), rwkv_stages())

    @pl.when(j == nt)
    def _():
        _interleave(ffn_stages())


def _rwkv_ffn(zd, x, ya, yb, yc, p, l, batch, seq, final):
    c = WKV_CHUNK
    nt = seq // c
    cur = lambda wd: pl.BlockSpec((batch, c, wd), lambda j: (0, jnp.minimum(j, nt - 1), 0))
    prv = lambda wd: pl.BlockSpec((batch, c, wd), lambda j: (0, jnp.maximum(j - 1, 0), 0))
    return pl.pallas_call(
        functools.partial(_rwkv_ffn_kernel, nb=batch, nt=nt, l=l, final=final),
        grid=(nt + 1,),
        in_specs=[cur(D_TM), prv(D_MODEL), prv(GROUP_W), prv(GROUP_W), prv(GROUP_W),
                  _layer((D_MODEL, D_MODEL), l), _layer((1, D_MODEL), l), _full((1, D_MODEL)),
                  pl.BlockSpec(memory_space=pl.ANY), pl.BlockSpec(memory_space=pl.ANY)]
        + _wkv_param_specs(l),
        out_specs=[prv(D_MODEL), _full((batch, GROUP_W, GROUP_W))],
        out_shape=[jax.ShapeDtypeStruct((batch, seq, D_MODEL), F32),
                   jax.ShapeDtypeStruct((batch, GROUP_W, GROUP_W), F32)],
        scratch_shapes=[pltpu.VMEM((batch, 8, D_TM), F32),
                        pltpu.VMEM((batch, GROUP_W, GROUP_W), F32),
                        pltpu.VMEM((batch * c, GROUP_W), F32),
                        pltpu.VMEM((batch * c, D_MODEL), F32),
                        pltpu.VMEM((D_MODEL, 2 * D_FF), BF16),
                        pltpu.VMEM((D_FF, D_MODEL), BF16),
                        pltpu.SemaphoreType.DMA((2,))],
        compiler_params=_cparams(("arbitrary",)),
    )(zd, x, ya, yb, yc, p["w_out"], p["n2"], p["gf"], p["w_gu"], p["w_down"],
      *[p[n] for n in _WKV_PARAMS])


def _sample_mix_kernel(za_ref, zb_ref, zc_ref, zd_ref, shift_ref, s_ref, ssm_ref, conv_ref,
                       lng_ref, lnb_ref, w00_ref, b0_ref, lam_ref, bb_ref, cc_ref, d_ref, gw_ref,
                       gb_ref, cw_ref, cb_ref, *rest):
    prm = {n: ref[...] for n, ref in zip(_WKV_PARAMS, rest)}
    ya_ref, yb_ref, yc_ref, yd_ref, vn_ref, snew_ref, ssmnew_ref, convnew_ref = rest[len(_WKV_PARAMS):]

    za = za_ref[...]
    vn = _gm_norm(za[:, GROUP_W:], lng_ref[...], lnb_ref[...])
    vn_ref[...] = vn
    ya_ref[...] = jax.nn.gelu(za[:, :GROUP_W]) * (w00_ref[...] * vn + b0_ref[...])

    u = zb_ref[...]
    bu = jnp.dot(u.astype(BF16), bb_ref[...], preferred_element_type=F32)
    lam = lam_ref[...]
    lr, li = lam[0:1], lam[1:2]
    st = ssm_ref[...]
    s_re, s_im = st[:, :SSM_S], st[:, SSM_S:]
    st_new = jnp.concatenate([lr * s_re - li * s_im + bu[:, :SSM_S],
                              lr * s_im + li * s_re + bu[:, SSM_S:]], axis=1)
    ssmnew_ref[...] = st_new
    yb_ref[...] = _s5_output(st_new, u, cc_ref[...], d_ref[...], gw_ref[...], gb_ref[...])

    zc = zc_ref[...]
    z = zc[:, 2 * GROUP_W:] * zc[:, :GROUP_W]
    cw = cw_ref[...]
    buf = conv_ref[...]
    y = cb_ref[...] + cw[0:1] * buf[:, :GROUP_W] + cw[1:2] * buf[:, GROUP_W:] + cw[2:3] * z
    yc_ref[...] = zc[:, GROUP_W:2 * GROUP_W] * y
    convnew_ref[:, :GROUP_W] = buf[:, GROUP_W:]
    convnew_ref[:, GROUP_W:] = z

    bd256 = _bd_mask(GROUP_W)
    bd_ones = jnp.where(bd256, 1.0, 0.0).astype(BF16)
    zd = zd_ref[...]
    r, logd, k2, v, kk, a, g = _wkv_inputs(zd, shift_ref[...], prm, bd_ones)
    bt = zd.shape[0]
    s = s_ref[...]
    eye4 = (lax.broadcasted_iota(jnp.int32, (WKV_N, GROUP_W), 0)
            == lax.broadcasted_iota(jnp.int32, (WKV_N, GROUP_W), 1) % WKV_N)

    def head_sum(x3):
        return _dot_split(x3.reshape(bt * WKV_N, GROUP_W), bd_ones).reshape(bt, WKV_N, GROUP_W)

    sa = head_sum(s * (-kk)[:, None, :])
    vcol = head_sum(jnp.where(eye4[None], v[:, None, :], 0.0))
    s_new = (s * jnp.exp(logd)[:, None, :] + sa * (kk * a)[:, None, :] + vcol * k2[:, None, :])
    snew_ref[...] = s_new
    o_rep = head_sum(s_new * r[:, None, :])
    o = jnp.sum(jnp.where(eye4[None], o_rep, 0.0), axis=1)
    yd_ref[...] = _wkv_output(o, r, k2, v, g, prm, bd_ones)


def _sample_mix(za, zb, zc, zd, shift, s_t, ssm, conv, p, l, bt):
    rows = za.shape[0]
    row_blk = lambda wd: pl.BlockSpec((bt, wd), lambda i: (i, 0))
    st_blk = lambda wd: pl.BlockSpec((None, bt, wd), lambda i: (l, i, 0))
    s_blk = pl.BlockSpec((None, bt, WKV_N, GROUP_W), lambda i: (l, i, 0, 0))
    vec = _layer((1, GROUP_W), l)
    s_arg = 5
    return pl.pallas_call(
        _sample_mix_kernel,
        grid=(rows // bt,),
        in_specs=[row_blk(2 * GROUP_W), row_blk(GROUP_W), row_blk(3 * GROUP_W), row_blk(D_TM),
                  st_blk(D_TM), s_blk, st_blk(2 * SSM_S), st_blk(2 * GROUP_W),
                  vec, vec, vec, vec]
        + [_layer(s, l) for s in _S5_SHAPES]
        + [_layer((3, GROUP_W), l), vec] + _wkv_param_specs(l),
        out_specs=[row_blk(GROUP_W)] * 5 + [s_blk, row_blk(2 * SSM_S), row_blk(2 * GROUP_W)],
        out_shape=[jax.ShapeDtypeStruct((rows, GROUP_W), F32)] * 5
        + [jax.ShapeDtypeStruct(s_t.shape, F32),
           jax.ShapeDtypeStruct((rows, 2 * SSM_S), F32),
           jax.ShapeDtypeStruct((rows, 2 * GROUP_W), F32)],
        input_output_aliases={s_arg: 5},
        compiler_params=_cparams(("parallel",)),
    )(za, zb, zc, zd, shift, s_t, ssm, conv, p["lng"], p["lnb"], p["w00"], p["b0"],
      *[p[n] for n in _S5_PARAMS], p["cw"], p["cb"], *[p[n] for n in _WKV_PARAMS])


def _out_ffn_kernel(x_ref, ya_ref, yb_ref, yc_ref, yd_ref, wo_ref, g2_ref, wg_ref, wu_ref, wd_ref,
                    gf_ref, o_ref, h_ref, *, final, ff_split):
    c = pl.program_id(1)

    @pl.when(c == 0)
    def _():
        x = x_ref[...]
        for i, y_ref in enumerate((ya_ref, yb_ref, yc_ref, yd_ref)):
            x = x + jnp.dot(y_ref[...].astype(BF16), wo_ref[i * GROUP_W:(i + 1) * GROUP_W, :],
                            preferred_element_type=F32)
        o_ref[...] = x
        h_ref[...] = _rms(x, g2_ref[...]).astype(BF16)

    h = h_ref[...]
    gate = jnp.dot(h, wg_ref[...], preferred_element_type=F32)
    up = jnp.dot(h, wu_ref[...], preferred_element_type=F32)
    act = (gate * jax.nn.sigmoid(gate) * up).astype(BF16)
    o_ref[...] += jnp.dot(act, wd_ref[...], preferred_element_type=F32)

    if final:
        @pl.when(c == ff_split - 1)
        def _():
            o_ref[...] = _rms(o_ref[...], gf_ref[...])


def _out_ffn(x, ys, p, l, tm, final, ff_split):
    rows = x.shape[0]
    fc = D_FF // ff_split
    row_blk = lambda wd_: pl.BlockSpec((tm, wd_), lambda i, c: (i, 0))
    return pl.pallas_call(
        functools.partial(_out_ffn_kernel, final=final, ff_split=ff_split),
        grid=(rows // tm, ff_split),
        in_specs=[row_blk(D_MODEL)] + [row_blk(GROUP_W)] * 4
        + [_layer((D_MODEL, D_MODEL), l), _layer((1, D_MODEL), l),
           pl.BlockSpec((None, D_MODEL, fc), lambda i, c: (l, 0, c)),
           pl.BlockSpec((None, D_MODEL, fc), lambda i, c: (l, 0, ff_split + c)),
           pl.BlockSpec((None, fc, D_MODEL), lambda i, c: (l, c, 0)),
           _full((1, D_MODEL))],
        out_specs=row_blk(D_MODEL),
        out_shape=jax.ShapeDtypeStruct((rows, D_MODEL), F32),
        scratch_shapes=[pltpu.VMEM((tm, D_MODEL), BF16)],
        compiler_params=_cparams(("parallel", "arbitrary")),
    )(x, *ys, p["w_out"], p["n2"], p["w_gu"], p["w_gu"], p["w_down"], p["gf"])


def _pad_lora(w, start):
    return jnp.pad(w.astype(BF16), ((0, 0), (start, LORA_PAD - start - w.shape[1]), (0, 0)))


def kernel(x_prompt, x_sample, state_wkv, state_shift, state_ssm_re, state_ssm_im, state_conv,
           norm1_g, w_in, gm_ln_g, gm_ln_b, gm_ws, gm_bs,
           ssm_a_re, ssm_a_im, ssm_log_dt, ssm_b_re, ssm_b_im, ssm_c_re, ssm_c_im, ssm_d,
           ssm_glu_w, ssm_glu_b, conv_w, conv_b,
           tm_mu, tm_w0, tm_w2, tm_a0, tm_a2, tm_g2, tm_k_k, tm_k_a, tm_r_k, tm_ln_g, tm_ln_b,
           w_out, norm2_g, ffn_w_gu, ffn_w_down, norm_f_g):
    depth = w_in.shape[0]
    bp, seq, _ = x_prompt.shape
    bs = x_sample.shape[0]
    assert x_sample.shape[1] == 1 and seq % GM_CHUNK == 0 and seq % WKV_CHUNK == 0
    head_d = GROUP_W // GM_HEADS
    rowv = lambda p: p.reshape(depth, 1, -1)

    lam, bb, cc = _s5_prep(ssm_a_re, ssm_a_im, ssm_log_dt, ssm_b_re, ssm_b_im, ssm_c_re, ssm_c_im)
    p = {
        "n1": rowv(norm1_g), "n2": rowv(norm2_g), "w_in": w_in.astype(BF16),
        "lng": rowv(gm_ln_g), "lnb": rowv(gm_ln_b),
        "wcat": jnp.transpose(gm_ws, (0, 2, 1, 3)).reshape(depth, GM_CHUNK, GM_HEADS * GM_CHUNK),
        "bias": jnp.repeat(jnp.transpose(gm_bs, (0, 2, 1)), head_d, axis=2),
        "w00": rowv(jnp.repeat(gm_ws[:, :, 0, 0], head_d, axis=1)),
        "b0": rowv(jnp.repeat(gm_bs[:, :, 0], head_d, axis=1)),
        "lam": lam, "bb": bb, "cc": cc, "ssm_d": rowv(ssm_d),
        "glu_w": ssm_glu_w.astype(BF16), "glu_b": rowv(ssm_glu_b),
        "cw": conv_w, "cb": rowv(conv_b),
        "mu": rowv(tm_mu), "w0": rowv(tm_w0), "w2": _pad_lora(tm_w2, 0),
        "a0": rowv(tm_a0), "a2": _pad_lora(tm_a2, 32), "g2": _pad_lora(tm_g2, 64),
        "k_k": rowv(tm_k_k), "k_a": rowv(tm_k_a), "r_k": rowv(tm_r_k),
        "ln_g": rowv(tm_ln_g), "ln_b": rowv(tm_ln_b),
        "w_out": w_out.astype(BF16), "w_gu": ffn_w_gu.astype(BF16), "w_down": ffn_w_down.astype(BF16),
        "gf": norm_f_g.reshape(1, D_MODEL),
    }

    xp = x_prompt.reshape(bp * seq, D_MODEL)
    xs = x_sample.reshape(bs, D_MODEL)
    wkv_s_buf = jnp.transpose(state_wkv, (0, 1, 3, 2, 4)).reshape(depth, bs, WKV_N, GROUP_W)
    ssm_s_in = jnp.concatenate([state_ssm_re.reshape(depth, bs, SSM_S),
                                state_ssm_im.reshape(depth, bs, SSM_S)], axis=-1)
    conv_s_in = state_conv.reshape(depth, bs, 2 * GROUP_W)

    outs = {k: [] for k in ("wkv_p", "sh_p", "sh_s", "ssm_p", "ssm_s", "cv_p", "cv_s", "chv")}
    tm_p = ROW_TILE if seq % ROW_TILE == 0 else GM_CHUNK
    bt_s = SAMPLE_TILE if bs % SAMPLE_TILE == 0 else bs
    for l in range(depth):
        final = l == depth - 1

        ya, zb, yc, zd, tail = _inproj_mix(xp, p, l, bp, seq, tm_p)
        per_batch = lambda y: y.reshape(bp, seq, y.shape[-1])
        yb, ssm_fin = _s5(per_batch(zb), p, l, bp, seq)
        xp3, wkv_fin = _rwkv_ffn(per_batch(zd), per_batch(xp), per_batch(ya), yb, per_batch(yc),
                                 p, l, bp, seq, final)
        xp = xp3.reshape(bp * seq, D_MODEL)
        outs["wkv_p"].append(wkv_fin)
        outs["sh_p"].append(zd.reshape(bp, seq, D_TM)[:, -1])
        outs["ssm_p"].append(ssm_fin)
        outs["cv_p"].append(tail[:, 6:8])

        za, zb, zc, zd = _inproj(xs, p["n1"], p["w_in"], l)
        ya, yb, yc, yd, vn, wkv_s_buf, ssm_new, conv_new = _sample_mix(
            za, zb, zc, zd, state_shift, wkv_s_buf, ssm_s_in, conv_s_in, p, l, bt_s)
        xs = _out_ffn(xs, (ya, yb, yc, yd), p, l, bs, final, FF_SPLIT_SAMPLE)
        outs["sh_s"].append(zd)
        outs["ssm_s"].append(ssm_new)
        outs["cv_s"].append(conv_new.reshape(bs, 2, GROUP_W))
        outs["chv"].append(vn.reshape(bs, 1, GROUP_W))

    def wkv_blocks(s_bd):
        s5d = s_bd.reshape(depth, -1, WKV_HEADS, WKV_N, WKV_HEADS, WKV_N)
        return jnp.stack([s5d[:, :, h, :, h, :] for h in range(WKV_HEADS)], axis=2)

    wkv_p = wkv_blocks(jnp.stack(outs["wkv_p"]))
    wkv_s = jnp.transpose(wkv_s_buf.reshape(depth, bs, WKV_N, WKV_HEADS, WKV_N), (0, 1, 3, 2, 4))
    ssm_p = jnp.stack(outs["ssm_p"])
    ssm_s = jnp.stack(outs["ssm_s"])
    split = lambda s, i: s[..., i * SSM_S:(i + 1) * SSM_S].reshape(depth, -1, SSM_GROUPS, SSM_P)
    return (xp.reshape(bp, seq, D_MODEL), xs.reshape(bs, 1, D_MODEL),
            wkv_p, wkv_s,
            jnp.stack(outs["sh_p"]), jnp.stack(outs["sh_s"]),
            split(ssm_p, 0), split(ssm_s, 0), split(ssm_p, 1), split(ssm_s, 1),
            jnp.stack(outs["cv_p"]), jnp.stack(outs["cv_s"]),
            jnp.stack(outs["chv"]))
```

```python
import functools

import jax
import jax.numpy as jnp
from jax import lax
from jax.experimental import pallas as pl
from jax.experimental.pallas import tpu as pltpu

F32 = jnp.float32
BF16 = jnp.bfloat16

D_MODEL = 1024
GROUP_W = 256
GM_CHUNK = 128
GM_HEADS = 4
SSM_CH = 16
SSM_GROUPS = 16
SSM_P = 64
SSM_S = SSM_GROUPS * SSM_P
WKV_N = 64
WKV_HEADS = 4
LORA_PAD = 128
D_TM = 3 * GROUP_W + LORA_PAD
IN_COLS = 6 * GROUP_W + D_TM
D_FF = 2816
FF_SPLIT_SAMPLE = 11
NORM_EPS = 1e-6
GM_LN_EPS = 1e-5
WKV_LN_EPS = 64e-5

WKV_CHUNK = 64
S5_TCHUNK = 128
ROW_TILE = 1024
SAMPLE_TILE = 32
VMEM_LIMIT = 56 * 1024 * 1024


def _cparams(sem):
    return pltpu.CompilerParams(dimension_semantics=sem, vmem_limit_bytes=VMEM_LIMIT)


def _full(shape):
    n = len(shape)
    return pl.BlockSpec(shape, lambda *_: (0,) * n)


def _layer(shape, l):
    n = len(shape)
    return pl.BlockSpec((None,) + tuple(shape), lambda *_: (l,) + (0,) * n)


def _dot(a, b):
    return jnp.dot(a.astype(BF16), b.astype(BF16), preferred_element_type=F32)


def _dot_nt(a, b):
    return lax.dot_general(a.astype(BF16), b.astype(BF16), (((1,), (1,)), ((), ())),
                           preferred_element_type=F32)


def _dot_tn(a, b):
    return lax.dot_general(a.astype(BF16), b.astype(BF16), (((0,), (0,)), ((), ())),
                           preferred_element_type=F32)


def _dot_split(x, ones_bf16):
    hi = x.astype(BF16)
    lo = (x - hi.astype(F32)).astype(BF16)
    return (jnp.dot(hi, ones_bf16, preferred_element_type=F32)
            + jnp.dot(lo, ones_bf16, preferred_element_type=F32))


def _rms(x, g):
    return x * lax.rsqrt(jnp.mean(x * x, axis=-1, keepdims=True) + NORM_EPS) * g


def _softplus(y):
    return jnp.maximum(y, 0.0) + jnp.log1p(jnp.exp(-jnp.abs(y)))


def _gm_norm(zav, ln_g, ln_b):
    vf = jax.nn.gelu(zav)
    mu = jnp.mean(vf, axis=-1, keepdims=True)
    var = jnp.mean(jnp.square(vf - mu), axis=-1, keepdims=True)
    return (vf - mu) * lax.rsqrt(var + GM_LN_EPS) * ln_g + ln_b


def _inproj_kernel(x_ref, g_ref, w_ref, za_ref, zb_ref, zc_ref, zd_ref):
    h = _rms(x_ref[...], g_ref[...])
    z = jnp.dot(h.astype(BF16), w_ref[...], preferred_element_type=F32)
    za_ref[...] = z[:, 0:2 * GROUP_W]
    zb_ref[...] = z[:, 2 * GROUP_W:3 * GROUP_W]
    zc_ref[...] = z[:, 3 * GROUP_W:6 * GROUP_W]
    zd_ref[...] = z[:, 6 * GROUP_W:]


def _inproj(x, g, w, l):
    rows = x.shape[0]
    widths = (2 * GROUP_W, GROUP_W, 3 * GROUP_W, D_TM)
    return pl.pallas_call(
        _inproj_kernel,
        grid=(1,),
        in_specs=[_full((rows, D_MODEL)), _layer((1, D_MODEL), l), _layer((D_MODEL, IN_COLS), l)],
        out_specs=[_full((rows, wd)) for wd in widths],
        out_shape=[jax.ShapeDtypeStruct((rows, wd), F32) for wd in widths],
        compiler_params=_cparams(("arbitrary",)),
    )(x, g, w)


def _inproj_mix_kernel(x_ref, g_ref, w_ref, lng_ref, lnb_ref, wcat_ref, bias_ref, cw_ref, cb_ref,
                       ya_ref, zb_ref, yc_ref, zd_ref, tail_ref, prev_ref, *, tile):
    @pl.when(pl.program_id(1) == 0)
    def _():
        prev_ref[...] = jnp.zeros_like(prev_ref)

    h = _rms(x_ref[...], g_ref[...])
    z = jnp.dot(h.astype(BF16), w_ref[...], preferred_element_type=F32)
    zb_ref[...] = z[:, 2 * GROUP_W:3 * GROUP_W]
    zd_ref[...] = z[:, 6 * GROUP_W:]

    u = jax.nn.gelu(z[:, :GROUP_W])
    vn = _gm_norm(z[:, GROUP_W:2 * GROUP_W], lng_ref[...], lnb_ref[...])
    kc = GM_HEADS * GM_CHUNK
    t_i = lax.broadcasted_iota(jnp.int32, (GM_CHUNK, kc), 0)
    s_i = lax.broadcasted_iota(jnp.int32, (GM_CHUNK, kc), 1) % GM_CHUNK
    wm = jnp.where(s_i <= t_i, wcat_ref[...], 0.0).astype(BF16)
    r_h = lax.broadcasted_iota(jnp.int32, (kc, GROUP_W), 0) // GM_CHUNK
    c_h = lax.broadcasted_iota(jnp.int32, (kc, GROUP_W), 1) // (GROUP_W // GM_HEADS)
    head_mask = r_h == c_h
    for c in range(tile // GM_CHUNK):
        rows = slice(c * GM_CHUNK, (c + 1) * GM_CHUNK)
        vc = vn[rows].astype(BF16)
        rhs = jnp.where(head_mask, jnp.concatenate([vc] * GM_HEADS, axis=0), jnp.zeros((), BF16))
        s = jnp.dot(wm, rhs, preferred_element_type=F32) + bias_ref[...]
        ya_ref[rows, :] = u[rows] * s

    zz = z[:, 5 * GROUP_W:6 * GROUP_W] * z[:, 3 * GROUP_W:4 * GROUP_W]
    row = lax.broadcasted_iota(jnp.int32, zz.shape, 0)
    prev = prev_ref[...]
    z1 = jnp.where(row == 0, prev[7:8], pltpu.roll(zz, 1, 0))
    z2 = jnp.where(row == 0, prev[6:7], jnp.where(row == 1, prev[7:8], pltpu.roll(zz, 2, 0)))
    cw = cw_ref[...]
    y = cb_ref[...] + cw[0:1] * z2 + cw[1:2] * z1 + cw[2:3] * zz
    yc_ref[...] = z[:, 4 * GROUP_W:5 * GROUP_W] * y
    prev_ref[...] = zz[tile - 8:]
    tail_ref[...] = zz[tile - 8:]


def _inproj_mix(x, p, l, batch, seq, tile):
    nt = seq // tile
    rows = batch * seq
    row_blk = lambda wd: pl.BlockSpec((tile, wd), lambda b, j: (b * nt + j, 0))
    widths = (GROUP_W, GROUP_W, GROUP_W, D_TM)
    return pl.pallas_call(
        functools.partial(_inproj_mix_kernel, tile=tile),
        grid=(batch, nt),
        in_specs=[row_blk(D_MODEL), _layer((1, D_MODEL), l), _layer((D_MODEL, IN_COLS), l),
                  _layer((1, GROUP_W), l), _layer((1, GROUP_W), l),
                  _layer((GM_CHUNK, GM_HEADS * GM_CHUNK), l), _layer((GM_CHUNK, GROUP_W), l),
                  _layer((3, GROUP_W), l), _layer((1, GROUP_W), l)],
        out_specs=[row_blk(wd) for wd in widths]
        + [pl.BlockSpec((None, 8, GROUP_W), lambda b, j: (b, 0, 0))],
        out_shape=[jax.ShapeDtypeStruct((rows, wd), F32) for wd in widths]
        + [jax.ShapeDtypeStruct((batch, 8, GROUP_W), F32)],
        scratch_shapes=[pltpu.VMEM((8, GROUP_W), F32)],
        compiler_params=_cparams(("parallel", "arbitrary")),
    )(x, p["n1"], p["w_in"], p["lng"], p["lnb"], p["wcat"], p["bias"], p["cw"], p["cb"])


def _s5_prep_kernel(are_ref, aim_ref, ldt_ref, bre_ref, bim_ref, cre_ref, cim_ref,
                    lam_ref, bb_ref, cc_ref):
    lam_re = jnp.minimum(are_ref[...], -1e-4)
    lam_im = aim_ref[...]
    dt = jnp.exp(ldt_ref[...])
    mag = jnp.exp(lam_re * dt)
    lb_re = mag * jnp.cos(lam_im * dt)
    lb_im = mag * jnp.sin(lam_im * dt)
    den = lam_re * lam_re + lam_im * lam_im
    f_re = ((lb_re - 1.0) * lam_re + lb_im * lam_im) / den
    f_im = (lb_im * lam_re - (lb_re - 1.0) * lam_im) / den
    lam_ref[0:1, :] = lb_re
    lam_ref[1:2, :] = lb_im
    br, bi = bre_ref[...], bim_ref[...]
    grp_r = lax.broadcasted_iota(jnp.int32, (GROUP_W, SSM_S), 0) // SSM_CH
    grp_c = lax.broadcasted_iota(jnp.int32, (GROUP_W, SSM_S), 1) // SSM_P
    m = grp_r == grp_c
    bb_ref[:, :SSM_S] = jnp.where(m, f_re * br - f_im * bi, 0.0).astype(BF16)
    bb_ref[:, SSM_S:] = jnp.where(m, f_re * bi + f_im * br, 0.0).astype(BF16)
    grp_r2 = lax.broadcasted_iota(jnp.int32, (SSM_S, GROUP_W), 0) // SSM_P
    grp_c2 = lax.broadcasted_iota(jnp.int32, (SSM_S, GROUP_W), 1) // SSM_CH
    m2 = grp_r2 == grp_c2
    cc_ref[:SSM_S, :] = jnp.where(m2, cre_ref[...], 0.0).astype(BF16)
    cc_ref[SSM_S:, :] = jnp.where(m2, -cim_ref[...], 0.0).astype(BF16)


def _s5_prep(a_re, a_im, log_dt, b_re, b_im, c_re, c_im):
    depth = a_re.shape[0]
    flat = lambda p: p.reshape(depth, 1, SSM_S)
    b_exp = lambda b: jnp.tile(jnp.transpose(b, (0, 3, 1, 2)).reshape(depth, SSM_CH, SSM_S),
                               (1, SSM_GROUPS, 1))
    c_exp = lambda c: jnp.tile(jnp.transpose(c, (0, 1, 3, 2)).reshape(depth, SSM_S, SSM_CH),
                               (1, 1, SSM_GROUPS))
    lyr = lambda shape: pl.BlockSpec((None,) + shape, lambda l: (l, 0, 0))
    return pl.pallas_call(
        _s5_prep_kernel,
        grid=(depth,),
        in_specs=[lyr((1, SSM_S))] * 3 + [lyr((GROUP_W, SSM_S))] * 2 + [lyr((SSM_S, GROUP_W))] * 2,
        out_specs=[lyr((2, SSM_S)), lyr((GROUP_W, 2 * SSM_S)), lyr((2 * SSM_S, GROUP_W))],
        out_shape=[jax.ShapeDtypeStruct((depth, 2, SSM_S), F32),
                   jax.ShapeDtypeStruct((depth, GROUP_W, 2 * SSM_S), BF16),
                   jax.ShapeDtypeStruct((depth, 2 * SSM_S, GROUP_W), BF16)],
        compiler_params=_cparams(("arbitrary",)),
    )(flat(a_re), flat(a_im), flat(log_dt), b_exp(b_re), b_exp(b_im), c_exp(c_re), c_exp(c_im))


def _s5_output(st, u, cc, d, glu_w, glu_b):
    y = jnp.dot(st.astype(BF16), cc, preferred_element_type=F32) + d * u
    y = jax.nn.gelu(y)
    return y * jax.nn.sigmoid(jnp.dot(y.astype(BF16), glu_w, preferred_element_type=F32) + glu_b)


_S5_PARAMS = ("lam", "bb", "cc", "ssm_d", "glu_w", "glu_b")
_S5_SHAPES = ((2, SSM_S), (GROUP_W, 2 * SSM_S), (2 * SSM_S, GROUP_W), (1, GROUP_W),
              (GROUP_W, GROUP_W), (1, GROUP_W))


S5_SLABS = 2 * SSM_S // 128
S5_PITCH = S5_TCHUNK + 8


def _s5_kernel(u_ref, lam_ref, bb_ref, cc_ref, d_ref, gw_ref, gb_ref, y_ref, fin_ref,
               bu_ref, st_ref, *, batch, tsteps):
    half = S5_SLABS // 2

    @pl.when(pl.program_id(0) == 0)
    def _():
        st_ref[...] = jnp.zeros_like(st_ref)

    u = u_ref[...].reshape(batch * tsteps, GROUP_W)
    bu = jnp.dot(u.astype(BF16), bb_ref[...], preferred_element_type=F32)
    for b in range(batch):
        for s in range(S5_SLABS):
            bu_ref[s, b * S5_PITCH:b * S5_PITCH + tsteps, :] = (
                bu[b * tsteps:(b + 1) * tsteps, s * 128:(s + 1) * 128])
    lam = lam_ref[...]
    lr = [jnp.broadcast_to(lam[0:1, s * 128:(s + 1) * 128], (batch, 128)) for s in range(half)]
    li = [jnp.broadcast_to(lam[1:2, s * 128:(s + 1) * 128], (batch, 128)) for s in range(half)]

    def step(t, carry):
        rows = pl.ds(t, batch, stride=S5_PITCH)
        new = [None] * S5_SLABS
        for s in range(half):
            s_re, s_im = carry[s], carry[half + s]
            new[s] = lr[s] * s_re - li[s] * s_im + bu_ref[s, rows, :]
            new[half + s] = lr[s] * s_im + li[s] * s_re + bu_ref[half + s, rows, :]
            bu_ref[s, rows, :] = new[s]
            bu_ref[half + s, rows, :] = new[half + s]
        return tuple(new)

    st0 = st_ref[...]
    fin = lax.fori_loop(0, tsteps, step,
                        tuple(st0[:, s * 128:(s + 1) * 128] for s in range(S5_SLABS)), unroll=4)
    for s in range(S5_SLABS):
        st_ref[:, s * 128:(s + 1) * 128] = fin[s]
        fin_ref[:, s * 128:(s + 1) * 128] = fin[s]
    st = jnp.concatenate(
        [jnp.concatenate([bu_ref[s, b * S5_PITCH:b * S5_PITCH + tsteps, :] for s in range(S5_SLABS)],
                         axis=1) for b in range(batch)], axis=0)
    y = _s5_output(st, u, cc_ref[...], d_ref[...], gw_ref[...], gb_ref[...])
    y_ref[...] = y.reshape(batch, tsteps, GROUP_W)


def _s5(u, p, l, batch, seq):
    tsteps = min(S5_TCHUNK, seq)
    blk = pl.BlockSpec((batch, tsteps, GROUP_W), lambda i: (0, i, 0))
    return pl.pallas_call(
        functools.partial(_s5_kernel, batch=batch, tsteps=tsteps),
        grid=(seq // tsteps,),
        in_specs=[blk] + [_layer(s, l) for s in _S5_SHAPES],
        out_specs=[blk, _full((batch, 2 * SSM_S))],
        out_shape=[jax.ShapeDtypeStruct((batch, seq, GROUP_W), F32),
                   jax.ShapeDtypeStruct((batch, 2 * SSM_S), F32)],
        scratch_shapes=[pltpu.VMEM((S5_SLABS, batch * S5_PITCH, 128), F32),
                        pltpu.VMEM((batch, 2 * SSM_S), F32)],
        compiler_params=_cparams(("arbitrary",)),
    )(u, *[p[n] for n in _S5_PARAMS])


def _wkv_inputs(zd, zprev, p, bd_ones):
    zs = zd + p["mu"] * (zprev - zd)
    r = zs[:, 0:GROUP_W]
    k = zs[:, GROUP_W:2 * GROUP_W]
    v = zs[:, 2 * GROUP_W:3 * GROUP_W]
    lora = zs[:, 3 * GROUP_W:]
    w = -_softplus(-(p["w0"] + _dot(jnp.tanh(lora), p["w2"]))) - 0.5
    logd = -jnp.exp(w)
    a = jax.nn.sigmoid(p["a0"] + _dot(lora, p["a2"]))
    g = _dot(jax.nn.sigmoid(lora), p["g2"])
    kk = k * p["k_k"]
    nrm = jnp.sqrt(_dot_split(kk * kk, bd_ones))
    kk = kk / jnp.maximum(nrm, 1e-12)
    k2 = k * (1.0 + (a - 1.0) * p["k_a"])
    return r, logd, k2, v, kk, a, g


def _wkv_output(o, r, k2, v, g, p, bd_ones):
    inv_n = 1.0 / WKV_N
    m = _dot_split(o, bd_ones) * inv_n
    var = _dot_split(jnp.square(o - m), bd_ones) * inv_n
    on = (o - m) * lax.rsqrt(var + WKV_LN_EPS) * p["ln_g"] + p["ln_b"]
    bonus = _dot_split(r * k2 * p["r_k"], bd_ones) * v
    return (on + bonus) * g


_WKV_PARAMS = ("mu", "w0", "w2", "a0", "a2", "g2", "k_k", "k_a", "r_k", "ln_g", "ln_b")
_WKV_PARAM_SHAPES = {"mu": (1, D_TM), "w2": (LORA_PAD, GROUP_W), "a2": (LORA_PAD, GROUP_W),
                     "g2": (LORA_PAD, GROUP_W)}


def _wkv_param_specs(l):
    return [_layer(_WKV_PARAM_SHAPES.get(n, (1, GROUP_W)), l) for n in _WKV_PARAMS]


def _bd_mask(n):
    hr = lax.broadcasted_iota(jnp.int32, (n, n), 0) // (n // WKV_HEADS)
    hc = lax.broadcasted_iota(jnp.int32, (n, n), 1) // (n // WKV_HEADS)
    return hr == hc


def _expand(xp, lo_bf16):
    xb = xp.astype(BF16)
    zero = jnp.zeros_like(lo_bf16)
    hi_bf16 = 1 - lo_bf16
    t0, t1 = xb[:, :128], xb[:, 128:]
    return jnp.concatenate([jnp.concatenate([t0 * lo_bf16, zero], axis=1),
                            jnp.concatenate([t0 * hi_bf16, zero], axis=1),
                            jnp.concatenate([zero, t1 * lo_bf16], axis=1),
                            jnp.concatenate([zero, t1 * hi_bf16], axis=1)], axis=0)


def _interleave(*stage_gens):
    results = [None] * len(stage_gens)
    live = list(range(len(stage_gens)))
    while live:
        for i in list(live):
            try:
                next(stage_gens[i])
            except StopIteration as stop:
                results[i] = stop.value
                live.remove(i)
    return results


def _wkv_chunk_stages(r, logd, k2, v, kk, a, states, tri_ones, bd256):
    c = WKV_CHUNK
    n = len(states)
    rows = [slice(i * c, (i + 1) * c) for i in range(n)]
    t_i = lax.broadcasted_iota(jnp.int32, (c, GROUP_W), 0)
    s_i = lax.broadcasted_iota(jnp.int32, (c, GROUP_W), 1) % c
    strict = s_i < t_i
    incl = s_i <= t_i
    lo_bf16 = jnp.where(lax.broadcasted_iota(jnp.int32, (c, 128), 1) < WKV_N, 1.0, 0.0).astype(BF16)
    ex = lambda xp: _expand(xp, lo_bf16)

    cum = [jnp.dot(tri_ones, logd[rw], preferred_element_type=F32, precision=lax.Precision.HIGHEST)
           for rw in rows]
    yield
    g_last = [cm[c - 1:c] for cm in cum]
    a_t, r_t, b_t, k_t, b_h, k_h, v_c = [], [], [], [], [], [], []
    for i, rw in enumerate(rows):
        e_neg = jnp.exp(-cum[i])
        e_end = jnp.exp(g_last[i] - cum[i])
        bvec = kk[rw] * a[rw]
        a_t.append(-kk[rw] * jnp.exp(cum[i] - logd[rw]))
        r_t.append(r[rw] * jnp.exp(cum[i]))
        b_t.append(bvec * e_neg)
        k_t.append(k2[rw] * e_neg)
        b_h.append(bvec * e_end)
        k_h.append(k2[rw] * e_end)
        v_c.append(v[rw])
    yield

    ar = [jnp.concatenate([a_t[i], r_t[i]], axis=0) for i in range(n)]
    p_b = [_dot_nt(ar[i], ex(b_t[i])) for i in range(n)]
    yield
    p_k = [_dot_nt(ar[i], ex(k_t[i])) for i in range(n)]
    yield
    l_p = [jnp.where(strict, p[:c], 0.0) for p in p_b]
    aak = [jnp.where(strict, p[:c], 0.0) for p in p_k]
    rb = [jnp.where(incl, p[c:], 0.0) for p in p_b]
    rk = [jnp.where(incl, p[c:], 0.0) for p in p_k]

    def off_diag(m):
        return (t_i // (2 * m) == s_i // (2 * m)) & (t_i % (2 * m) >= m) & (s_i % (2 * m) < m)

    eye_p = jnp.where(s_i == t_i, 1.0, 0.0)
    t_p = [eye_p + jnp.where(off_diag(1), l_p[i], 0.0) for i in range(n)]
    m = 2
    while m < c:
        off = off_diag(m)
        x = [_dot(t_p[i], ex(jnp.where(off, l_p[i], 0.0))) for i in range(n)]
        yield
        t_p = [t_p[i] + _dot(x[i], ex(t_p[i])) for i in range(n)]
        yield
        m *= 2

    v_bd = [ex(v_c[i]) for i in range(n)]
    a2 = [_dot(t_p[i], ex(a_t[i])) for i in range(n)]
    av = [_dot(aak[i], v_bd[i]) for i in range(n)]
    yield
    w0 = [_dot(t_p[i], ex(av[i])) for i in range(n)]
    from_s = [_dot_nt(jnp.concatenate([a2[i], r_t[i]], axis=0), states[i]) for i in range(n)]
    yield
    w = [from_s[i][:c] + w0[i] for i in range(n)]
    o = [from_s[i][c:] + _dot(jnp.concatenate([rb[i], rk[i]], axis=1),
                              jnp.concatenate([ex(w[i]), v_bd[i]], axis=0)) for i in range(n)]
    yield
    upd = [_dot_tn(jnp.concatenate([w[i], v_c[i]], axis=0), jnp.concatenate([b_h[i], k_h[i]], axis=0))
           for i in range(n)]
    s_new = [states[i] * jnp.exp(g_last[i]) + jnp.where(bd256, upd[i], 0.0) for i in range(n)]
    return jnp.concatenate(o, axis=0), s_new


FF_CHUNK = 256


def _ffn_stages(x, ys, wo_ref, n2, wgu_ref, wd_ref, acc_ref):
    for i, y in enumerate(ys):
        x = x + jnp.dot(y.astype(BF16), wo_ref[i * GROUP_W:(i + 1) * GROUP_W, :],
                        preferred_element_type=F32)
    acc_ref[...] = x
    yield
    h = _rms(x, n2).astype(BF16)
    for c in range(D_FF // FF_CHUNK):
        gate = jnp.dot(h, wgu_ref[:, c * FF_CHUNK:(c + 1) * FF_CHUNK], preferred_element_type=F32)
        up = jnp.dot(h, wgu_ref[:, D_FF + c * FF_CHUNK:D_FF + (c + 1) * FF_CHUNK],
                     preferred_element_type=F32)
        act = (gate * jax.nn.sigmoid(gate) * up).astype(BF16)
        acc_ref[...] += jnp.dot(act, wd_ref[c * FF_CHUNK:(c + 1) * FF_CHUNK, :], preferred_element_type=F32)
        yield


def _rwkv_ffn_kernel(zd_ref, x_ref, ya_ref, yb_ref, yc_ref, wo_ref, n2_ref, gf_ref, wgu_hbm, wd_hbm,
                     *rest, nb, nt, l, final):
    prm = {n: ref[...] for n, ref in zip(_WKV_PARAMS, rest)}
    o_ref, sfin_ref, prev_ref, s_ref, yd_ref, acc_ref, wgu_ref, wd_ref, sem = rest[len(_WKV_PARAMS):]
    c = WKV_CHUNK
    j = pl.program_id(0)

    def weight_copies():
        return (pltpu.make_async_copy(wgu_hbm.at[l], wgu_ref, sem.at[0]),
                pltpu.make_async_copy(wd_hbm.at[l], wd_ref, sem.at[1]))

    def rwkv_stages():
        bd256 = _bd_mask(GROUP_W)
        bd_ones = jnp.where(bd256, 1.0, 0.0).astype(BF16)
        tri_ones = jnp.where(lax.broadcasted_iota(jnp.int32, (c, c), 1)
                             <= lax.broadcasted_iota(jnp.int32, (c, c), 0), 1.0, 0.0)
        zd3 = zd_ref[...]
        zd = zd3.reshape(nb * c, D_TM)
        first = lax.broadcasted_iota(jnp.int32, (nb, c, D_TM), 1) == 0
        carried = jnp.broadcast_to(prev_ref[:, 7:8, :], (nb, c, D_TM))
        zprev = jnp.where(first, carried, pltpu.roll(zd, 1, 0).reshape(nb, c, D_TM)).reshape(nb * c, D_TM)
        r, logd, k2, v, kk, a, g = _wkv_inputs(zd, zprev, prm, bd_ones)
        yield
        o, s_new = yield from _wkv_chunk_stages(r, logd, k2, v, kk, a, [s_ref[b] for b in range(nb)],
                                                tri_ones, bd256)
        yield
        yd_ref[...] = _wkv_output(o, r, k2, v, g, prm, bd_ones)
        for b in range(nb):
            s_ref[b] = s_new[b]
            sfin_ref[b] = s_new[b]
        prev_ref[...] = zd3[:, c - 8:, :]

    def ffn_stages():
        flat = lambda ref: ref[...].reshape(nb * c, ref.shape[-1])
        ys = (flat(ya_ref), flat(yb_ref), flat(yc_ref), yd_ref[...])
        yield from _ffn_stages(flat(x_ref), ys, wo_ref, n2_ref[...], wgu_ref, wd_ref, acc_ref)
        x = acc_ref[...]
        if final:
            x = _rms(x, gf_ref[...])
        o_ref[...] = x.reshape(nb, c, D_MODEL)

    @pl.when(j == 0)
    def _():
        prev_ref[...] = jnp.zeros_like(prev_ref)
        s_ref[...] = jnp.zeros_like(s_ref)
        for cp in weight_copies():
            cp.start()
        _interleave(rwkv_stages())
        for cp in weight_copies():
            cp.wait()

    @pl.when((j > 0) & (j < nt))
    def _():
        _interleave(ffn_stages(), rwkv_stages())

    @pl.when(j == nt)
    def _():
        _interleave(ffn_stages())


def _rwkv_ffn(zd, x, ya, yb, yc, p, l, batch, seq, final):
    c = WKV_CHUNK
    nt = seq // c
    cur = lambda wd: pl.BlockSpec((batch, c, wd), lambda j: (0, jnp.minimum(j, nt - 1), 0))
    prv = lambda wd: pl.BlockSpec((batch, c, wd), lambda j: (0, jnp.maximum(j - 1, 0), 0))
    return pl.pallas_call(
        functools.partial(_rwkv_ffn_kernel, nb=batch, nt=nt, l=l, final=final),
        grid=(nt + 1,),
        in_specs=[cur(D_TM), prv(D_MODEL), prv(GROUP_W), prv(GROUP_W), prv(GROUP_W),
                  _layer((D_MODEL, D_MODEL), l), _layer((1, D_MODEL), l), _full((1, D_MODEL)),
                  pl.BlockSpec(memory_space=pl.ANY), pl.BlockSpec(memory_space=pl.ANY)]
        + _wkv_param_specs(l),
        out_specs=[prv(D_MODEL), _full((batch, GROUP_W, GROUP_W))],
        out_shape=[jax.ShapeDtypeStruct((batch, seq, D_MODEL), F32),
                   jax.ShapeDtypeStruct((batch, GROUP_W, GROUP_W), F32)],
        scratch_shapes=[pltpu.VMEM((batch, 8, D_TM), F32),
                        pltpu.VMEM((batch, GROUP_W, GROUP_W), F32),
                        pltpu.VMEM((batch * c, GROUP_W), F32),
                        pltpu.VMEM((batch * c, D_MODEL), F32),
                        pltpu.VMEM((D_MODEL, 2 * D_FF), BF16),
                        pltpu.VMEM((D_FF, D_MODEL), BF16),
                        pltpu.SemaphoreType.DMA((2,))],
        compiler_params=_cparams(("arbitrary",)),
    )(zd, x, ya, yb, yc, p["w_out"], p["n2"], p["gf"], p["w_gu"], p["w_down"],
      *[p[n] for n in _WKV_PARAMS])


def _sample_mix_kernel(za_ref, zb_ref, zc_ref, zd_ref, shift_ref, s_ref, ssm_ref, conv_ref,
                       lng_ref, lnb_ref, w00_ref, b0_ref, lam_ref, bb_ref, cc_ref, d_ref, gw_ref,
                       gb_ref, cw_ref, cb_ref, *rest):
    prm = {n: ref[...] for n, ref in zip(_WKV_PARAMS, rest)}
    ya_ref, yb_ref, yc_ref, yd_ref, vn_ref, snew_ref, ssmnew_ref, convnew_ref = rest[len(_WKV_PARAMS):]

    za = za_ref[...]
    vn = _gm_norm(za[:, GROUP_W:], lng_ref[...], lnb_ref[...])
    vn_ref[...] = vn
    ya_ref[...] = jax.nn.gelu(za[:, :GROUP_W]) * (w00_ref[...] * vn + b0_ref[...])

    u = zb_ref[...]
    bu = jnp.dot(u.astype(BF16), bb_ref[...], preferred_element_type=F32)
    lam = lam_ref[...]
    lr, li = lam[0:1], lam[1:2]
    st = ssm_ref[...]
    s_re, s_im = st[:, :SSM_S], st[:, SSM_S:]
    st_new = jnp.concatenate([lr * s_re - li * s_im + bu[:, :SSM_S],
                              lr * s_im + li * s_re + bu[:, SSM_S:]], axis=1)
    ssmnew_ref[...] = st_new
    yb_ref[...] = _s5_output(st_new, u, cc_ref[...], d_ref[...], gw_ref[...], gb_ref[...])

    zc = zc_ref[...]
    z = zc[:, 2 * GROUP_W:] * zc[:, :GROUP_W]
    cw = cw_ref[...]
    buf = conv_ref[...]
    y = cb_ref[...] + cw[0:1] * buf[:, :GROUP_W] + cw[1:2] * buf[:, GROUP_W:] + cw[2:3] * z
    yc_ref[...] = zc[:, GROUP_W:2 * GROUP_W] * y
    convnew_ref[:, :GROUP_W] = buf[:, GROUP_W:]
    convnew_ref[:, GROUP_W:] = z

    bd256 = _bd_mask(GROUP_W)
    bd_ones = jnp.where(bd256, 1.0, 0.0).astype(BF16)
    zd = zd_ref[...]
    r, logd, k2, v, kk, a, g = _wkv_inputs(zd, shift_ref[...], prm, bd_ones)
    bt = zd.shape[0]
    s = s_ref[...]
    eye4 = (lax.broadcasted_iota(jnp.int32, (WKV_N, GROUP_W), 0)
            == lax.broadcasted_iota(jnp.int32, (WKV_N, GROUP_W), 1) % WKV_N)

    def head_sum(x3):
        return _dot_split(x3.reshape(bt * WKV_N, GROUP_W), bd_ones).reshape(bt, WKV_N, GROUP_W)

    sa = head_sum(s * (-kk)[:, None, :])
    vcol = head_sum(jnp.where(eye4[None], v[:, None, :], 0.0))
    s_new = (s * jnp.exp(logd)[:, None, :] + sa * (kk * a)[:, None, :] + vcol * k2[:, None, :])
    snew_ref[...] = s_new
    o_rep = head_sum(s_new * r[:, None, :])
    o = jnp.sum(jnp.where(eye4[None], o_rep, 0.0), axis=1)
    yd_ref[...] = _wkv_output(o, r, k2, v, g, prm, bd_ones)


def _sample_mix(za, zb, zc, zd, shift, s_t, ssm, conv, p, l, bt):
    rows = za.shape[0]
    row_blk = lambda wd: pl.BlockSpec((bt, wd), lambda i: (i, 0))
    st_blk = lambda wd: pl.BlockSpec((None, bt, wd), lambda i: (l, i, 0))
    s_blk = pl.BlockSpec((None, bt, WKV_N, GROUP_W), lambda i: (l, i, 0, 0))
    vec = _layer((1, GROUP_W), l)
    s_arg = 5
    return pl.pallas_call(
        _sample_mix_kernel,
        grid=(rows // bt,),
        in_specs=[row_blk(2 * GROUP_W), row_blk(GROUP_W), row_blk(3 * GROUP_W), row_blk(D_TM),
                  st_blk(D_TM), s_blk, st_blk(2 * SSM_S), st_blk(2 * GROUP_W),
                  vec, vec, vec, vec]
        + [_layer(s, l) for s in _S5_SHAPES]
        + [_layer((3, GROUP_W), l), vec] + _wkv_param_specs(l),
        out_specs=[row_blk(GROUP_W)] * 5 + [s_blk, row_blk(2 * SSM_S), row_blk(2 * GROUP_W)],
        out_shape=[jax.ShapeDtypeStruct((rows, GROUP_W), F32)] * 5
        + [jax.ShapeDtypeStruct(s_t.shape, F32),
           jax.ShapeDtypeStruct((rows, 2 * SSM_S), F32),
           jax.ShapeDtypeStruct((rows, 2 * GROUP_W), F32)],
        input_output_aliases={s_arg: 5},
        compiler_params=_cparams(("parallel",)),
    )(za, zb, zc, zd, shift, s_t, ssm, conv, p["lng"], p["lnb"], p["w00"], p["b0"],
      *[p[n] for n in _S5_PARAMS], p["cw"], p["cb"], *[p[n] for n in _WKV_PARAMS])


def _out_ffn_kernel(x_ref, ya_ref, yb_ref, yc_ref, yd_ref, wo_ref, g2_ref, wg_ref, wu_ref, wd_ref,
                    gf_ref, o_ref, h_ref, *, final, ff_split):
    c = pl.program_id(1)

    @pl.when(c == 0)
    def _():
        x = x_ref[...]
        for i, y_ref in enumerate((ya_ref, yb_ref, yc_ref, yd_ref)):
            x = x + jnp.dot(y_ref[...].astype(BF16), wo_ref[i * GROUP_W:(i + 1) * GROUP_W, :],
                            preferred_element_type=F32)
        o_ref[...] = x
        h_ref[...] = _rms(x, g2_ref[...]).astype(BF16)

    h = h_ref[...]
    gate = jnp.dot(h, wg_ref[...], preferred_element_type=F32)
    up = jnp.dot(h, wu_ref[...], preferred_element_type=F32)
    act = (gate * jax.nn.sigmoid(gate) * up).astype(BF16)
    o_ref[...] += jnp.dot(act, wd_ref[...], preferred_element_type=F32)

    if final:
        @pl.when(c == ff_split - 1)
        def _():
            o_ref[...] = _rms(o_ref[...], gf_ref[...])


def _out_ffn(x, ys, p, l, tm, final, ff_split):
    rows = x.shape[0]
    fc = D_FF // ff_split
    row_blk = lambda wd_: pl.BlockSpec((tm, wd_), lambda i, c: (i, 0))
    return pl.pallas_call(
        functools.partial(_out_ffn_kernel, final=final, ff_split=ff_split),
        grid=(rows // tm, ff_split),
        in_specs=[row_blk(D_MODEL)] + [row_blk(GROUP_W)] * 4
        + [_layer((D_MODEL, D_MODEL), l), _layer((1, D_MODEL), l),
           pl.BlockSpec((None, D_MODEL, fc), lambda i, c: (l, 0, c)),
           pl.BlockSpec((None, D_MODEL, fc), lambda i, c: (l, 0, ff_split + c)),
           pl.BlockSpec((None, fc, D_MODEL), lambda i, c: (l, c, 0)),
           _full((1, D_MODEL))],
        out_specs=row_blk(D_MODEL),
        out_shape=jax.ShapeDtypeStruct((rows, D_MODEL), F32),
        scratch_shapes=[pltpu.VMEM((tm, D_MODEL), BF16)],
        compiler_params=_cparams(("parallel", "arbitrary")),
    )(x, *ys, p["w_out"], p["n2"], p["w_gu"], p["w_gu"], p["w_down"], p["gf"])


def _pad_lora(w, start):
    return jnp.pad(w.astype(BF16), ((0, 0), (start, LORA_PAD - start - w.shape[1]), (0, 0)))


def kernel(x_prompt, x_sample, state_wkv, state_shift, state_ssm_re, state_ssm_im, state_conv,
           norm1_g, w_in, gm_ln_g, gm_ln_b, gm_ws, gm_bs,
           ssm_a_re, ssm_a_im, ssm_log_dt, ssm_b_re, ssm_b_im, ssm_c_re, ssm_c_im, ssm_d,
           ssm_glu_w, ssm_glu_b, conv_w, conv_b,
           tm_mu, tm_w0, tm_w2, tm_a0, tm_a2, tm_g2, tm_k_k, tm_k_a, tm_r_k, tm_ln_g, tm_ln_b,
           w_out, norm2_g, ffn_w_gu, ffn_w_down, norm_f_g):
    depth = w_in.shape[0]
    bp, seq, _ = x_prompt.shape
    bs = x_sample.shape[0]
    assert x_sample.shape[1] == 1 and seq % GM_CHUNK == 0 and seq % WKV_CHUNK == 0
    head_d = GROUP_W // GM_HEADS
    rowv = lambda p: p.reshape(depth, 1, -1)

    lam, bb, cc = _s5_prep(ssm_a_re, ssm_a_im, ssm_log_dt, ssm_b_re, ssm_b_im, ssm_c_re, ssm_c_im)
    p = {
        "n1": rowv(norm1_g), "n2": rowv(norm2_g), "w_in": w_in.astype(BF16),
        "lng": rowv(gm_ln_g), "lnb": rowv(gm_ln_b),
        "wcat": jnp.transpose(gm_ws, (0, 2, 1, 3)).reshape(depth, GM_CHUNK, GM_HEADS * GM_CHUNK),
        "bias": jnp.repeat(jnp.transpose(gm_bs, (0, 2, 1)), head_d, axis=2),
        "w00": rowv(jnp.repeat(gm_ws[:, :, 0, 0], head_d, axis=1)),
        "b0": rowv(jnp.repeat(gm_bs[:, :, 0], head_d, axis=1)),
        "lam": lam, "bb": bb, "cc": cc, "ssm_d": rowv(ssm_d),
        "glu_w": ssm_glu_w.astype(BF16), "glu_b": rowv(ssm_glu_b),
        "cw": conv_w, "cb": rowv(conv_b),
        "mu": rowv(tm_mu), "w0": rowv(tm_w0), "w2": _pad_lora(tm_w2, 0),
        "a0": rowv(tm_a0), "a2": _pad_lora(tm_a2, 32), "g2": _pad_lora(tm_g2, 64),
        "k_k": rowv(tm_k_k), "k_a": rowv(tm_k_a), "r_k": rowv(tm_r_k),
        "ln_g": rowv(tm_ln_g), "ln_b": rowv(tm_ln_b),
        "w_out": w_out.astype(BF16), "w_gu": ffn_w_gu.astype(BF16), "w_down": ffn_w_down.astype(BF16),
        "gf": norm_f_g.reshape(1, D_MODEL),
    }

    xp = x_prompt.reshape(bp * seq, D_MODEL)
    xs = x_sample.reshape(bs, D_MODEL)
    wkv_s_buf = jnp.transpose(state_wkv, (0, 1, 3, 2, 4)).reshape(depth, bs, WKV_N, GROUP_W)
    ssm_s_in = jnp.concatenate([state_ssm_re.reshape(depth, bs, SSM_S),
                                state_ssm_im.reshape(depth, bs, SSM_S)], axis=-1)
    conv_s_in = state_conv.reshape(depth, bs, 2 * GROUP_W)

    outs = {k: [] for k in ("wkv_p", "sh_p", "sh_s", "ssm_p", "ssm_s", "cv_p", "cv_s", "chv")}
    tm_p = ROW_TILE if seq % ROW_TILE == 0 else GM_CHUNK
    bt_s = SAMPLE_TILE if bs % SAMPLE_TILE == 0 else bs
    for l in range(depth):
        final = l == depth - 1

        ya, zb, yc, zd, tail = _inproj_mix(xp, p, l, bp, seq, tm_p)
        per_batch = lambda y: y.reshape(bp, seq, y.shape[-1])
        yb, ssm_fin = _s5(per_batch(zb), p, l, bp, seq)
        xp3, wkv_fin = _rwkv_ffn(per_batch(zd), per_batch(xp), per_batch(ya), yb, per_batch(yc),
                                 p, l, bp, seq, final)
        xp = xp3.reshape(bp * seq, D_MODEL)
        outs["wkv_p"].append(wkv_fin)
        outs["sh_p"].append(zd.reshape(bp, seq, D_TM)[:, -1])
        outs["ssm_p"].append(ssm_fin)
        outs["cv_p"].append(tail[:, 6:8])

        za, zb, zc, zd = _inproj(xs, p["n1"], p["w_in"], l)
        ya, yb, yc, yd, vn, wkv_s_buf, ssm_new, conv_new = _sample_mix(
            za, zb, zc, zd, state_shift, wkv_s_buf, ssm_s_in, conv_s_in, p, l, bt_s)
        xs = _out_ffn(xs, (ya, yb, yc, yd), p, l, bs, final, FF_SPLIT_SAMPLE)
        outs["sh_s"].append(zd)
        outs["ssm_s"].append(ssm_new)
        outs["cv_s"].append(conv_new.reshape(bs, 2, GROUP_W))
        outs["chv"].append(vn.reshape(bs, 1, GROUP_W))

    def wkv_blocks(s_bd):
        s5d = s_bd.reshape(depth, -1, WKV_HEADS, WKV_N, WKV_HEADS, WKV_N)
        return jnp.stack([s5d[:, :, h, :, h, :] for h in range(WKV_HEADS)], axis=2)

    wkv_p = wkv_blocks(jnp.stack(outs["wkv_p"]))
    wkv_s = jnp.transpose(wkv_s_buf.reshape(depth, bs, WKV_N, WKV_HEADS, WKV_N), (0, 1, 3, 2, 4))
    ssm_p = jnp.stack(outs["ssm_p"])
    ssm_s = jnp.stack(outs["ssm_s"])
    split = lambda s, i: s[..., i * SSM_S:(i + 1) * SSM_S].reshape(depth, -1, SSM_GROUPS, SSM_P)
    return (xp.reshape(bp, seq, D_MODEL), xs.reshape(bs, 1, D_MODEL),
            wkv_p, wkv_s,
            jnp.stack(outs["sh_p"]), jnp.stack(outs["sh_s"]),
            split(ssm_p, 0), split(ssm_s, 0), split(ssm_p, 1), split(ssm_s, 1),
            jnp.stack(outs["cv_p"]), jnp.stack(outs["cv_s"]),
            jnp.stack(outs["chv"]))
```

```python
import functools

import jax
import jax.numpy as jnp
from jax import lax
from jax.experimental import pallas as pl
from jax.experimental.pallas import tpu as pltpu

F32 = jnp.float32
BF16 = jnp.bfloat16

D_MODEL = 1024
GROUP_W = 256
GM_CHUNK = 128
GM_HEADS = 4
SSM_CH = 16
SSM_GROUPS = 16
SSM_P = 64
SSM_S = SSM_GROUPS * SSM_P
WKV_N = 64
WKV_HEADS = 4
LORA_PAD = 128
D_TM = 3 * GROUP_W + LORA_PAD
IN_COLS = 6 * GROUP_W + D_TM
D_FF = 2816
FF_SPLIT_SAMPLE = 11
NORM_EPS = 1e-6
GM_LN_EPS = 1e-5
WKV_LN_EPS = 64e-5

WKV_CHUNK = 64
S5_TCHUNK = 128
ROW_TILE = 512
SAMPLE_TILE = 32
VMEM_LIMIT = 56 * 1024 * 1024


def _cparams(sem):
    return pltpu.CompilerParams(dimension_semantics=sem, vmem_limit_bytes=VMEM_LIMIT)


def _full(shape):
    n = len(shape)
    return pl.BlockSpec(shape, lambda *_: (0,) * n)


def _layer(shape, l):
    n = len(shape)
    return pl.BlockSpec((None,) + tuple(shape), lambda *_: (l,) + (0,) * n)


def _dot(a, b):
    return jnp.dot(a.astype(BF16), b.astype(BF16), preferred_element_type=F32)


def _dot_nt(a, b):
    return lax.dot_general(a.astype(BF16), b.astype(BF16), (((1,), (1,)), ((), ())),
                           preferred_element_type=F32)


def _dot_tn(a, b):
    return lax.dot_general(a.astype(BF16), b.astype(BF16), (((0,), (0,)), ((), ())),
                           preferred_element_type=F32)


def _dot_split(x, ones_bf16):
    hi = x.astype(BF16)
    lo = (x - hi.astype(F32)).astype(BF16)
    return (jnp.dot(hi, ones_bf16, preferred_element_type=F32)
            + jnp.dot(lo, ones_bf16, preferred_element_type=F32))


def _rms(x, g):
    return x * lax.rsqrt(jnp.mean(x * x, axis=-1, keepdims=True) + NORM_EPS) * g


def _softplus(y):
    return jnp.maximum(y, 0.0) + jnp.log1p(jnp.exp(-jnp.abs(y)))


def _gm_norm(zav, ln_g, ln_b):
    vf = jax.nn.gelu(zav)
    mu = jnp.mean(vf, axis=-1, keepdims=True)
    var = jnp.mean(jnp.square(vf - mu), axis=-1, keepdims=True)
    return (vf - mu) * lax.rsqrt(var + GM_LN_EPS) * ln_g + ln_b


def _inproj_kernel(x_ref, g_ref, w_ref, za_ref, zb_ref, zc_ref, zd_ref):
    h = _rms(x_ref[...], g_ref[...])
    z = jnp.dot(h.astype(BF16), w_ref[...], preferred_element_type=F32)
    za_ref[...] = z[:, 0:2 * GROUP_W]
    zb_ref[...] = z[:, 2 * GROUP_W:3 * GROUP_W]
    zc_ref[...] = z[:, 3 * GROUP_W:6 * GROUP_W]
    zd_ref[...] = z[:, 6 * GROUP_W:]


def _inproj(x, g, w, l):
    rows = x.shape[0]
    widths = (2 * GROUP_W, GROUP_W, 3 * GROUP_W, D_TM)
    return pl.pallas_call(
        _inproj_kernel,
        grid=(1,),
        in_specs=[_full((rows, D_MODEL)), _layer((1, D_MODEL), l), _layer((D_MODEL, IN_COLS), l)],
        out_specs=[_full((rows, wd)) for wd in widths],
        out_shape=[jax.ShapeDtypeStruct((rows, wd), F32) for wd in widths],
        compiler_params=_cparams(("arbitrary",)),
    )(x, g, w)


def _inproj_mix_kernel(x_ref, g_ref, w_ref, lng_ref, lnb_ref, wcat_ref, bias_ref, cw_ref, cb_ref,
                       ya_ref, zb_ref, yc_ref, zd_ref, tail_ref, prev_ref, *, tile):
    @pl.when(pl.program_id(1) == 0)
    def _():
        prev_ref[...] = jnp.zeros_like(prev_ref)

    h = _rms(x_ref[...], g_ref[...])
    z = jnp.dot(h.astype(BF16), w_ref[...], preferred_element_type=F32)
    zb_ref[...] = z[:, 2 * GROUP_W:3 * GROUP_W]
    zd_ref[...] = z[:, 6 * GROUP_W:]

    u = jax.nn.gelu(z[:, :GROUP_W])
    vn = _gm_norm(z[:, GROUP_W:2 * GROUP_W], lng_ref[...], lnb_ref[...])
    kc = GM_HEADS * GM_CHUNK
    t_i = lax.broadcasted_iota(jnp.int32, (GM_CHUNK, kc), 0)
    s_i = lax.broadcasted_iota(jnp.int32, (GM_CHUNK, kc), 1) % GM_CHUNK
    wm = jnp.where(s_i <= t_i, wcat_ref[...], 0.0).astype(BF16)
    r_h = lax.broadcasted_iota(jnp.int32, (kc, GROUP_W), 0) // GM_CHUNK
    c_h = lax.broadcasted_iota(jnp.int32, (kc, GROUP_W), 1) // (GROUP_W // GM_HEADS)
    head_mask = r_h == c_h
    for c in range(tile // GM_CHUNK):
        rows = slice(c * GM_CHUNK, (c + 1) * GM_CHUNK)
        vc = vn[rows].astype(BF16)
        rhs = jnp.where(head_mask, jnp.concatenate([vc] * GM_HEADS, axis=0), jnp.zeros((), BF16))
        s = jnp.dot(wm, rhs, preferred_element_type=F32) + bias_ref[...]
        ya_ref[rows, :] = (u[rows] * s).astype(ya_ref.dtype)

    zz = z[:, 5 * GROUP_W:6 * GROUP_W] * z[:, 3 * GROUP_W:4 * GROUP_W]
    row = lax.broadcasted_iota(jnp.int32, zz.shape, 0)
    prev = prev_ref[...]
    z1 = jnp.where(row == 0, prev[7:8], pltpu.roll(zz, 1, 0))
    z2 = jnp.where(row == 0, prev[6:7], jnp.where(row == 1, prev[7:8], pltpu.roll(zz, 2, 0)))
    cw = cw_ref[...]
    y = cb_ref[...] + cw[0:1] * z2 + cw[1:2] * z1 + cw[2:3] * zz
    yc_ref[...] = (z[:, 4 * GROUP_W:5 * GROUP_W] * y).astype(yc_ref.dtype)
    prev_ref[...] = zz[tile - 8:]
    tail_ref[...] = zz[tile - 8:]


def _inproj_mix(x, p, l, batch, seq, tile):
    nt = seq // tile
    rows = batch * seq
    row_blk = lambda wd: pl.BlockSpec((tile, wd), lambda b, j: (b * nt + j, 0))
    widths = (GROUP_W, GROUP_W, GROUP_W, D_TM)
    return pl.pallas_call(
        functools.partial(_inproj_mix_kernel, tile=tile),
        grid=(batch, nt),
        in_specs=[row_blk(D_MODEL), _layer((1, D_MODEL), l), _layer((D_MODEL, IN_COLS), l),
                  _layer((1, GROUP_W), l), _layer((1, GROUP_W), l),
                  _layer((GM_CHUNK, GM_HEADS * GM_CHUNK), l), _layer((GM_CHUNK, GROUP_W), l),
                  _layer((3, GROUP_W), l), _layer((1, GROUP_W), l)],
        out_specs=[row_blk(wd) for wd in widths]
        + [pl.BlockSpec((None, 8, GROUP_W), lambda b, j: (b, 0, 0))],
        out_shape=[jax.ShapeDtypeStruct((rows, wd), dt) for wd, dt in zip(widths, (BF16, F32, BF16, F32))]
        + [jax.ShapeDtypeStruct((batch, 8, GROUP_W), F32)],
        scratch_shapes=[pltpu.VMEM((8, GROUP_W), F32)],
        compiler_params=_cparams(("parallel", "arbitrary")),
    )(x, p["n1"], p["w_in"], p["lng"], p["lnb"], p["wcat"], p["bias"], p["cw"], p["cb"])


def _s5_prep_kernel(are_ref, aim_ref, ldt_ref, bre_ref, bim_ref, cre_ref, cim_ref,
                    lam_ref, bb_ref, cc_ref):
    lam_re = jnp.minimum(are_ref[...], -1e-4)
    lam_im = aim_ref[...]
    dt = jnp.exp(ldt_ref[...])
    mag = jnp.exp(lam_re * dt)
    lb_re = mag * jnp.cos(lam_im * dt)
    lb_im = mag * jnp.sin(lam_im * dt)
    den = lam_re * lam_re + lam_im * lam_im
    f_re = ((lb_re - 1.0) * lam_re + lb_im * lam_im) / den
    f_im = (lb_im * lam_re - (lb_re - 1.0) * lam_im) / den
    lam_ref[0:1, :] = lb_re
    lam_ref[1:2, :] = lb_im
    br, bi = bre_ref[...], bim_ref[...]
    grp_r = lax.broadcasted_iota(jnp.int32, (GROUP_W, SSM_S), 0) // SSM_CH
    grp_c = lax.broadcasted_iota(jnp.int32, (GROUP_W, SSM_S), 1) // SSM_P
    m = grp_r == grp_c
    bb_ref[:, :SSM_S] = jnp.where(m, f_re * br - f_im * bi, 0.0).astype(BF16)
    bb_ref[:, SSM_S:] = jnp.where(m, f_re * bi + f_im * br, 0.0).astype(BF16)
    grp_r2 = lax.broadcasted_iota(jnp.int32, (SSM_S, GROUP_W), 0) // SSM_P
    grp_c2 = lax.broadcasted_iota(jnp.int32, (SSM_S, GROUP_W), 1) // SSM_CH
    m2 = grp_r2 == grp_c2
    cc_ref[:SSM_S, :] = jnp.where(m2, cre_ref[...], 0.0).astype(BF16)
    cc_ref[SSM_S:, :] = jnp.where(m2, -cim_ref[...], 0.0).astype(BF16)


def _s5_prep(a_re, a_im, log_dt, b_re, b_im, c_re, c_im):
    depth = a_re.shape[0]
    flat = lambda p: p.reshape(depth, 1, SSM_S)
    b_exp = lambda b: jnp.tile(jnp.transpose(b, (0, 3, 1, 2)).reshape(depth, SSM_CH, SSM_S),
                               (1, SSM_GROUPS, 1))
    c_exp = lambda c: jnp.tile(jnp.transpose(c, (0, 1, 3, 2)).reshape(depth, SSM_S, SSM_CH),
                               (1, 1, SSM_GROUPS))
    lyr = lambda shape: pl.BlockSpec((None,) + shape, lambda l: (l, 0, 0))
    return pl.pallas_call(
        _s5_prep_kernel,
        grid=(depth,),
        in_specs=[lyr((1, SSM_S))] * 3 + [lyr((GROUP_W, SSM_S))] * 2 + [lyr((SSM_S, GROUP_W))] * 2,
        out_specs=[lyr((2, SSM_S)), lyr((GROUP_W, 2 * SSM_S)), lyr((2 * SSM_S, GROUP_W))],
        out_shape=[jax.ShapeDtypeStruct((depth, 2, SSM_S), F32),
                   jax.ShapeDtypeStruct((depth, GROUP_W, 2 * SSM_S), BF16),
                   jax.ShapeDtypeStruct((depth, 2 * SSM_S, GROUP_W), BF16)],
        compiler_params=_cparams(("arbitrary",)),
    )(flat(a_re), flat(a_im), flat(log_dt), b_exp(b_re), b_exp(b_im), c_exp(c_re), c_exp(c_im))


def _s5_output(st, u, cc, d, glu_w, glu_b):
    y = jnp.dot(st.astype(BF16), cc, preferred_element_type=F32) + d * u
    y = jax.nn.gelu(y)
    return y * jax.nn.sigmoid(jnp.dot(y.astype(BF16), glu_w, preferred_element_type=F32) + glu_b)


_S5_PARAMS = ("lam", "bb", "cc", "ssm_d", "glu_w", "glu_b")
_S5_SHAPES = ((2, SSM_S), (GROUP_W, 2 * SSM_S), (2 * SSM_S, GROUP_W), (1, GROUP_W),
              (GROUP_W, GROUP_W), (1, GROUP_W))


S5_SLABS = 2 * SSM_S // 128
S5_PITCH = S5_TCHUNK + 8


def _s5_kernel(u_ref, lam_ref, bb_ref, cc_ref, d_ref, gw_ref, gb_ref, y_ref, fin_ref,
               bu_ref, st_ref, *, batch, tsteps):
    half = S5_SLABS // 2

    @pl.when(pl.program_id(0) == 0)
    def _():
        st_ref[...] = jnp.zeros_like(st_ref)

    u = u_ref[...].reshape(batch * tsteps, GROUP_W)
    bu = jnp.dot(u.astype(BF16), bb_ref[...], preferred_element_type=F32)
    for b in range(batch):
        for s in range(S5_SLABS):
            bu_ref[s, b * S5_PITCH:b * S5_PITCH + tsteps, :] = (
                bu[b * tsteps:(b + 1) * tsteps, s * 128:(s + 1) * 128])
    lam = lam_ref[...]
    lr = [jnp.broadcast_to(lam[0:1, s * 128:(s + 1) * 128], (batch, 128)) for s in range(half)]
    li = [jnp.broadcast_to(lam[1:2, s * 128:(s + 1) * 128], (batch, 128)) for s in range(half)]

    def step(t, carry):
        rows = pl.ds(t, batch, stride=S5_PITCH)
        new = [None] * S5_SLABS
        for s in range(half):
            s_re, s_im = carry[s], carry[half + s]
            new[s] = lr[s] * s_re - li[s] * s_im + bu_ref[s, rows, :]
            new[half + s] = lr[s] * s_im + li[s] * s_re + bu_ref[half + s, rows, :]
            bu_ref[s, rows, :] = new[s]
            bu_ref[half + s, rows, :] = new[half + s]
        return tuple(new)

    st0 = st_ref[...]
    fin = lax.fori_loop(0, tsteps, step,
                        tuple(st0[:, s * 128:(s + 1) * 128] for s in range(S5_SLABS)), unroll=4)
    for s in range(S5_SLABS):
        st_ref[:, s * 128:(s + 1) * 128] = fin[s]
        fin_ref[:, s * 128:(s + 1) * 128] = fin[s]
    st = jnp.concatenate(
        [jnp.concatenate([bu_ref[s, b * S5_PITCH:b * S5_PITCH + tsteps, :] for s in range(S5_SLABS)],
                         axis=1) for b in range(batch)], axis=0)
    y = _s5_output(st, u, cc_ref[...], d_ref[...], gw_ref[...], gb_ref[...])
    y_ref[...] = y.reshape(batch, tsteps, GROUP_W).astype(y_ref.dtype)


def _s5(u, p, l, batch, seq):
    tsteps = min(S5_TCHUNK, seq)
    blk = pl.BlockSpec((batch, tsteps, GROUP_W), lambda i: (0, i, 0))
    return pl.pallas_call(
        functools.partial(_s5_kernel, batch=batch, tsteps=tsteps),
        grid=(seq // tsteps,),
        in_specs=[blk] + [_layer(s, l) for s in _S5_SHAPES],
        out_specs=[blk, _full((batch, 2 * SSM_S))],
        out_shape=[jax.ShapeDtypeStruct((batch, seq, GROUP_W), BF16),
                   jax.ShapeDtypeStruct((batch, 2 * SSM_S), F32)],
        scratch_shapes=[pltpu.VMEM((S5_SLABS, batch * S5_PITCH, 128), F32),
                        pltpu.VMEM((batch, 2 * SSM_S), F32)],
        compiler_params=_cparams(("arbitrary",)),
    )(u, *[p[n] for n in _S5_PARAMS])


def _wkv_inputs(zd, zprev, p, bd_ones):
    zs = zd + p["mu"] * (zprev - zd)
    r = zs[:, 0:GROUP_W]
    k = zs[:, GROUP_W:2 * GROUP_W]
    v = zs[:, 2 * GROUP_W:3 * GROUP_W]
    lora = zs[:, 3 * GROUP_W:]
    w = -_softplus(-(p["w0"] + _dot(jnp.tanh(lora), p["w2"]))) - 0.5
    logd = -jnp.exp(w)
    a = jax.nn.sigmoid(p["a0"] + _dot(lora, p["a2"]))
    g = _dot(jax.nn.sigmoid(lora), p["g2"])
    kk = k * p["k_k"]
    nrm = jnp.sqrt(_dot_split(kk * kk, bd_ones))
    kk = kk / jnp.maximum(nrm, 1e-12)
    k2 = k * (1.0 + (a - 1.0) * p["k_a"])
    return r, logd, k2, v, kk, a, g


def _wkv_output(o, r, k2, v, g, p, bd_ones):
    inv_n = 1.0 / WKV_N
    m = _dot_split(o, bd_ones) * inv_n
    var = _dot_split(jnp.square(o - m), bd_ones) * inv_n
    on = (o - m) * lax.rsqrt(var + WKV_LN_EPS) * p["ln_g"] + p["ln_b"]
    bonus = _dot_split(r * k2 * p["r_k"], bd_ones) * v
    return (on + bonus) * g


_WKV_PARAMS = ("mu", "w0", "w2", "a0", "a2", "g2", "k_k", "k_a", "r_k", "ln_g", "ln_b")
_WKV_PARAM_SHAPES = {"mu": (1, D_TM), "w2": (LORA_PAD, GROUP_W), "a2": (LORA_PAD, GROUP_W),
                     "g2": (LORA_PAD, GROUP_W)}


def _wkv_param_specs(l):
    return [_layer(_WKV_PARAM_SHAPES.get(n, (1, GROUP_W)), l) for n in _WKV_PARAMS]


def _bd_mask(n):
    hr = lax.broadcasted_iota(jnp.int32, (n, n), 0) // (n // WKV_HEADS)
    hc = lax.broadcasted_iota(jnp.int32, (n, n), 1) // (n // WKV_HEADS)
    return hr == hc


def _expand(xp, lo_bf16):
    xb = xp.astype(BF16)
    zero = jnp.zeros_like(lo_bf16)
    hi_bf16 = 1 - lo_bf16
    t0, t1 = xb[:, :128], xb[:, 128:]
    return jnp.concatenate([jnp.concatenate([t0 * lo_bf16, zero], axis=1),
                            jnp.concatenate([t0 * hi_bf16, zero], axis=1),
                            jnp.concatenate([zero, t1 * lo_bf16], axis=1),
                            jnp.concatenate([zero, t1 * hi_bf16], axis=1)], axis=0)


def _interleave(*stage_gens):
    results = [None] * len(stage_gens)
    live = list(range(len(stage_gens)))
    while live:
        for i in list(live):
            try:
                next(stage_gens[i])
            except StopIteration as stop:
                results[i] = stop.value
                live.remove(i)
    return results


def _wkv_chunk_stages(r, logd, k2, v, kk, a, states, tri_ones, bd256):
    c = WKV_CHUNK
    n = len(states)
    rows = [slice(i * c, (i + 1) * c) for i in range(n)]
    t_i = lax.broadcasted_iota(jnp.int32, (c, GROUP_W), 0)
    s_i = lax.broadcasted_iota(jnp.int32, (c, GROUP_W), 1) % c
    strict = s_i < t_i
    incl = s_i <= t_i
    lo_bf16 = jnp.where(lax.broadcasted_iota(jnp.int32, (c, 128), 1) < WKV_N, 1.0, 0.0).astype(BF16)
    ex = lambda xp: _expand(xp, lo_bf16)

    cum = [jnp.dot(tri_ones, logd[rw], preferred_element_type=F32, precision=lax.Precision.HIGHEST)
           for rw in rows]
    yield
    g_last = [cm[c - 1:c] for cm in cum]
    a_t, r_t, b_t, k_t, b_h, k_h, v_c = [], [], [], [], [], [], []
    for i, rw in enumerate(rows):
        e_neg = jnp.exp(-cum[i])
        e_end = jnp.exp(g_last[i] - cum[i])
        bvec = kk[rw] * a[rw]
        a_t.append(-kk[rw] * jnp.exp(cum[i] - logd[rw]))
        r_t.append(r[rw] * jnp.exp(cum[i]))
        b_t.append(bvec * e_neg)
        k_t.append(k2[rw] * e_neg)
        b_h.append(bvec * e_end)
        k_h.append(k2[rw] * e_end)
        v_c.append(v[rw])
    yield

    ar = [jnp.concatenate([a_t[i], r_t[i]], axis=0) for i in range(n)]
    p_b = [_dot_nt(ar[i], ex(b_t[i])) for i in range(n)]
    yield
    p_k = [_dot_nt(ar[i], ex(k_t[i])) for i in range(n)]
    yield
    l_p = [jnp.where(strict, p[:c], 0.0) for p in p_b]
    aak = [jnp.where(strict, p[:c], 0.0) for p in p_k]
    rb = [jnp.where(incl, p[c:], 0.0) for p in p_b]
    rk = [jnp.where(incl, p[c:], 0.0) for p in p_k]

    def off_diag(m):
        return (t_i // (2 * m) == s_i // (2 * m)) & (t_i % (2 * m) >= m) & (s_i % (2 * m) < m)

    eye_p = jnp.where(s_i == t_i, 1.0, 0.0)
    t_p = [eye_p + jnp.where(off_diag(1), l_p[i], 0.0) for i in range(n)]
    m = 2
    while m < c:
        off = off_diag(m)
        x = [_dot(t_p[i], ex(jnp.where(off, l_p[i], 0.0))) for i in range(n)]
        yield
        t_p = [t_p[i] + _dot(x[i], ex(t_p[i])) for i in range(n)]
        yield
        m *= 2

    v_bd = [ex(v_c[i]) for i in range(n)]
    a2 = [_dot(t_p[i], ex(a_t[i])) for i in range(n)]
    av = [_dot(aak[i], v_bd[i]) for i in range(n)]
    yield
    w0 = [_dot(t_p[i], ex(av[i])) for i in range(n)]
    from_s = [_dot_nt(jnp.concatenate([a2[i], r_t[i]], axis=0), states[i]) for i in range(n)]
    yield
    w = [from_s[i][:c] + w0[i] for i in range(n)]
    o = [from_s[i][c:] + _dot(jnp.concatenate([rb[i], rk[i]], axis=1),
                              jnp.concatenate([ex(w[i]), v_bd[i]], axis=0)) for i in range(n)]
    yield
    upd = [_dot_tn(jnp.concatenate([w[i], v_c[i]], axis=0), jnp.concatenate([b_h[i], k_h[i]], axis=0))
           for i in range(n)]
    s_new = [states[i] * jnp.exp(g_last[i]) + jnp.where(bd256, upd[i], 0.0) for i in range(n)]
    return jnp.concatenate(o, axis=0), s_new


FF_CHUNK = 256


def _ffn_stages(x, ys, wo_ref, n2, wgu_ref, wd_ref, acc_ref):
    for i, y in enumerate(ys):
        x = x + jnp.dot(y.astype(BF16), wo_ref[i * GROUP_W:(i + 1) * GROUP_W, :],
                        preferred_element_type=F32)
    acc_ref[...] = x
    yield
    h = _rms(x, n2).astype(BF16)
    for c in range(D_FF // FF_CHUNK):
        gate = jnp.dot(h, wgu_ref[:, c * FF_CHUNK:(c + 1) * FF_CHUNK], preferred_element_type=F32)
        up = jnp.dot(h, wgu_ref[:, D_FF + c * FF_CHUNK:D_FF + (c + 1) * FF_CHUNK],
                     preferred_element_type=F32)
        act = (gate * jax.nn.sigmoid(gate) * up).astype(BF16)
        acc_ref[...] += jnp.dot(act, wd_ref[c * FF_CHUNK:(c + 1) * FF_CHUNK, :], preferred_element_type=F32)
        yield


def _rwkv_ffn_kernel(zd_ref, x_ref, ya_ref, yb_ref, yc_ref, wo_ref, n2_ref, gf_ref, wgu_hbm, wd_hbm,
                     *rest, nb, nt, l, final):
    prm = {n: ref[...] for n, ref in zip(_WKV_PARAMS, rest)}
    o_ref, sfin_ref, prev_ref, s_ref, yd_ref, acc_ref, wgu_ref, wd_ref, sem = rest[len(_WKV_PARAMS):]
    c = WKV_CHUNK
    j = pl.program_id(0)

    def weight_copies():
        return (pltpu.make_async_copy(wgu_hbm.at[l], wgu_ref, sem.at[0]),
                pltpu.make_async_copy(wd_hbm.at[l], wd_ref, sem.at[1]))

    def rwkv_stages():
        bd256 = _bd_mask(GROUP_W)
        bd_ones = jnp.where(bd256, 1.0, 0.0).astype(BF16)
        tri_ones = jnp.where(lax.broadcasted_iota(jnp.int32, (c, c), 1)
                             <= lax.broadcasted_iota(jnp.int32, (c, c), 0), 1.0, 0.0)
        zd3 = zd_ref[...]
        zd = zd3.reshape(nb * c, D_TM)
        first = lax.broadcasted_iota(jnp.int32, (nb, c, D_TM), 1) == 0
        carried = jnp.broadcast_to(prev_ref[:, 7:8, :], (nb, c, D_TM))
        zprev = jnp.where(first, carried, pltpu.roll(zd, 1, 0).reshape(nb, c, D_TM)).reshape(nb * c, D_TM)
        r, logd, k2, v, kk, a, g = _wkv_inputs(zd, zprev, prm, bd_ones)
        yield
        o, s_new = yield from _wkv_chunk_stages(r, logd, k2, v, kk, a, [s_ref[b] for b in range(nb)],
                                                tri_ones, bd256)
        yield
        yd_ref[...] = _wkv_output(o, r, k2, v, g, prm, bd_ones)
        for b in range(nb):
            s_ref[b] = s_new[b]
            sfin_ref[b] = s_new[b]
        prev_ref[...] = zd3[:, c - 8:, :]

    def ffn_stages():
        flat = lambda ref: ref[...].reshape(nb * c, ref.shape[-1])
        ys = (flat(ya_ref), flat(yb_ref), flat(yc_ref), yd_ref[...])
        yield from _ffn_stages(flat(x_ref), ys, wo_ref, n2_ref[...], wgu_ref, wd_ref, acc_ref)
        x = acc_ref[...]
        if final:
            x = _rms(x, gf_ref[...])
        o_ref[...] = x.reshape(nb, c, D_MODEL)

    @pl.when(j == 0)
    def _():
        prev_ref[...] = jnp.zeros_like(prev_ref)
        s_ref[...] = jnp.zeros_like(s_ref)
        for cp in weight_copies():
            cp.start()
        _interleave(rwkv_stages())
        for cp in weight_copies():
            cp.wait()

    @pl.when((j > 0) & (j < nt))
    def _():
        _interleave(ffn_stages(), rwkv_stages())

    @pl.when(j == nt)
    def _():
        _interleave(ffn_stages())


def _rwkv_ffn(zd, x, ya, yb, yc, p, l, batch, seq, final):
    c = WKV_CHUNK
    nt = seq // c
    cur = lambda wd: pl.BlockSpec((batch, c, wd), lambda j: (0, jnp.minimum(j, nt - 1), 0))
    prv = lambda wd: pl.BlockSpec((batch, c, wd), lambda j: (0, jnp.maximum(j - 1, 0), 0))
    return pl.pallas_call(
        functools.partial(_rwkv_ffn_kernel, nb=batch, nt=nt, l=l, final=final),
        grid=(nt + 1,),
        in_specs=[cur(D_TM), prv(D_MODEL), prv(GROUP_W), prv(GROUP_W), prv(GROUP_W),
                  _layer((D_MODEL, D_MODEL), l), _layer((1, D_MODEL), l), _full((1, D_MODEL)),
                  pl.BlockSpec(memory_space=pl.ANY), pl.BlockSpec(memory_space=pl.ANY)]
        + _wkv_param_specs(l),
        out_specs=[prv(D_MODEL), _full((batch, GROUP_W, GROUP_W))],
        out_shape=[jax.ShapeDtypeStruct((batch, seq, D_MODEL), F32),
                   jax.ShapeDtypeStruct((batch, GROUP_W, GROUP_W), F32)],
        scratch_shapes=[pltpu.VMEM((batch, 8, D_TM), F32),
                        pltpu.VMEM((batch, GROUP_W, GROUP_W), F32),
                        pltpu.VMEM((batch * c, GROUP_W), F32),
                        pltpu.VMEM((batch * c, D_MODEL), F32),
                        pltpu.VMEM((D_MODEL, 2 * D_FF), BF16),
                        pltpu.VMEM((D_FF, D_MODEL), BF16),
                        pltpu.SemaphoreType.DMA((2,))],
        compiler_params=_cparams(("arbitrary",)),
    )(zd, x, ya, yb, yc, p["w_out"], p["n2"], p["gf"], p["w_gu"], p["w_down"],
      *[p[n] for n in _WKV_PARAMS])


def _sample_mix_kernel(za_ref, zb_ref, zc_ref, zd_ref, shift_ref, s_ref, ssm_ref, conv_ref,
                       lng_ref, lnb_ref, w00_ref, b0_ref, lam_ref, bb_ref, cc_ref, d_ref, gw_ref,
                       gb_ref, cw_ref, cb_ref, *rest):
    prm = {n: ref[...] for n, ref in zip(_WKV_PARAMS, rest)}
    ya_ref, yb_ref, yc_ref, yd_ref, vn_ref, snew_ref, ssmnew_ref, convnew_ref = rest[len(_WKV_PARAMS):]

    za = za_ref[...]
    vn = _gm_norm(za[:, GROUP_W:], lng_ref[...], lnb_ref[...])
    vn_ref[...] = vn
    ya_ref[...] = jax.nn.gelu(za[:, :GROUP_W]) * (w00_ref[...] * vn + b0_ref[...])

    u = zb_ref[...]
    bu = jnp.dot(u.astype(BF16), bb_ref[...], preferred_element_type=F32)
    lam = lam_ref[...]
    lr, li = lam[0:1], lam[1:2]
    st = ssm_ref[...]
    s_re, s_im = st[:, :SSM_S], st[:, SSM_S:]
    st_new = jnp.concatenate([lr * s_re - li * s_im + bu[:, :SSM_S],
                              lr * s_im + li * s_re + bu[:, SSM_S:]], axis=1)
    ssmnew_ref[...] = st_new
    yb_ref[...] = _s5_output(st_new, u, cc_ref[...], d_ref[...], gw_ref[...], gb_ref[...])

    zc = zc_ref[...]
    z = zc[:, 2 * GROUP_W:] * zc[:, :GROUP_W]
    cw = cw_ref[...]
    buf = conv_ref[...]
    y = cb_ref[...] + cw[0:1] * buf[:, :GROUP_W] + cw[1:2] * buf[:, GROUP_W:] + cw[2:3] * z
    yc_ref[...] = zc[:, GROUP_W:2 * GROUP_W] * y
    convnew_ref[:, :GROUP_W] = buf[:, GROUP_W:]
    convnew_ref[:, GROUP_W:] = z

    bd256 = _bd_mask(GROUP_W)
    bd_ones = jnp.where(bd256, 1.0, 0.0).astype(BF16)
    zd = zd_ref[...]
    r, logd, k2, v, kk, a, g = _wkv_inputs(zd, shift_ref[...], prm, bd_ones)
    bt = zd.shape[0]
    s = s_ref[...]
    eye4 = (lax.broadcasted_iota(jnp.int32, (WKV_N, GROUP_W), 0)
            == lax.broadcasted_iota(jnp.int32, (WKV_N, GROUP_W), 1) % WKV_N)

    def head_sum(x3):
        return _dot_split(x3.reshape(bt * WKV_N, GROUP_W), bd_ones).reshape(bt, WKV_N, GROUP_W)

    sa = head_sum(s * (-kk)[:, None, :])
    vcol = head_sum(jnp.where(eye4[None], v[:, None, :], 0.0))
    s_new = (s * jnp.exp(logd)[:, None, :] + sa * (kk * a)[:, None, :] + vcol * k2[:, None, :])
    snew_ref[...] = s_new
    o_rep = head_sum(s_new * r[:, None, :])
    o = jnp.sum(jnp.where(eye4[None], o_rep, 0.0), axis=1)
    yd_ref[...] = _wkv_output(o, r, k2, v, g, prm, bd_ones)


def _sample_mix(za, zb, zc, zd, shift, s_t, ssm, conv, p, l, bt):
    rows = za.shape[0]
    row_blk = lambda wd: pl.BlockSpec((bt, wd), lambda i: (i, 0))
    st_blk = lambda wd: pl.BlockSpec((None, bt, wd), lambda i: (l, i, 0))
    s_blk = pl.BlockSpec((None, bt, WKV_N, GROUP_W), lambda i: (l, i, 0, 0))
    vec = _layer((1, GROUP_W), l)
    s_arg = 5
    return pl.pallas_call(
        _sample_mix_kernel,
        grid=(rows // bt,),
        in_specs=[row_blk(2 * GROUP_W), row_blk(GROUP_W), row_blk(3 * GROUP_W), row_blk(D_TM),
                  st_blk(D_TM), s_blk, st_blk(2 * SSM_S), st_blk(2 * GROUP_W),
                  vec, vec, vec, vec]
        + [_layer(s, l) for s in _S5_SHAPES]
        + [_layer((3, GROUP_W), l), vec] + _wkv_param_specs(l),
        out_specs=[row_blk(GROUP_W)] * 5 + [s_blk, row_blk(2 * SSM_S), row_blk(2 * GROUP_W)],
        out_shape=[jax.ShapeDtypeStruct((rows, GROUP_W), F32)] * 5
        + [jax.ShapeDtypeStruct(s_t.shape, F32),
           jax.ShapeDtypeStruct((rows, 2 * SSM_S), F32),
           jax.ShapeDtypeStruct((rows, 2 * GROUP_W), F32)],
        input_output_aliases={s_arg: 5},
        compiler_params=_cparams(("parallel",)),
    )(za, zb, zc, zd, shift, s_t, ssm, conv, p["lng"], p["lnb"], p["w00"], p["b0"],
      *[p[n] for n in _S5_PARAMS], p["cw"], p["cb"], *[p[n] for n in _WKV_PARAMS])


def _out_ffn_kernel(x_ref, ya_ref, yb_ref, yc_ref, yd_ref, wo_ref, g2_ref, wg_ref, wu_ref, wd_ref,
                    gf_ref, o_ref, h_ref, *, final, ff_split):
    c = pl.program_id(1)

    @pl.when(c == 0)
    def _():
        x = x_ref[...]
        for i, y_ref in enumerate((ya_ref, yb_ref, yc_ref, yd_ref)):
            x = x + jnp.dot(y_ref[...].astype(BF16), wo_ref[i * GROUP_W:(i + 1) * GROUP_W, :],
                            preferred_element_type=F32)
        o_ref[...] = x
        h_ref[...] = _rms(x, g2_ref[...]).astype(BF16)

    h = h_ref[...]
    gate = jnp.dot(h, wg_ref[...], preferred_element_type=F32)
    up = jnp.dot(h, wu_ref[...], preferred_element_type=F32)
    act = (gate * jax.nn.sigmoid(gate) * up).astype(BF16)
    o_ref[...] += jnp.dot(act, wd_ref[...], preferred_element_type=F32)

    if final:
        @pl.when(c == ff_split - 1)
        def _():
            o_ref[...] = _rms(o_ref[...], gf_ref[...])


def _out_ffn(x, ys, p, l, tm, final, ff_split):
    rows = x.shape[0]
    fc = D_FF // ff_split
    row_blk = lambda wd_: pl.BlockSpec((tm, wd_), lambda i, c: (i, 0))
    return pl.pallas_call(
        functools.partial(_out_ffn_kernel, final=final, ff_split=ff_split),
        grid=(rows // tm, ff_split),
        in_specs=[row_blk(D_MODEL)] + [row_blk(GROUP_W)] * 4
        + [_layer((D_MODEL, D_MODEL), l), _layer((1, D_MODEL), l),
           pl.BlockSpec((None, D_MODEL, fc), lambda i, c: (l, 0, c)),
           pl.BlockSpec((None, D_MODEL, fc), lambda i, c: (l, 0, ff_split + c)),
           pl.BlockSpec((None, fc, D_MODEL), lambda i, c: (l, c, 0)),
           _full((1, D_MODEL))],
        out_specs=row_blk(D_MODEL),
        out_shape=jax.ShapeDtypeStruct((rows, D_MODEL), F32),
        scratch_shapes=[pltpu.VMEM((tm, D_MODEL), BF16)],
        compiler_params=_cparams(("parallel", "arbitrary")),
    )(x, *ys, p["w_out"], p["n2"], p["w_gu"], p["w_gu"], p["w_down"], p["gf"])


def _pad_lora(w, start):
    return jnp.pad(w.astype(BF16), ((0, 0), (start, LORA_PAD - start - w.shape[1]), (0, 0)))


def kernel(x_prompt, x_sample, state_wkv, state_shift, state_ssm_re, state_ssm_im, state_conv,
           norm1_g, w_in, gm_ln_g, gm_ln_b, gm_ws, gm_bs,
           ssm_a_re, ssm_a_im, ssm_log_dt, ssm_b_re, ssm_b_im, ssm_c_re, ssm_c_im, ssm_d,
           ssm_glu_w, ssm_glu_b, conv_w, conv_b,
           tm_mu, tm_w0, tm_w2, tm_a0, tm_a2, tm_g2, tm_k_k, tm_k_a, tm_r_k, tm_ln_g, tm_ln_b,
           w_out, norm2_g, ffn_w_gu, ffn_w_down, norm_f_g):
    depth = w_in.shape[0]
    bp, seq, _ = x_prompt.shape
    bs = x_sample.shape[0]
    assert x_sample.shape[1] == 1 and seq % GM_CHUNK == 0 and seq % WKV_CHUNK == 0
    head_d = GROUP_W // GM_HEADS
    rowv = lambda p: p.reshape(depth, 1, -1)

    lam, bb, cc = _s5_prep(ssm_a_re, ssm_a_im, ssm_log_dt, ssm_b_re, ssm_b_im, ssm_c_re, ssm_c_im)
    p = {
        "n1": rowv(norm1_g), "n2": rowv(norm2_g), "w_in": w_in.astype(BF16),
        "lng": rowv(gm_ln_g), "lnb": rowv(gm_ln_b),
        "wcat": jnp.transpose(gm_ws, (0, 2, 1, 3)).reshape(depth, GM_CHUNK, GM_HEADS * GM_CHUNK),
        "bias": jnp.repeat(jnp.transpose(gm_bs, (0, 2, 1)), head_d, axis=2),
        "w00": rowv(jnp.repeat(gm_ws[:, :, 0, 0], head_d, axis=1)),
        "b0": rowv(jnp.repeat(gm_bs[:, :, 0], head_d, axis=1)),
        "lam": lam, "bb": bb, "cc": cc, "ssm_d": rowv(ssm_d),
        "glu_w": ssm_glu_w.astype(BF16), "glu_b": rowv(ssm_glu_b),
        "cw": conv_w, "cb": rowv(conv_b),
        "mu": rowv(tm_mu), "w0": rowv(tm_w0), "w2": _pad_lora(tm_w2, 0),
        "a0": rowv(tm_a0), "a2": _pad_lora(tm_a2, 32), "g2": _pad_lora(tm_g2, 64),
        "k_k": rowv(tm_k_k), "k_a": rowv(tm_k_a), "r_k": rowv(tm_r_k),
        "ln_g": rowv(tm_ln_g), "ln_b": rowv(tm_ln_b),
        "w_out": w_out.astype(BF16), "w_gu": ffn_w_gu.astype(BF16), "w_down": ffn_w_down.astype(BF16),
        "gf": norm_f_g.reshape(1, D_MODEL),
    }

    xp = x_prompt.reshape(bp * seq, D_MODEL)
    xs = x_sample.reshape(bs, D_MODEL)
    wkv_s_buf = jnp.transpose(state_wkv, (0, 1, 3, 2, 4)).reshape(depth, bs, WKV_N, GROUP_W)
    ssm_s_in = jnp.concatenate([state_ssm_re.reshape(depth, bs, SSM_S),
                                state_ssm_im.reshape(depth, bs, SSM_S)], axis=-1)
    conv_s_in = state_conv.reshape(depth, bs, 2 * GROUP_W)

    outs = {k: [] for k in ("wkv_p", "sh_p", "sh_s", "ssm_p", "ssm_s", "cv_p", "cv_s", "chv")}
    tm_p = ROW_TILE if seq % ROW_TILE == 0 else GM_CHUNK
    bt_s = SAMPLE_TILE if bs % SAMPLE_TILE == 0 else bs
    for l in range(depth):
        final = l == depth - 1

        ya, zb, yc, zd, tail = _inproj_mix(xp, p, l, bp, seq, tm_p)
        per_batch = lambda y: y.reshape(bp, seq, y.shape[-1])
        yb, ssm_fin = _s5(per_batch(zb), p, l, bp, seq)
        xp3, wkv_fin = _rwkv_ffn(per_batch(zd), per_batch(xp), per_batch(ya), yb, per_batch(yc),
                                 p, l, bp, seq, final)
        xp = xp3.reshape(bp * seq, D_MODEL)
        outs["wkv_p"].append(wkv_fin)
        outs["sh_p"].append(zd.reshape(bp, seq, D_TM)[:, -1])
        outs["ssm_p"].append(ssm_fin)
        outs["cv_p"].append(tail[:, 6:8])

        za, zb, zc, zd = _inproj(xs, p["n1"], p["w_in"], l)
        ya, yb, yc, yd, vn, wkv_s_buf, ssm_new, conv_new = _sample_mix(
            za, zb, zc, zd, state_shift, wkv_s_buf, ssm_s_in, conv_s_in, p, l, bt_s)
        xs = _out_ffn(xs, (ya, yb, yc, yd), p, l, bs, final, FF_SPLIT_SAMPLE)
        outs["sh_s"].append(zd)
        outs["ssm_s"].append(ssm_new)
        outs["cv_s"].append(conv_new.reshape(bs, 2, GROUP_W))
        outs["chv"].append(vn.reshape(bs, 1, GROUP_W))

    def wkv_blocks(s_bd):
        s5d = s_bd.reshape(depth, -1, WKV_HEADS, WKV_N, WKV_HEADS, WKV_N)
        return jnp.stack([s5d[:, :, h, :, h, :] for h in range(WKV_HEADS)], axis=2)

    wkv_p = wkv_blocks(jnp.stack(outs["wkv_p"]))
    wkv_s = jnp.transpose(wkv_s_buf.reshape(depth, bs, WKV_N, WKV_HEADS, WKV_N), (0, 1, 3, 2, 4))
    ssm_p = jnp.stack(outs["ssm_p"])
    ssm_s = jnp.stack(outs["ssm_s"])
    split = lambda s, i: s[..., i * SSM_S:(i + 1) * SSM_S].reshape(depth, -1, SSM_GROUPS, SSM_P)
    return (xp.reshape(bp, seq, D_MODEL), xs.reshape(bs, 1, D_MODEL),
            wkv_p, wkv_s,
            jnp.stack(outs["sh_p"]), jnp.stack(outs["sh_s"]),
            split(ssm_p, 0), split(ssm_s, 0), split(ssm_p, 1), split(ssm_s, 1),
            jnp.stack(outs["cv_p"]), jnp.stack(outs["cv_s"]),
            jnp.stack(outs["chv"]))
```
